```python
import jax
import jax.numpy as jnp
from jax import lax
import numpy as np

D_MODEL = 1024
BATCH = 16
SEQ = 256
DEPTH = 4
DEC_BATCH = 2
DEC_SEQ = 1024
PAST_LEN = 256

GRID_W = 64
N_EVEN = (DEPTH + 1) // 2
N_ODD = DEPTH // 2
H_A = 8
DK_A = 64
DV_A = 64
CHUNK = 16
H_B = 8
Q_RANK = 256
KV_RANK = 128
D_NOPE = 64
D_ROPE = 32
DV_B = 64
ROPE_BASE = 10000.0
Q_BLOCK = 128
H_C = 16
N_C = 64
W_LORA = 64
A_LORA = 64
G_LORA = 128
D_FF = 2816
N_EXPERTS = 8
TOP_K = 2
D_FF_EXPERT = 2816
EPS = 1e-6
GN_EPS = 64e-5
HGRN_SIZES = (H_A * DK_A, H_A * DK_A, H_A * DK_A, H_A * DV_A, H_A * DV_A)
MLA_SIZES = (Q_RANK, KV_RANK, D_ROPE)
P_EVEN = sum(HGRN_SIZES) + sum(MLA_SIZES)
MIX_EVEN = H_A * DV_A + H_B * DV_B

kernel_name = 'hybrid_hgrn2_mla_rwkv7_prefix_dit'


def rms_norm(x, g):
    xf = x.astype(jnp.float32)
    y = xf * lax.rsqrt(jnp.mean(xf * xf, axis=-1, keepdims=True) + EPS)
    return (y * g.astype(jnp.float32)).astype(x.dtype)


def modulation(cond, w, b):
    mod = jax.nn.silu(cond) @ w + b
    return [m[:, None, :] for m in jnp.split(mod, 6, axis=-1)]


def axial_rope(rows):
    row, col = jnp.meshgrid(jnp.arange(rows), jnp.arange(GRID_W), indexing='ij')
    row = row.reshape(-1).astype(jnp.float32)
    col = col.reshape(-1).astype(jnp.float32)
    n_freq = D_ROPE // 4
    inv = 1.0 / (ROPE_BASE ** (jnp.arange(n_freq, dtype=jnp.float32) / n_freq))
    ar = row[:, None] * inv
    ac = col[:, None] * inv
    ang = jnp.concatenate([ar, ar, ac, ac], axis=-1)
    return jnp.cos(ang), jnp.sin(ang)


def rotate_axial(x):
    xb = x.reshape(x.shape[:-1] + (2, 2, D_ROPE // 4))
    rot = jnp.concatenate([-xb[..., 1:, :], xb[..., :1, :]], axis=-2)
    return rot.reshape(x.shape)


def apply_rope(x, cos, sin):
    return x * cos.astype(x.dtype) + rotate_axial(x) * sin.astype(x.dtype)


def blocked_attention(q, k, v, scale):
    bsz, t_len, n_h, d_q = q.shape
    nb = t_len // Q_BLOCK
    qb = jnp.moveaxis(q.reshape(bsz, nb, Q_BLOCK, n_h, d_q), 1, 0)

    def one_block(qi):
        s = jnp.einsum('bqhd,bkhd->bhqk', qi, k).astype(jnp.float32) * scale
        p = jax.nn.softmax(s, axis=-1).astype(v.dtype)
        return jnp.einsum('bhqk,bkhd->bqhd', p, v)

    o = lax.map(one_block, qb)
    return jnp.moveaxis(o, 0, 1).reshape(bsz, t_len, n_h, v.shape[-1])


def gla_chunkwise(q, k, v, logf, s0):
    bsz, t_len, n_h, _ = q.shape
    dv = v.shape[-1]
    nc = t_len // CHUNK

    def blk(t):
        return jnp.transpose(t.astype(jnp.float32).reshape(bsz, nc, CHUNK, n_h, t.shape[-1]), (0, 3, 1, 2, 4))

    q, k, v, logf = blk(q), blk(k), blk(v), blk(logf)
    bcum = jnp.cumsum(logf, axis=3)
    causal = jnp.tril(jnp.ones((CHUNK, CHUNK), dtype=bool))
    diff = bcum[:, :, :, :, None, :] - bcum[:, :, :, None, :, :]
    decay = jnp.exp(jnp.where(causal[:, :, None], diff, -jnp.inf))
    attn = jnp.einsum('bhnid,bhnjd,bhnijd->bhnij', q, k, decay)
    o_intra = jnp.einsum('bhnij,bhnjv->bhniv', attn, v)
    blast = bcum[:, :, :, -1, :]
    k_dec = k * jnp.exp(blast[:, :, :, None, :] - bcum)
    upd = jnp.einsum('bhnjd,bhnjv->bhndv', k_dec, v)

    def step(s, inp):
        d, u = inp
        return d[..., None] * s + u, s

    s_fin, s_prev = lax.scan(step, s0.astype(jnp.float32),
                             (jnp.moveaxis(jnp.exp(blast), 2, 0), jnp.moveaxis(upd, 2, 0)))
    s_prev = jnp.moveaxis(s_prev, 0, 2)
    o_inter = jnp.einsum('bhnid,bhndv->bhniv', q * jnp.exp(bcum), s_prev)
    o = jnp.transpose(o_intra + o_inter, (0, 2, 3, 1, 4)).reshape(bsz, t_len, n_h, dv)
    return o, s_fin


def hgrn_gates(logit, lb):
    logit = logit.astype(jnp.float32)
    logf = jnp.logaddexp(jnp.log(lb), jnp.log1p(-lb) + jax.nn.log_sigmoid(logit))
    key = (1.0 - lb) * jax.nn.sigmoid(-logit)
    return logf, key


def even_mixer(h, w_in, lb, hgrn_gain, q_gain, w_q_up, kv_gain, w_kv_up, w_out, s0_f, s0_b, ctx_ckv, ctx_krope, rope):
    bsz, t_len, _ = h.shape
    idx = np.cumsum(HGRN_SIZES + MLA_SIZES)[:-1].tolist()
    q_a, f_fw, f_bw, i_a, g_a, cq, ckv, krope = jnp.split(h @ w_in, idx, axis=-1)

    def heads(t):
        return t.reshape(bsz, t_len, H_A, -1)

    def flip(t):
        return jnp.flip(t, axis=1)

    logf_f, k_f = hgrn_gates(f_fw, lb[0])
    logf_b, k_b = hgrn_gates(f_bw, lb[1])
    qh, vh = heads(q_a), heads(i_a)
    o_f, sf_f = gla_chunkwise(qh, heads(k_f), vh, heads(logf_f), s0_f)
    o_b, sf_b = gla_chunkwise(flip(qh), flip(heads(k_b)), flip(vh), flip(heads(logf_b)), s0_b)
    o_a = rms_norm(o_f + flip(o_b), hgrn_gain).astype(h.dtype) * jax.nn.silu(heads(g_a))
    o_a = o_a.reshape(bsz, t_len, H_A * DV_A)

    q = (rms_norm(cq, q_gain) @ w_q_up).reshape(bsz, t_len, H_B, D_NOPE + D_ROPE)
    q_nope, q_rope = q[..., :D_NOPE], q[..., D_NOPE:]
    ckv = rms_norm(ckv, kv_gain)
    if rope is None:
        keys_ckv, keys_krope = ckv, krope
    else:
        cos, sin = rope
        q_rope = apply_rope(q_rope, cos[:, None, :], sin[:, None, :])
        keys_ckv = jnp.concatenate([ctx_ckv.astype(h.dtype), ckv], axis=1)
        keys_krope = jnp.concatenate([ctx_krope.astype(h.dtype), apply_rope(krope, cos, sin)], axis=1)
    n_keys = keys_ckv.shape[1]
    kv = (keys_ckv @ w_kv_up).reshape(bsz, n_keys, H_B, D_NOPE + DV_B)
    k_nope, v = kv[..., :D_NOPE], kv[..., D_NOPE:]
    k = jnp.concatenate([k_nope, jnp.broadcast_to(keys_krope[:, :, None, :], (bsz, n_keys, H_B, D_ROPE))], axis=-1)
    o_att = blocked_attention(jnp.concatenate([q_nope, q_rope], axis=-1), k, v, (D_NOPE + D_ROPE) ** -0.5)
    out = jnp.concatenate([o_a, o_att.reshape(bsz, t_len, H_B * DV_B)], axis=-1) @ w_out
    return out, (sf_f, sf_b, ckv, krope)


def centred_shift(x):
    prev = jnp.pad(x[:, :-1], ((0, 0), (1, 0), (0, 0)))
    nxt = jnp.pad(x[:, 1:], ((0, 0), (0, 1), (0, 0)))
    return 0.5 * (prev + nxt) - x


def rwkv7_scan(r, logw, k, v, kk, a, s0):
    def step(s, inp):
        r_t, lw_t, k_t, v_t, kk_t, a_t = inp
        sa = jnp.einsum('bhvk,bhk->bhv', s, -kk_t)
        s = (s * jnp.exp(lw_t)[:, :, None, :] + sa[..., None] * (kk_t * a_t)[:, :, None, :]
             + v_t[..., None] * k_t[:, :, None, :])
        return s, jnp.einsum('bhvk,bhk->bhv', s, r_t)

    xs = tuple(jnp.moveaxis(t, 1, 0) for t in (r, logw, k, v, kk, a))
    s_fin, y = lax.scan(step, s0.astype(jnp.float32), xs)
    return jnp.moveaxis(y, 0, 1), s_fin


def odd_mixer(h, mu, w_r, w_k, w_v, w_o, w0, w1, w2, a0, a1, a2, g1, g2, k_k, k_a, r_k, ln_w, ln_b, s0_f, s0_b):
    bsz, t_len, d = h.shape
    xx = centred_shift(h)
    xr, xw, xk, xv, xa, xg = [h + xx * mu[i] for i in range(6)]
    r = xr @ w_r
    k = xk @ w_k
    v = xv @ w_v
    a = jax.nn.sigmoid(a0 + (xa @ a1) @ a2)
    g = jax.nn.sigmoid(xg @ g1) @ g2

    def heads(t):
        return t.astype(jnp.float32).reshape(bsz, t_len, H_C, N_C)

    def log_decay(dr):
        wr = (w0[dr] + jnp.tanh(xw @ w1[dr]) @ w2[dr]).astype(jnp.float32)
        return heads(-jnp.exp(-jax.nn.softplus(-wr) - 0.5))

    def flip(t):
        return jnp.flip(t, axis=1)

    kk = heads(k * k_k)
    kk = kk * lax.rsqrt(jnp.sum(kk * kk, axis=-1, keepdims=True) + 1e-12)
    k = k * (1.0 + (a - 1.0) * k_a)
    rh, kh, vh, ah = heads(r), heads(k), heads(v), heads(a)
    y_f, sf_f = rwkv7_scan(rh, log_decay(0), kh, vh, kk, ah, s0_f)
    y_b, sf_b = rwkv7_scan(flip(rh), flip(log_decay(1)), flip(kh), flip(vh), flip(kk), flip(ah), s0_b)
    y = y_f + flip(y_b)
    mean = jnp.mean(y, axis=-1, keepdims=True)
    var = jnp.mean(jnp.square(y - mean), axis=-1, keepdims=True)
    y = ((y - mean) * lax.rsqrt(var + GN_EPS)).reshape(bsz, t_len, d) * ln_w + ln_b
    bonus = jnp.sum(rh * kh * r_k.reshape(H_C, N_C).astype(jnp.float32), axis=-1, keepdims=True) * vh
    y = (y + bonus.reshape(bsz, t_len, d)).astype(h.dtype)
    return (y * g) @ w_o, (sf_f, sf_b)


def swiglu(x, wg, wu, wd):
    return (jax.nn.silu(x @ wg) * (x @ wu)) @ wd


def moe_swiglu(x, w_router, wg, wu, wd):
    bsz, t_len, d = x.shape
    xt = x.reshape(-1, d)
    logits = (xt @ w_router).astype(jnp.float32)
    top_v, top_i = lax.top_k(logits, TOP_K)
    top_w = jax.nn.softmax(top_v, axis=-1)
    gate = jnp.sum(jax.nn.one_hot(top_i, N_EXPERTS, dtype=jnp.float32) * top_w[..., None], axis=1)
    out = jnp.zeros_like(xt)
    for e in range(N_EXPERTS):
        out = out + gate[:, e:e + 1].astype(x.dtype) * swiglu(xt, wg[e], wu[e], wd[e])
    return out.reshape(bsz, t_len, d)


def setup_inputs(seed: int = 0) -> dict:
    key = jax.random.key(seed)
    keys = jax.random.split(key, 64)
    counter = [0]

    def nxt():
        kk = keys[counter[0]]
        counter[0] += 1
        return kk

    def nrm(shape, scale):
        return scale * jax.random.normal(nxt(), shape, jnp.float32)

    def gain(shape):
        return 1.0 + nrm(shape, 0.05)

    D = D_MODEL
    return {
        'x_prompt': nrm((BATCH, SEQ, D), 1.0),
        'x_sample': nrm((DEC_BATCH, DEC_SEQ, D), 1.0),
        'cache_ckv': nrm((DEC_BATCH, N_EVEN, PAST_LEN, KV_RANK), 1.0),
        'cache_krope': nrm((DEC_BATCH, N_EVEN, PAST_LEN, D_ROPE), 1.0),
        'state_hgrn': nrm((DEC_BATCH, N_EVEN, 2, H_A, DK_A, DV_A), 0.5),
        'state_rwkv': nrm((DEC_BATCH, N_ODD, 2, H_C, N_C, N_C), 0.5),
        'c': nrm((DEC_BATCH, D), 1.0),
        'c_ctx': nrm((D,), 1.0),
        'ada_w': nrm((DEPTH, D, 6 * D), 0.5 * D ** -0.5),
        'ada_b': nrm((DEPTH, 6 * D), 0.02),
        'norm_gains': gain((DEPTH, 4, D)),
        'ev_w_in': nrm((N_EVEN, D, P_EVEN), D ** -0.5),
        'hgrn_lb_logits': nrm((N_EVEN, 2, H_A * DK_A), 0.5),
        'hgrn_norm': gain((N_EVEN, DV_A)),
        'mla_q_norm': gain((N_EVEN, Q_RANK)),
        'mla_w_q_up': nrm((N_EVEN, Q_RANK, H_B * (D_NOPE + D_ROPE)), Q_RANK ** -0.5),
        'mla_kv_norm': gain((N_EVEN, KV_RANK)),
        'mla_w_kv_up': nrm((N_EVEN, KV_RANK, H_B * (D_NOPE + DV_B)), KV_RANK ** -0.5),
        'ev_w_out': nrm((N_EVEN, MIX_EVEN, D), MIX_EVEN ** -0.5),
        'rw_mu': jax.random.uniform(nxt(), (N_ODD, 6, D), jnp.float32),
        'rw_w_r': nrm((N_ODD, D, D), D ** -0.5),
        'rw_w_k': nrm((N_ODD, D, D), D ** -0.5),
        'rw_w_v': nrm((N_ODD, D, D), D ** -0.5),
        'rw_w_o': nrm((N_ODD, D, D), D ** -0.5),
        'rw_w0': nrm((N_ODD, 2, D), 1.0) - 1.0,
        'rw_w1': nrm((N_ODD, 2, D, W_LORA), D ** -0.5),
        'rw_w2': nrm((N_ODD, 2, W_LORA, D), 0.5 * W_LORA ** -0.5),
        'rw_a0': nrm((N_ODD, D), 0.5),
        'rw_a1': nrm((N_ODD, D, A_LORA), D ** -0.5),
        'rw_a2': nrm((N_ODD, A_LORA, D), 0.5 * A_LORA ** -0.5),
        'rw_g1': nrm((N_ODD, D, G_LORA), D ** -0.5),
        'rw_g2': nrm((N_ODD, G_LORA, D), G_LORA ** -0.5),
        'rw_k_k': 0.85 + nrm((N_ODD, D), 0.05),
        'rw_k_a': gain((N_ODD, D)),
        'rw_r_k': nrm((N_ODD, D), 0.1),
        'rw_ln_w': gain((N_ODD, D)),
        'rw_ln_b': nrm((N_ODD, D), 0.02),
        'ffn_w_gate': nrm((N_EVEN, D, D_FF), D ** -0.5),
        'ffn_w_up': nrm((N_EVEN, D, D_FF), D ** -0.5),
        'ffn_w_down': nrm((N_EVEN, D_FF, D), D_FF ** -0.5),
        'moe_router': nrm((N_ODD, D, N_EXPERTS), D ** -0.5),
        'moe_w_gate': nrm((N_ODD, N_EXPERTS, D, D_FF_EXPERT), D ** -0.5),
        'moe_w_up': nrm((N_ODD, N_EXPERTS, D, D_FF_EXPERT), D ** -0.5),
        'moe_w_down': nrm((N_ODD, N_EXPERTS, D_FF_EXPERT, D), D_FF_EXPERT ** -0.5),
    }


def reference(x_prompt, x_sample, cache_ckv, cache_krope, state_hgrn, state_rwkv, c, c_ctx,
              ada_w, ada_b, norm_gains,
              ev_w_in, hgrn_lb_logits, hgrn_norm, mla_q_norm, mla_w_q_up, mla_kv_norm, mla_w_kv_up, ev_w_out,
              rw_mu, rw_w_r, rw_w_k, rw_w_v, rw_w_o, rw_w0, rw_w1, rw_w2, rw_a0, rw_a1, rw_a2, rw_g1, rw_g2,
              rw_k_k, rw_k_a, rw_r_k, rw_ln_w, rw_ln_b,
              ffn_w_gate, ffn_w_up, ffn_w_down, moe_router, moe_w_gate, moe_w_up, moe_w_down):
    lb_all = jnp.cumsum(jax.nn.softmax(hgrn_lb_logits.astype(jnp.float32), axis=0), axis=0)
    lb_all = lb_all - lb_all[:1]
    rows = x_sample.shape[1] // GRID_W
    rope = axial_rope(rows)
    cond_ctx = jnp.broadcast_to(c_ctx[None, :], (x_prompt.shape[0], D_MODEL))

    def layer(l, x, cond, latent):
        j = l // 2
        bsz = x.shape[0]
        sh_m, sc_m, gt_m, sh_f, sc_f, gt_f = modulation(cond, ada_w[l], ada_b[l])
        h = rms_norm(x, norm_gains[l, 0]) * (1.0 + sc_m) + sh_m
        if l % 2 == 0:
            if latent:
                s0_f, s0_b = state_hgrn[:, j, 0], state_hgrn[:, j, 1]
                ctx_ckv, ctx_krope, rp = cache_ckv[:, j], cache_krope[:, j], rope
            else:
                s0_f = jnp.zeros((bsz, H_A, DK_A, DV_A), jnp.float32)
                s0_b = s0_f
                ctx_ckv, ctx_krope, rp = None, None, None
            out, new = even_mixer(h, ev_w_in[j], lb_all[j], hgrn_norm[j], mla_q_norm[j], mla_w_q_up[j],
                                  mla_kv_norm[j], mla_w_kv_up[j], ev_w_out[j], s0_f, s0_b, ctx_ckv, ctx_krope, rp)
        else:
            if latent:
                s0_f, s0_b = state_rwkv[:, j, 0], state_rwkv[:, j, 1]
            else:
                s0_f = jnp.zeros((bsz, H_C, N_C, N_C), jnp.float32)
                s0_b = s0_f
            out, new = odd_mixer(h, rw_mu[j], rw_w_r[j], rw_w_k[j], rw_w_v[j], rw_w_o[j], rw_w0[j], rw_w1[j],
                                 rw_w2[j], rw_a0[j], rw_a1[j], rw_a2[j], rw_g1[j], rw_g2[j], rw_k_k[j], rw_k_a[j],
                                 rw_r_k[j], rw_ln_w[j], rw_ln_b[j], s0_f, s0_b)
        x = x + gt_m * rms_norm(out, norm_gains[l, 1])
        h = rms_norm(x, norm_gains[l, 2]) * (1.0 + sc_f) + sh_f
        if l % 2 == 0:
            ff = swiglu(h, ffn_w_gate[j], ffn_w_up[j], ffn_w_down[j])
        else:
            ff = moe_swiglu(h, moe_router[j], moe_w_gate[j], moe_w_up[j], moe_w_down[j])
        x = x + gt_f * rms_norm(ff, norm_gains[l, 3])
        return x, new

    y_prompt = x_prompt
    ckv_l, krope_l, hgrn_l, rwkv_l = [], [], [], []
    for l in range(DEPTH):
        y_prompt, new = layer(l, y_prompt, cond_ctx, False)
        if l % 2 == 0:
            hgrn_l.append(jnp.stack(new[:2], axis=1))
            ckv_l.append(new[2])
            krope_l.append(new[3])
        else:
            rwkv_l.append(jnp.stack(new, axis=1))
    dt = x_prompt.dtype
    new_ckv = jnp.stack(ckv_l, axis=1).astype(dt)
    new_krope = jnp.stack(krope_l, axis=1).astype(dt)
    new_hgrn = jnp.stack(hgrn_l, axis=1).astype(dt)
    new_rwkv = jnp.stack(rwkv_l, axis=1).astype(dt)

    y_sample = x_sample
    for l in range(DEPTH):
        y_sample, _ = layer(l, y_sample, c, True)

    return (y_prompt, y_sample, new_ckv, new_krope, new_hgrn, new_rwkv)
```

```python
import functools

import numpy as np
import jax
import jax.numpy as jnp
from jax import lax
from jax.experimental import pallas as pl
from jax.experimental.pallas import tpu as pltpu

D_MODEL = 1024
BATCH = 16
SEQ = 256
DEPTH = 4
DEC_BATCH = 2
DEC_SEQ = 1024
PAST_LEN = 256
GRID_W = 64
H_A = 8
DK_A = 64
DV_A = 64
H_B = 8
Q_RANK = 256
KV_RANK = 128
D_NOPE = 64
D_ROPE = 32
DV_B = 64
ROPE_BASE = 10000.0
H_C = 16
N_C = 64
D_FF = 2816
N_EXPERTS = 8
EPS = 1e-6
GN_EPS = 64e-5

F32 = jnp.float32
BF16 = jnp.bfloat16
HIGHEST = lax.Precision.HIGHEST

N_PROMPT = BATCH * SEQ
N_SAMPLE = DEC_BATCH * DEC_SEQ
N_TOK = N_PROMPT + N_SAMPLE
ROW_TILE = 256
N_TILES = N_TOK // ROW_TILE
LANES = 128
HEAD = 64
VMEM_LIMIT = 48 * 1024 * 1024
SCAN_TB = 32

P_QA, P_FF, P_FB, P_IA, P_GA, P_CQ, P_CKV, P_KR, P_KROT, P_EVEN_N = 0, 512, 1024, 1536, 2048, 2560, 2816, 2944, 3072, 3200


def _cparams(sem):
    return pltpu.CompilerParams(dimension_semantics=sem, vmem_limit_bytes=VMEM_LIMIT)


def _mod_kernel(c_ref, w_ref, b_ref, o_ref):
    c = c_ref[...]
    s = (c * jax.nn.sigmoid(c)).astype(BF16)
    o_ref[0] = jnp.dot(s, w_ref[0].astype(BF16), preferred_element_type=F32) + b_ref[0]


def _modulation(cond8, ada_w, ada_b):
    tn = 768
    return pl.pallas_call(
        _mod_kernel,
        out_shape=jax.ShapeDtypeStruct((DEPTH, 8, 6 * D_MODEL), F32),
        grid=(DEPTH, 6 * D_MODEL // tn),
        in_specs=[pl.BlockSpec((8, D_MODEL), lambda l, n: (0, 0)),
                  pl.BlockSpec((1, D_MODEL, tn), lambda l, n: (l, 0, n)),
                  pl.BlockSpec((1, 1, tn), lambda l, n: (l, 0, n))],
        out_specs=pl.BlockSpec((1, 8, tn), lambda l, n: (l, 0, n)),
        compiler_params=_cparams(("parallel", "parallel")),
    )(cond8, ada_w, ada_b.reshape(DEPTH, 1, 6 * D_MODEL))


def _rms(x, g):
    return x * lax.rsqrt(jnp.mean(x * x, axis=-1, keepdims=True) + EPS) * g


def _norm_mod_kernel(x_ref, g_ref, sc_ref, sh_ref, o_ref):
    h = _rms(x_ref[...], g_ref[...]) * (1.0 + sc_ref[0]) + sh_ref[0]
    o_ref[...] = h.astype(o_ref.dtype)


def _tile_spec():
    return pl.BlockSpec((1, 1, D_MODEL), lambda i: (i, 0, 0))


def _row_spec(width=D_MODEL):
    return pl.BlockSpec((ROW_TILE, width), lambda i: (i, 0))


def _vec_spec(width=D_MODEL):
    return pl.BlockSpec((1, width), lambda i: (0, 0))


def _norm_mod(x, g, sc, sh):
    return pl.pallas_call(
        _norm_mod_kernel,
        out_shape=jax.ShapeDtypeStruct((N_TOK, D_MODEL), BF16),
        grid=(N_TILES,),
        in_specs=[_row_spec(), _vec_spec(), _tile_spec(), _tile_spec()],
        out_specs=_row_spec(),
        compiler_params=_cparams(("parallel",)),
    )(x, g.reshape(1, D_MODEL), sc, sh)


def _norm_mod_router_kernel(x_ref, g_ref, sc_ref, sh_ref, wr_ref, o_ref, gate_ref):
    h = _rms(x_ref[...], g_ref[...]) * (1.0 + sc_ref[0]) + sh_ref[0]
    o_ref[...] = h.astype(o_ref.dtype)
    logits = jnp.dot(h, wr_ref[...], precision=HIGHEST, preferred_element_type=F32)
    lane = lax.broadcasted_iota(jnp.int32, logits.shape, 1).astype(F32)
    neg = jnp.float32(-jnp.inf)
    lg = jnp.where(lane < N_EXPERTS, logits, neg)
    m1 = jnp.max(lg, axis=-1, keepdims=True)
    i1 = jnp.min(jnp.where(lg == m1, lane, float(LANES)), axis=-1, keepdims=True)
    lg2 = jnp.where(lane == i1, neg, lg)
    m2 = jnp.max(lg2, axis=-1, keepdims=True)
    i2 = jnp.min(jnp.where(lg2 == m2, lane, float(LANES)), axis=-1, keepdims=True)
    e = jnp.exp(m2 - m1)
    w1 = 1.0 / (1.0 + e)
    w2 = e * w1
    gate_ref[...] = jnp.where(lane == i1, w1, jnp.where(lane == i2, w2, 0.0))


def _norm_mod_router(x, g, sc, sh, w_router):
    wr = jnp.pad(w_router, ((0, 0), (0, LANES - N_EXPERTS)))
    return pl.pallas_call(
        _norm_mod_router_kernel,
        out_shape=(jax.ShapeDtypeStruct((N_TOK, D_MODEL), BF16),
                   jax.ShapeDtypeStruct((N_TOK, LANES), F32)),
        grid=(N_TILES,),
        in_specs=[_row_spec(), _vec_spec(), _tile_spec(), _tile_spec(),
                  pl.BlockSpec((D_MODEL, LANES), lambda i: (0, 0))],
        out_specs=(_row_spec(), _row_spec(LANES)),
        compiler_params=_cparams(("parallel",)),
    )(x, g.reshape(1, D_MODEL), sc, sh, wr)


def _resid_norm_kernel(x_ref, y_ref, g_ref, gt_ref, o_ref):
    o_ref[...] = x_ref[...] + gt_ref[0] * _rms(y_ref[...], g_ref[...])


def _resid_norm(x, y, g, gt):
    return pl.pallas_call(
        _resid_norm_kernel,
        out_shape=jax.ShapeDtypeStruct((N_TOK, D_MODEL), F32),
        grid=(N_TILES,),
        in_specs=[_row_spec(), _row_spec(), _vec_spec(), _tile_spec()],
        out_specs=_row_spec(),
        compiler_params=_cparams(("parallel",)),
    )(x, y, g.reshape(1, D_MODEL), gt)


def _mm_kernel(a_ref, w_ref, o_ref):
    o_ref[...] = jnp.dot(a_ref[...].astype(BF16), w_ref[...], preferred_element_type=F32).astype(o_ref.dtype)


def _mm(a, w, out_dtype=F32, tm=512, tn=512):
    m, k = a.shape
    n = w.shape[1]
    tm, tn = min(tm, m), min(tn, n)
    return pl.pallas_call(
        _mm_kernel,
        out_shape=jax.ShapeDtypeStruct((m, n), out_dtype),
        grid=(n // tn, m // tm),
        in_specs=[pl.BlockSpec((tm, k), lambda j, i: (i, 0)),
                  pl.BlockSpec((k, tn), lambda j, i: (0, j))],
        out_specs=pl.BlockSpec((tm, tn), lambda j, i: (i, j)),
        compiler_params=_cparams(("parallel", "parallel")),
    )(a, w)


def _mm_acc_kernel(a_ref, w_ref, o_ref, acc_ref, *, nk):
    kk = pl.program_id(2)

    @pl.when(kk == 0)
    def _():
        acc_ref[...] = jnp.zeros_like(acc_ref)

    acc_ref[...] += jnp.dot(a_ref[...], w_ref[...], preferred_element_type=F32)

    @pl.when(kk == nk - 1)
    def _():
        o_ref[...] = acc_ref[...]


def _mm_acc(a, w, tm, tn, tk):
    m, k = a.shape
    n = w.shape[1]
    nk = k // tk
    return pl.pallas_call(
        functools.partial(_mm_acc_kernel, nk=nk),
        out_shape=jax.ShapeDtypeStruct((m, n), F32),
        grid=(n // tn, m // tm, nk),
        in_specs=[pl.BlockSpec((tm, tk), lambda j, i, kk: (i, kk)),
                  pl.BlockSpec((tk, tn), lambda j, i, kk: (kk, j))],
        out_specs=pl.BlockSpec((tm, tn), lambda j, i, kk: (i, j)),
        scratch_shapes=[pltpu.VMEM((tm, tn), F32)],
        compiler_params=_cparams(("parallel", "parallel", "arbitrary")),
    )(a, w)


def _mm_resid_kernel(a_ref, w_ref, x_ref, g_ref, gt_ref, o_ref):
    y = jnp.dot(a_ref[...], w_ref[...], preferred_element_type=F32)
    o_ref[...] = x_ref[...] + gt_ref[0] * _rms(y, g_ref[...])


def _mm_resid(a, w, x, g, gt):
    k = a.shape[1]
    return pl.pallas_call(
        _mm_resid_kernel,
        out_shape=jax.ShapeDtypeStruct((N_TOK, D_MODEL), F32),
        grid=(N_TILES,),
        in_specs=[_row_spec(k), pl.BlockSpec((k, D_MODEL), lambda i: (0, 0)),
                  _row_spec(), _vec_spec(), _tile_spec()],
        out_specs=_row_spec(),
        compiler_params=_cparams(("parallel",)),
    )(a, w, x, g.reshape(1, D_MODEL), gt)


def _swiglu_kernel(a_ref, wg_ref, wu_ref, gate_ref, o_ref):
    a = a_ref[...]
    g = jnp.dot(a, wg_ref[0], preferred_element_type=F32)
    u = jnp.dot(a, wu_ref[0], preferred_element_type=F32)
    o_ref[...] = (gate_ref[0] * (g * jax.nn.sigmoid(g)) * u).astype(o_ref.dtype)


def _swiglu(a, wg, wu, gate, tm=512, tn=1408):
    n_e, _, f = wg.shape
    m = a.shape[0]
    nt = f // tn
    return pl.pallas_call(
        _swiglu_kernel,
        out_shape=jax.ShapeDtypeStruct((m, n_e * f), BF16),
        grid=(n_e, nt, m // tm),
        in_specs=[pl.BlockSpec((tm, D_MODEL), lambda e, j, i: (i, 0)),
                  pl.BlockSpec((1, D_MODEL, tn), lambda e, j, i: (e, 0, j)),
                  pl.BlockSpec((1, D_MODEL, tn), lambda e, j, i: (e, 0, j)),
                  pl.BlockSpec((1, tm, 1), lambda e, j, i: (e, i, 0))],
        out_specs=pl.BlockSpec((tm, tn), lambda e, j, i: (i, e * nt + j)),
        compiler_params=_cparams(("parallel", "parallel", "parallel")),
    )(a, wg, wu, gate)


def _lora_kernel(x_ref, w1_ref, w2_ref, b_ref, o_ref, *, mid, out):
    t = mid(jnp.dot(x_ref[...], w1_ref[...], preferred_element_type=F32))
    o_ref[...] = out(jnp.dot(t.astype(BF16), w2_ref[...], preferred_element_type=F32) + b_ref[...])


def _lora(x, w1, w2, b, mid, out, tm=512):
    m = x.shape[0]
    r = w1.shape[1]
    return pl.pallas_call(
        functools.partial(_lora_kernel, mid=mid, out=out),
        out_shape=jax.ShapeDtypeStruct((m, D_MODEL), F32),
        grid=(m // tm,),
        in_specs=[pl.BlockSpec((tm, D_MODEL), lambda i: (i, 0)),
                  pl.BlockSpec((D_MODEL, r), lambda i: (0, 0)),
                  pl.BlockSpec((r, D_MODEL), lambda i: (0, 0)),
                  pl.BlockSpec((1, D_MODEL), lambda i: (0, 0))],
        out_specs=pl.BlockSpec((tm, D_MODEL), lambda i: (i, 0)),
        compiler_params=_cparams(("parallel",)),
    )(x, w1.astype(BF16), w2.astype(BF16), b.reshape(1, D_MODEL))


def _identity(x):
    return x


def _decay(wr):
    return jnp.exp(-float(np.exp(-0.5)) * jax.nn.sigmoid(wr))


def _block_ones():
    i = np.arange(LANES)
    return jnp.asarray((i[:, None] // HEAD == i[None, :] // HEAD).astype(np.float32))


def _head_sum(x, bo):
    parts = [jnp.dot(x[:, c * LANES:(c + 1) * LANES], bo, precision=HIGHEST, preferred_element_type=F32)
             for c in range(x.shape[1] // LANES)]
    return jnp.concatenate(parts, axis=-1)


def _hgrn_scan_kernel(q_ref, x_ref, v_ref, lb_ref, s0_ref, o_ref, s_ref, f_ref, k_ref, *, tb):
    @pl.when(pl.program_id(1) == 0)
    def _():
        s_ref[...] = s0_ref[...]

    lb = lb_ref[...]
    x = x_ref[...]
    f_ref[...] = lb + (1.0 - lb) * jax.nn.sigmoid(x)
    k_ref[...] = (1.0 - lb) * jax.nn.sigmoid(-x)

    def step(t, carry):
        vv = v_ref[t]
        acc = [jnp.zeros((HEAD, LANES), F32), jnp.zeros((HEAD, LANES), F32)]
        for d in range(HEAD):
            sn = s_ref[d] * f_ref[t, d:d + 1, :] + vv * k_ref[t, d:d + 1, :]
            s_ref[d] = sn
            acc[d % 2] = acc[d % 2] + sn * q_ref[t, d:d + 1, :]
        o_ref[t] = acc[0] + acc[1]
        return carry

    lax.fori_loop(0, tb, step, 0)


def _hgrn_scan(q, x, v, lb, s0):
    t_len, _, n = q.shape
    tb = SCAN_TB
    blk = pl.BlockSpec((tb, HEAD, LANES), lambda g, t: (t, 0, g))
    st = pl.BlockSpec((HEAD, HEAD, LANES), lambda g, t: (0, 0, g))
    return pl.pallas_call(
        functools.partial(_hgrn_scan_kernel, tb=tb),
        out_shape=(jax.ShapeDtypeStruct((t_len, HEAD, n), F32),
                   jax.ShapeDtypeStruct((HEAD, HEAD, n), F32)),
        grid=(n // LANES, t_len // tb),
        in_specs=[blk, blk, blk, pl.BlockSpec((HEAD, LANES), lambda g, t: (0, g)), st],
        out_specs=(blk, st),
        scratch_shapes=[pltpu.VMEM((tb, HEAD, LANES), F32), pltpu.VMEM((tb, HEAD, LANES), F32)],
        compiler_params=_cparams(("parallel", "arbitrary")),
    )(q, x, v, lb, s0)


def _rwkv_scan_kernel(r_ref, w_ref, k_ref, v_ref, kk_ref, a_ref, s0_ref, y_ref, s_ref, *, tb):
    @pl.when(pl.program_id(1) == 0)
    def _():
        s_ref[...] = s0_ref[...]

    def step(t, carry):
        acc = [jnp.zeros((HEAD, LANES), F32), jnp.zeros((HEAD, LANES), F32)]
        for d in range(HEAD):
            acc[d % 2] = acc[d % 2] + s_ref[d] * kk_ref[t, d:d + 1, :]
        sa = -(acc[0] + acc[1])
        vv = v_ref[t]
        acc = [jnp.zeros((HEAD, LANES), F32), jnp.zeros((HEAD, LANES), F32)]
        for d in range(HEAD):
            b = kk_ref[t, d:d + 1, :] * a_ref[t, d:d + 1, :]
            sn = s_ref[d] * w_ref[t, d:d + 1, :] + sa * b + vv * k_ref[t, d:d + 1, :]
            s_ref[d] = sn
            acc[d % 2] = acc[d % 2] + sn * r_ref[t, d:d + 1, :]
        y_ref[t] = acc[0] + acc[1]
        return carry

    lax.fori_loop(0, tb, step, 0)


def _rwkv_scan(r, w, k, v, kk, a, s0):
    t_len, _, n = r.shape
    tb = SCAN_TB
    blk = pl.BlockSpec((tb, HEAD, LANES), lambda g, t: (t, 0, g))
    st = pl.BlockSpec((HEAD, HEAD, LANES), lambda g, t: (0, 0, g))
    return pl.pallas_call(
        functools.partial(_rwkv_scan_kernel, tb=tb),
        out_shape=(jax.ShapeDtypeStruct((t_len, HEAD, n), F32),
                   jax.ShapeDtypeStruct((HEAD, HEAD, n), F32)),
        grid=(n // LANES, t_len // tb),
        in_specs=[blk] * 6 + [st],
        out_specs=(blk, st),
        compiler_params=_cparams(("parallel", "arbitrary")),
    )(r, w, k, v, kk, a, s0)


def _to_scan(x_f, x_b, bsz, t_len, n_h):
    def one(x, rev):
        x = x.reshape(bsz, t_len, n_h, HEAD)
        if rev:
            x = x[:, ::-1]
        return jnp.transpose(x, (1, 3, 0, 2)).reshape(t_len, HEAD, bsz * n_h)

    y = jnp.concatenate([one(x_f, False), one(x_b, True)], axis=-1)
    pad = (-y.shape[-1]) % LANES
    return jnp.pad(y, ((0, 0), (0, 0), (0, pad))) if pad else y


def _from_scan(y, bsz, t_len, n_h):
    y = y[:, :, :2 * bsz * n_h].reshape(t_len, HEAD, 2, bsz, n_h)
    y = jnp.transpose(y, (2, 3, 0, 4, 1))
    return (y[0].reshape(bsz * t_len, n_h * HEAD), y[1, :, ::-1].reshape(bsz * t_len, n_h * HEAD))


def _state_to_scan(s0, bsz, n_h, swap):
    perm = (4, 3, 1, 0, 2) if swap else (3, 4, 1, 0, 2)
    s = jnp.transpose(s0, perm).reshape(HEAD, HEAD, 2 * bsz * n_h)
    pad = (-s.shape[-1]) % LANES
    return jnp.pad(s, ((0, 0), (0, 0), (0, pad))) if pad else s


def _state_from_scan(s, bsz, n_h, swap):
    s = s[:, :, :2 * bsz * n_h].reshape(HEAD, HEAD, 2, bsz, n_h)
    perm = (3, 2, 4, 1, 0) if swap else (3, 2, 4, 0, 1)
    return jnp.transpose(s, perm)


def _hgrn_post_kernel(of_ref, ob_ref, g_ref, gain_ref, bo_ref, o_ref):
    o = of_ref[...] + ob_ref[...]
    ms = _head_sum(o * o, bo_ref[...]) * (1.0 / DV_A)
    y = o * lax.rsqrt(ms + EPS) * gain_ref[...]
    g = g_ref[...]
    o_ref[...] = (y * (g * jax.nn.sigmoid(g))).astype(o_ref.dtype)


def _hgrn_post(o_f, o_b, proj, gain):
    w = H_A * DV_A
    return pl.pallas_call(
        _hgrn_post_kernel,
        out_shape=jax.ShapeDtypeStruct((N_TOK, w), BF16),
        grid=(N_TILES,),
        in_specs=[_row_spec(w), _row_spec(w),
                  pl.BlockSpec((ROW_TILE, w), lambda i: (i, P_GA // w)),
                  _vec_spec(w), pl.BlockSpec((LANES, LANES), lambda i: (0, 0))],
        out_specs=_row_spec(w),
        compiler_params=_cparams(("parallel",)),
    )(o_f, o_b, proj, jnp.tile(gain, H_A).reshape(1, w), _block_ones())


def _q_up_kernel(cq_ref, gain_ref, w_ref, cos_ref, sin_ref, qn_ref, qr_ref):
    cq = _rms(cq_ref[...], gain_ref[...]).astype(BF16)
    q = jnp.dot(cq, w_ref[...], preferred_element_type=F32)
    qn_ref[...] = q[:, :512].astype(BF16)
    qr_ref[...] = (q[:, 512:768] * cos_ref[...] + q[:, 768:1024] * sin_ref[...]).astype(BF16)


def _q_up(proj, gain, w, cos, sin):
    return pl.pallas_call(
        _q_up_kernel,
        out_shape=(jax.ShapeDtypeStruct((N_TOK, 512), BF16), jax.ShapeDtypeStruct((N_TOK, 256), BF16)),
        grid=(N_TILES,),
        in_specs=[pl.BlockSpec((ROW_TILE, Q_RANK), lambda i: (i, P_CQ // Q_RANK)),
                  _vec_spec(Q_RANK), pl.BlockSpec((Q_RANK, 1024), lambda i: (0, 0)),
                  _row_spec(256), _row_spec(256)],
        out_specs=(_row_spec(512), _row_spec(256)),
        compiler_params=_cparams(("parallel",)),
    )(proj, gain.reshape(1, Q_RANK), w, cos, sin)


def _kv_up_kernel(ckv_ref, kr_ref, krot_ref, gain_ref, w_ref, cos_ref, sin_ref, ckvn_ref, kn_ref, v_ref, krope_ref):
    ckv = _rms(ckv_ref[...], gain_ref[...])
    ckvn_ref[...] = ckv
    kv = jnp.dot(ckv.astype(BF16), w_ref[...], preferred_element_type=F32)
    kn_ref[...] = kv[:, :512].astype(BF16)
    v_ref[...] = kv[:, 512:].astype(BF16)
    krope_ref[...] = (kr_ref[...] * cos_ref[...] + krot_ref[...] * sin_ref[...]).astype(BF16)


def _kv_up(proj, gain, w, cos4, sin4):
    return pl.pallas_call(
        _kv_up_kernel,
        out_shape=(jax.ShapeDtypeStruct((N_TOK, KV_RANK), F32), jax.ShapeDtypeStruct((N_TOK, 512), BF16),
                   jax.ShapeDtypeStruct((N_TOK, 512), BF16), jax.ShapeDtypeStruct((N_TOK, LANES), BF16)),
        grid=(N_TILES,),
        in_specs=[pl.BlockSpec((ROW_TILE, LANES), lambda i: (i, P_CKV // LANES)),
                  pl.BlockSpec((ROW_TILE, LANES), lambda i: (i, P_KR // LANES)),
                  pl.BlockSpec((ROW_TILE, LANES), lambda i: (i, P_KROT // LANES)),
                  _vec_spec(KV_RANK), pl.BlockSpec((KV_RANK, 1024), lambda i: (0, 0)),
                  _row_spec(LANES), _row_spec(LANES)],
        out_specs=(_row_spec(KV_RANK), _row_spec(512), _row_spec(512), _row_spec(LANES)),
        compiler_params=_cparams(("parallel",)),
    )(proj, proj, proj, gain.reshape(1, KV_RANK), w, cos4, sin4)


def _attn_kernel(qn_ref, qr_ref, kn_ref, kr_ref, v_ref, o_ref, *, scale):
    hp = pl.program_id(1)
    qn = qn_ref[0]
    qr = qr_ref[0]
    kcat = jnp.concatenate([kn_ref[0], kr_ref[0]], axis=-1)
    v = v_ref[0]
    lane = lax.broadcasted_iota(jnp.int32, (1, LANES), 1)
    zero = jnp.zeros((), BF16)
    outs = []
    for j in range(2):
        mn = (lane // D_NOPE) == j
        mr = (lane // D_ROPE) == (2 * hp + j) % 4
        qcat = jnp.concatenate([jnp.where(mn, qn, zero), jnp.where(mr, qr, zero)], axis=-1)
        s = lax.dot_general(qcat, kcat, (((1,), (1,)), ((), ())), preferred_element_type=F32) * scale
        p = jnp.exp(s - jnp.max(s, axis=-1, keepdims=True))
        l = jnp.sum(p, axis=-1, keepdims=True)
        outs.append(jnp.dot(p.astype(BF16), v, preferred_element_type=F32) / l)
    o_ref[0] = jnp.where(lane < DV_B, outs[0], outs[1]).astype(o_ref.dtype)


def _attention(qn, qr, kn, kr, v, tq=256):
    bsz, t_len, _ = qn.shape
    tk = kn.shape[1]
    return pl.pallas_call(
        functools.partial(_attn_kernel, scale=float((D_NOPE + D_ROPE) ** -0.5)),
        out_shape=jax.ShapeDtypeStruct((bsz, t_len, 512), BF16),
        grid=(bsz, 4, t_len // tq),
        in_specs=[pl.BlockSpec((1, tq, LANES), lambda b, h, q: (b, q, h)),
                  pl.BlockSpec((1, tq, LANES), lambda b, h, q: (b, q, h // 2)),
                  pl.BlockSpec((1, tk, LANES), lambda b, h, q: (b, 0, h)),
                  pl.BlockSpec((1, tk, LANES), lambda b, h, q: (b, 0, 0)),
                  pl.BlockSpec((1, tk, LANES), lambda b, h, q: (b, 0, h))],
        out_specs=pl.BlockSpec((1, tq, LANES), lambda b, h, q: (b, q, h)),
        compiler_params=_cparams(("parallel", "parallel", "parallel")),
    )(qn, qr, kn, kr, v)


def _odd_pre_kernel(x_ref, xp_ref, xn_ref, g_ref, sc_ref, sh_ref, mu_ref, *o_refs):
    i = pl.program_id(0)
    g, sc, sh = g_ref[...], sc_ref[0], sh_ref[0]

    def nm(x):
        return _rms(x, g) * (1.0 + sc) + sh

    h = nm(x_ref[...])
    in_latent = i >= BATCH
    part = (i - BATCH) % 4
    hp = jnp.where(jnp.logical_and(in_latent, part != 0), nm(xp_ref[7:8, :]), 0.0)
    hn = jnp.where(jnp.logical_and(in_latent, part != 3), nm(xn_ref[0:1, :]), 0.0)
    row = lax.broadcasted_iota(jnp.int32, (ROW_TILE, 1), 0)
    prev = jnp.where(row == 0, hp, pltpu.roll(h, 1, axis=0))
    nxt = jnp.where(row == ROW_TILE - 1, hn, pltpu.roll(h, ROW_TILE - 1, axis=0))
    xx = 0.5 * (prev + nxt) - h
    for j, o_ref in enumerate(o_refs):
        o_ref[...] = (h + xx * mu_ref[j:j + 1, :]).astype(o_ref.dtype)


def _odd_pre(x, g, sc, sh, mu):
    per = ROW_TILE // 8
    last = N_TOK // 8 - 1
    return pl.pallas_call(
        _odd_pre_kernel,
        out_shape=tuple(jax.ShapeDtypeStruct((N_TOK, D_MODEL), BF16) for _ in range(6)),
        grid=(N_TILES,),
        in_specs=[_row_spec(),
                  pl.BlockSpec((8, D_MODEL), lambda i: (jnp.maximum(i * per - 1, 0), 0)),
                  pl.BlockSpec((8, D_MODEL), lambda i: (jnp.minimum((i + 1) * per, last), 0)),
                  _vec_spec(), _tile_spec(), _tile_spec(),
                  pl.BlockSpec((8, D_MODEL), lambda i: (0, 0))],
        out_specs=tuple(_row_spec() for _ in range(6)),
        compiler_params=_cparams(("parallel",)),
    )(x, x, x, g.reshape(1, D_MODEL), sc, sh, jnp.pad(mu, ((0, 2), (0, 0))))


def _rwkv_prep_kernel(k_ref, a_ref, kk_w_ref, ka_w_ref, bo_ref, kk_ref, k2_ref):
    k = k_ref[...]
    kk = k * kk_w_ref[...]
    kk_ref[...] = kk * lax.rsqrt(_head_sum(kk * kk, bo_ref[...]) + 1e-12)
    k2_ref[...] = k * (1.0 + (a_ref[...] - 1.0) * ka_w_ref[...])


def _rwkv_prep(k, a, k_k, k_a):
    return pl.pallas_call(
        _rwkv_prep_kernel,
        out_shape=(jax.ShapeDtypeStruct((N_TOK, D_MODEL), F32), jax.ShapeDtypeStruct((N_TOK, D_MODEL), F32)),
        grid=(N_TILES,),
        in_specs=[_row_spec(), _row_spec(), _vec_spec(), _vec_spec(),
                  pl.BlockSpec((LANES, LANES), lambda i: (0, 0))],
        out_specs=(_row_spec(), _row_spec()),
        compiler_params=_cparams(("parallel",)),
    )(k, a, k_k.reshape(1, D_MODEL), k_a.reshape(1, D_MODEL), _block_ones())


def _rwkv_post_kernel(yf_ref, yb_ref, r_ref, k_ref, v_ref, g_ref, rk_ref, lnw_ref, lnb_ref, bo_ref, o_ref):
    bo = bo_ref[...]
    y = yf_ref[...] + yb_ref[...]
    d = y - _head_sum(y, bo) * (1.0 / N_C)
    var = _head_sum(d * d, bo) * (1.0 / N_C)
    yn = d * lax.rsqrt(var + GN_EPS) * lnw_ref[...] + lnb_ref[...]
    bonus = _head_sum(r_ref[...] * k_ref[...] * rk_ref[...], bo) * v_ref[...]
    o_ref[...] = ((yn + bonus) * g_ref[...]).astype(o_ref.dtype)


def _rwkv_post(y_f, y_b, r, k2, v, g, r_k, ln_w, ln_b):
    return pl.pallas_call(
        _rwkv_post_kernel,
        out_shape=jax.ShapeDtypeStruct((N_TOK, D_MODEL), BF16),
        grid=(N_TILES,),
        in_specs=[_row_spec()] * 6 + [_vec_spec()] * 3 + [pl.BlockSpec((LANES, LANES), lambda i: (0, 0))],
        out_specs=_row_spec(),
        compiler_params=_cparams(("parallel",)),
    )(y_f, y_b, r, k2, v, g, r_k.reshape(1, D_MODEL), ln_w.reshape(1, D_MODEL), ln_b.reshape(1, D_MODEL),
      _block_ones())


def _rot_cols(w):
    wb = w.reshape(w.shape[:-1] + (2, 2, D_ROPE // 4))
    return jnp.concatenate([-wb[..., 1:, :], wb[..., :1, :]], axis=-2).reshape(w.shape)


def _rope_tables():
    rows = DEC_SEQ // GRID_W
    row, col = np.meshgrid(np.arange(rows), np.arange(GRID_W), indexing='ij')
    row = jnp.asarray(row.reshape(-1), F32)
    col = jnp.asarray(col.reshape(-1), F32)
    n_freq = D_ROPE // 4
    inv = 1.0 / (ROPE_BASE ** (jnp.arange(n_freq, dtype=F32) / n_freq))
    ar = row[:, None] * inv
    ac = col[:, None] * inv
    ang = jnp.concatenate([ar, ar, ac, ac], axis=-1)
    cos, sin = jnp.cos(ang), jnp.sin(ang)
    cos = jnp.concatenate([jnp.ones((N_PROMPT, D_ROPE), F32), jnp.tile(cos, (DEC_BATCH, 1))], axis=0)
    sin = jnp.concatenate([jnp.zeros((N_PROMPT, D_ROPE), F32), jnp.tile(sin, (DEC_BATCH, 1))], axis=0)
    return cos, sin


def _split_seq(x):
    return x[:N_PROMPT], x[N_PROMPT:]


def _even_mixer(j, h, lb_all, cos, sin, cache_ckv, cache_krope, state_hgrn,
                ev_w_in, hgrn_norm, mla_q_norm, mla_w_q_up, mla_kv_norm, mla_w_kv_up):
    w_in = ev_w_in[j]
    w_main = w_in[:, :P_KR]
    w_kr = w_in[:, P_KR:]
    w_aug = jnp.concatenate([w_main, jnp.tile(w_kr, (1, 4)), jnp.tile(_rot_cols(w_kr), (1, 4))], axis=1).astype(BF16)
    proj = _mm(h, w_aug, F32, tm=512, tn=640)

    lb = lb_all[j]
    outs_f, outs_b, states = [], [], []
    for (rows, bsz, t_len, s0) in ((slice(0, N_PROMPT), BATCH, SEQ, None),
                                   (slice(N_PROMPT, N_TOK), DEC_BATCH, DEC_SEQ, state_hgrn[:, j])):
        p = proj[rows]
        q_s = _to_scan(p[:, P_QA:P_QA + 512], p[:, P_QA:P_QA + 512], bsz, t_len, H_A)
        x_s = _to_scan(p[:, P_FF:P_FF + 512], p[:, P_FB:P_FB + 512], bsz, t_len, H_A)
        v_s = _to_scan(p[:, P_IA:P_IA + 512], p[:, P_IA:P_IA + 512], bsz, t_len, H_A)
        n = q_s.shape[-1]
        lb_s = jnp.transpose(jnp.broadcast_to(lb.reshape(2, 1, H_A, HEAD), (2, bsz, H_A, HEAD)), (3, 0, 1, 2))
        lb_s = lb_s.reshape(HEAD, 2 * bsz * H_A)
        lb_s = jnp.pad(lb_s, ((0, 0), (0, n - lb_s.shape[-1])))
        if s0 is None:
            s0_s = jnp.zeros((HEAD, HEAD, n), F32)
        else:
            s0_s = _state_to_scan(s0, bsz, H_A, False)
        o_s, s_fin = _hgrn_scan(q_s, x_s, v_s, lb_s, s0_s)
        o_f, o_b = _from_scan(o_s, bsz, t_len, H_A)
        outs_f.append(o_f)
        outs_b.append(o_b)
        states.append(_state_from_scan(s_fin, bsz, H_A, False))
    o_a = _hgrn_post(jnp.concatenate(outs_f, 0), jnp.concatenate(outs_b, 0), proj, hgrn_norm[j])

    wq = mla_w_q_up[j].reshape(Q_RANK, H_B, D_NOPE + D_ROPE)
    wq_n = wq[:, :, :D_NOPE].reshape(Q_RANK, H_B * D_NOPE)
    wq_r = wq[:, :, D_NOPE:]
    wq_aug = jnp.concatenate([wq_n, wq_r.reshape(Q_RANK, -1), _rot_cols(wq_r).reshape(Q_RANK, -1)], axis=1).astype(BF16)
    qn, qr = _q_up(proj, mla_q_norm[j], wq_aug, jnp.tile(cos, (1, H_B)), jnp.tile(sin, (1, H_B)))
    wkv = mla_w_kv_up[j].reshape(KV_RANK, H_B, D_NOPE + DV_B)
    wkv_aug = jnp.concatenate([wkv[:, :, :D_NOPE].reshape(KV_RANK, -1), wkv[:, :, D_NOPE:].reshape(KV_RANK, -1)],
                              axis=1).astype(BF16)
    ckv_n, kn, vv, kr = _kv_up(proj, mla_kv_norm[j], wkv_aug, jnp.tile(cos, (1, 4)), jnp.tile(sin, (1, 4)))
    kv_ctx = _mm(cache_ckv[:, j].reshape(DEC_BATCH * PAST_LEN, KV_RANK), wkv_aug, BF16)
    kn_c = kv_ctx[:, :512].reshape(DEC_BATCH, PAST_LEN, 512)
    v_c = kv_ctx[:, 512:].reshape(DEC_BATCH, PAST_LEN, 512)
    kr_c = jnp.tile(cache_krope[:, j], (1, 1, 4)).astype(BF16)

    def seqs(x, bsz):
        return x.reshape(bsz, -1, x.shape[-1])

    (qn_p, qn_s), (qr_p, qr_s) = _split_seq(qn), _split_seq(qr)
    (kn_p, kn_s), (v_p, v_s), (kr_p, kr_s) = _split_seq(kn), _split_seq(vv), _split_seq(kr)
    o_p = _attention(seqs(qn_p, BATCH), seqs(qr_p, BATCH), seqs(kn_p, BATCH), seqs(kr_p, BATCH), seqs(v_p, BATCH))
    o_s = _attention(seqs(qn_s, DEC_BATCH), seqs(qr_s, DEC_BATCH),
                     jnp.concatenate([kn_c, seqs(kn_s, DEC_BATCH)], axis=1),
                     jnp.concatenate([kr_c, seqs(kr_s, DEC_BATCH)], axis=1),
                     jnp.concatenate([v_c, seqs(v_s, DEC_BATCH)], axis=1))
    o_att = jnp.concatenate([o_p.reshape(N_PROMPT, 512), o_s.reshape(N_SAMPLE, 512)], axis=0)
    mix = jnp.concatenate([o_a, o_att], axis=1)

    new_ckv = ckv_n[:N_PROMPT].reshape(BATCH, SEQ, KV_RANK)
    new_krope = proj[:N_PROMPT, P_KR:P_KR + D_ROPE].reshape(BATCH, SEQ, D_ROPE)
    return mix, (states[0], new_ckv, new_krope)


def _odd_mixer(j, x, g0, sc, sh, state_rwkv, rw_mu, rw_w_r, rw_w_k, rw_w_v, rw_w0, rw_w1, rw_w2, rw_a0, rw_a1, rw_a2,
               rw_g1, rw_g2, rw_k_k, rw_k_a, rw_r_k, rw_ln_w, rw_ln_b):
    xr, xw, xk, xv, xa, xg = _odd_pre(x, g0, sc, sh, rw_mu[j])
    r = _mm(xr, rw_w_r[j].astype(BF16))
    k = _mm(xk, rw_w_k[j].astype(BF16))
    v = _mm(xv, rw_w_v[j].astype(BF16))
    a = _lora(xa, rw_a1[j], rw_a2[j], rw_a0[j], _identity, jax.nn.sigmoid)
    g = _lora(xg, rw_g1[j], rw_g2[j], jnp.zeros((D_MODEL,), F32), jax.nn.sigmoid, _identity)
    w_f = _lora(xw, rw_w1[j, 0], rw_w2[j, 0], rw_w0[j, 0], jnp.tanh, _decay)
    w_b = _lora(xw, rw_w1[j, 1], rw_w2[j, 1], rw_w0[j, 1], jnp.tanh, _decay)
    kk, k2 = _rwkv_prep(k, a, rw_k_k[j], rw_k_a[j])

    ys_f, ys_b, states = [], [], []
    for (rows, bsz, t_len, s0) in ((slice(0, N_PROMPT), BATCH, SEQ, None),
                                   (slice(N_PROMPT, N_TOK), DEC_BATCH, DEC_SEQ, state_rwkv[:, j])):
        def sc2(t_f, t_b=None):
            t_b = t_f if t_b is None else t_b
            return _to_scan(t_f[rows], t_b[rows], bsz, t_len, H_C)

        r_s, w_s, k_s, v_s, kk_s, a_s = sc2(r), sc2(w_f, w_b), sc2(k2), sc2(v), sc2(kk), sc2(a)
        n = r_s.shape[-1]
        if s0 is None:
            s0_s = jnp.zeros((HEAD, HEAD, n), F32)
        else:
            s0_s = _state_to_scan(s0, bsz, H_C, True)
        y_s, s_fin = _rwkv_scan(r_s, w_s, k_s, v_s, kk_s, a_s, s0_s)
        y_f, y_b = _from_scan(y_s, bsz, t_len, H_C)
        ys_f.append(y_f)
        ys_b.append(y_b)
        states.append(_state_from_scan(s_fin, bsz, H_C, True))
    mix = _rwkv_post(jnp.concatenate(ys_f, 0), jnp.concatenate(ys_b, 0), r, k2, v, g,
                     rw_r_k[j], rw_ln_w[j], rw_ln_b[j])
    return mix, states[0]


def kernel(x_prompt, x_sample, cache_ckv, cache_krope, state_hgrn, state_rwkv, c, c_ctx, ada_w, ada_b, norm_gains, ev_w_in, hgrn_lb_logits, hgrn_norm, mla_q_norm, mla_w_q_up, mla_kv_norm, mla_w_kv_up, ev_w_out, rw_mu, rw_w_r, rw_w_k, rw_w_v, rw_w_o, rw_w0, rw_w1, rw_w2, rw_a0, rw_a1, rw_a2, rw_g1, rw_g2, rw_k_k, rw_k_a, rw_r_k, rw_ln_w, rw_ln_b, ffn_w_gate, ffn_w_up, ffn_w_down, moe_router, moe_w_gate, moe_w_up, moe_w_down):
    lb_all = jnp.cumsum(jax.nn.softmax(hgrn_lb_logits.astype(F32), axis=0), axis=0)
    lb_all = lb_all - lb_all[:1]
    cos, sin = _rope_tables()

    cond8 = jnp.zeros((8, D_MODEL), F32).at[0].set(c_ctx).at[1:1 + DEC_BATCH].set(c)
    mod = _modulation(cond8, ada_w, ada_b)
    tile_row = np.array([0] * BATCH + [1 + t // 4 for t in range(N_TILES - BATCH)])
    modt = mod[:, tile_row, :].reshape(DEPTH, N_TILES, 6, 1, D_MODEL)

    x = jnp.concatenate([x_prompt.reshape(N_PROMPT, D_MODEL), x_sample.reshape(N_SAMPLE, D_MODEL)], axis=0)
    ckv_l, krope_l, hgrn_l, rwkv_l = [], [], [], []
    for l in range(DEPTH):
        j = l // 2
        sh_m, sc_m, gt_m, sh_f, sc_f, gt_f = [modt[l, :, i] for i in range(6)]
        if l % 2 == 0:
            h = _norm_mod(x, norm_gains[l, 0], sc_m, sh_m)
            mix, (s_h, n_ckv, n_kr) = _even_mixer(j, h, lb_all, cos, sin, cache_ckv, cache_krope, state_hgrn,
                                                  ev_w_in, hgrn_norm, mla_q_norm, mla_w_q_up, mla_kv_norm, mla_w_kv_up)
            hgrn_l.append(s_h)
            ckv_l.append(n_ckv)
            krope_l.append(n_kr)
            x = _mm_resid(mix, ev_w_out[j].astype(BF16), x, norm_gains[l, 1], gt_m)
            h = _norm_mod(x, norm_gains[l, 2], sc_f, sh_f)
            ones = jnp.ones((1, N_TOK, 1), F32)
            hid = _swiglu(h, ffn_w_gate[j][None].astype(BF16), ffn_w_up[j][None].astype(BF16), ones)
            x = _mm_resid(hid, ffn_w_down[j].astype(BF16), x, norm_gains[l, 3], gt_f)
        else:
            mix, s_r = _odd_mixer(j, x, norm_gains[l, 0], sc_m, sh_m, state_rwkv, rw_mu, rw_w_r, rw_w_k, rw_w_v,
                                  rw_w0, rw_w1, rw_w2, rw_a0, rw_a1, rw_a2, rw_g1, rw_g2, rw_k_k, rw_k_a, rw_r_k,
                                  rw_ln_w, rw_ln_b)
            rwkv_l.append(s_r)
            x = _mm_resid(mix, rw_w_o[j].astype(BF16), x, norm_gains[l, 1], gt_m)
            h, gate = _norm_mod_router(x, norm_gains[l, 2], sc_f, sh_f, moe_router[j])
            gate_e = jnp.transpose(gate[:, :N_EXPERTS])[:, :, None]
            hid = _swiglu(h, moe_w_gate[j].astype(BF16), moe_w_up[j].astype(BF16), gate_e)
            ff = _mm_acc(hid, moe_w_down[j].reshape(N_EXPERTS * D_FF, D_MODEL).astype(BF16), 1024, 1024, D_FF)
            x = _resid_norm(x, ff, norm_gains[l, 3], gt_f)

    y_prompt = x[:N_PROMPT].reshape(BATCH, SEQ, D_MODEL)
    y_sample = x[N_PROMPT:].reshape(DEC_BATCH, DEC_SEQ, D_MODEL)
    return (y_prompt, y_sample, jnp.stack(ckv_l, axis=1), jnp.stack(krope_l, axis=1),
            jnp.stack(hgrn_l, axis=1), jnp.stack(rwkv_l, axis=1))
```

```python
import functools
from typing import NamedTuple

import numpy as np
import jax
import jax.numpy as jnp
from jax import lax
from jax.experimental import pallas as pl
from jax.experimental.pallas import tpu as pltpu

D_MODEL = 1024
BATCH = 16
SEQ = 256
DEPTH = 4
DEC_BATCH = 2
DEC_SEQ = 1024
PAST_LEN = 256
GRID_W = 64
H_A = 8
DK_A = 64
DV_A = 64
H_B = 8
Q_RANK = 256
KV_RANK = 128
D_NOPE = 64
D_ROPE = 32
DV_B = 64
ROPE_BASE = 10000.0
H_C = 16
N_C = 64
D_FF = 2816
N_EXPERTS = 8
EPS = 1e-6
GN_EPS = 64e-5

F32 = jnp.float32
BF16 = jnp.bfloat16
HIGHEST = lax.Precision.HIGHEST

N_PROMPT = BATCH * SEQ
N_SAMPLE = DEC_BATCH * DEC_SEQ
N_TOK = N_PROMPT + N_SAMPLE
ROW_TILE = 256
N_TILES = N_TOK // ROW_TILE
LANES = 128
HEAD = 64
VMEM_LIMIT = 48 * 1024 * 1024
HALO = 16

T_CQ, T_CKV, T_KR, T_KROT, T_N = 0, 256, 384, 512, 640


def _cparams(sem):
    return pltpu.CompilerParams(dimension_semantics=sem, vmem_limit_bytes=VMEM_LIMIT)


def _mod_kernel(c_ref, w_ref, b_ref, o_ref):
    c = c_ref[...]
    s = (c * jax.nn.sigmoid(c)).astype(BF16)
    o_ref[0] = jnp.dot(s, w_ref[0].astype(BF16), preferred_element_type=F32) + b_ref[0]


def _modulation(cond8, ada_w, ada_b):
    tn = 768
    return pl.pallas_call(
        _mod_kernel,
        out_shape=jax.ShapeDtypeStruct((DEPTH, 8, 6 * D_MODEL), F32),
        grid=(DEPTH, 6 * D_MODEL // tn),
        in_specs=[pl.BlockSpec((8, D_MODEL), lambda l, n: (0, 0)),
                  pl.BlockSpec((1, D_MODEL, tn), lambda l, n: (l, 0, n)),
                  pl.BlockSpec((1, 1, tn), lambda l, n: (l, 0, n))],
        out_specs=pl.BlockSpec((1, 8, tn), lambda l, n: (l, 0, n)),
        compiler_params=_cparams(("parallel", "parallel")),
        name="modulation",
    )(cond8, ada_w, ada_b.reshape(DEPTH, 1, 6 * D_MODEL))


def _rms(x, g):
    return x * lax.rsqrt(jnp.mean(x * x, axis=-1, keepdims=True) + EPS) * g


def _norm_mod_kernel(x_ref, g_ref, sc_ref, sh_ref, o_ref):
    h = _rms(x_ref[...], g_ref[...]) * (1.0 + sc_ref[0]) + sh_ref[0]
    o_ref[...] = h.astype(o_ref.dtype)


def _tile_spec():
    return pl.BlockSpec((1, 1, D_MODEL), lambda i: (i, 0, 0))


def _row_spec(width=D_MODEL, col=0):
    return pl.BlockSpec((ROW_TILE, width), lambda i: (i, col))


def _vec_spec(width=D_MODEL):
    return pl.BlockSpec((1, width), lambda i: (0, 0))


def _norm_mod(x, g, sc, sh):
    return pl.pallas_call(
        _norm_mod_kernel,
        out_shape=jax.ShapeDtypeStruct((N_TOK, D_MODEL), BF16),
        grid=(N_TILES,),
        in_specs=[_row_spec(), _vec_spec(), _tile_spec(), _tile_spec()],
        out_specs=_row_spec(),
        compiler_params=_cparams(("parallel",)),
        name="norm_mod",
    )(x, g.reshape(1, D_MODEL), sc, sh)


def _norm_mod_router_kernel(x_ref, g_ref, sc_ref, sh_ref, wr_ref, o_ref, gate_ref):
    h = _rms(x_ref[...], g_ref[...]) * (1.0 + sc_ref[0]) + sh_ref[0]
    o_ref[...] = h.astype(o_ref.dtype)
    logits = jnp.dot(h, wr_ref[...], precision=HIGHEST, preferred_element_type=F32)
    lane = lax.broadcasted_iota(jnp.int32, logits.shape, 1).astype(F32)
    neg = jnp.float32(-jnp.inf)
    lg = jnp.where(lane < N_EXPERTS, logits, neg)
    m1 = jnp.max(lg, axis=-1, keepdims=True)
    i1 = jnp.min(jnp.where(lg == m1, lane, float(LANES)), axis=-1, keepdims=True)
    lg2 = jnp.where(lane == i1, neg, lg)
    m2 = jnp.max(lg2, axis=-1, keepdims=True)
    i2 = jnp.min(jnp.where(lg2 == m2, lane, float(LANES)), axis=-1, keepdims=True)
    e = jnp.exp(m2 - m1)
    w1 = 1.0 / (1.0 + e)
    w2 = e * w1
    gate_ref[...] = jnp.where(lane == i1, w1, jnp.where(lane == i2, w2, 0.0))


def _norm_mod_router(x, g, sc, sh, w_router):
    wr = jnp.pad(w_router, ((0, 0), (0, LANES - N_EXPERTS)))
    return pl.pallas_call(
        _norm_mod_router_kernel,
        out_shape=(jax.ShapeDtypeStruct((N_TOK, D_MODEL), BF16),
                   jax.ShapeDtypeStruct((N_TOK, LANES), F32)),
        grid=(N_TILES,),
        in_specs=[_row_spec(), _vec_spec(), _tile_spec(), _tile_spec(),
                  pl.BlockSpec((D_MODEL, LANES), lambda i: (0, 0))],
        out_specs=(_row_spec(), _row_spec(LANES)),
        compiler_params=_cparams(("parallel",)),
        name="norm_mod_router",
    )(x, g.reshape(1, D_MODEL), sc, sh, wr)


def _resid_norm_kernel(x_ref, y_ref, g_ref, gt_ref, o_ref):
    o_ref[...] = x_ref[...] + gt_ref[0] * _rms(y_ref[...], g_ref[...])


def _resid_norm(x, y, g, gt):
    return pl.pallas_call(
        _resid_norm_kernel,
        out_shape=jax.ShapeDtypeStruct((N_TOK, D_MODEL), F32),
        grid=(N_TILES,),
        in_specs=[_row_spec(), _row_spec(), _vec_spec(), _tile_spec()],
        out_specs=_row_spec(),
        compiler_params=_cparams(("parallel",)),
        name="resid_norm",
    )(x, y, g.reshape(1, D_MODEL), gt)


def _mm_kernel(a_ref, w_ref, o_ref):
    o_ref[...] = jnp.dot(a_ref[...].astype(BF16), w_ref[...], preferred_element_type=F32).astype(o_ref.dtype)


def _mm(a, w, out_dtype=F32, tm=512, tn=512, name="mm"):
    m, k = a.shape
    n = w.shape[1]
    tm, tn = min(tm, m), min(tn, n)
    return pl.pallas_call(
        _mm_kernel,
        out_shape=jax.ShapeDtypeStruct((m, n), out_dtype),
        grid=(n // tn, m // tm),
        in_specs=[pl.BlockSpec((tm, k), lambda j, i: (i, 0)),
                  pl.BlockSpec((k, tn), lambda j, i: (0, j))],
        out_specs=pl.BlockSpec((tm, tn), lambda j, i: (i, j)),
        compiler_params=_cparams(("parallel", "parallel")),
        name=name,
    )(a, w)


def _mm_split(a, w, tn, tm=512, name="mm_split"):
    m, k = a.shape
    p = w.shape[1] // tn
    return pl.pallas_call(
        _mm_kernel,
        out_shape=jax.ShapeDtypeStruct((p, m, tn), F32),
        grid=(p, m // tm),
        in_specs=[pl.BlockSpec((tm, k), lambda j, i: (i, 0)),
                  pl.BlockSpec((k, tn), lambda j, i: (0, j))],
        out_specs=pl.BlockSpec((None, tm, tn), lambda j, i: (j, i, 0)),
        compiler_params=_cparams(("parallel", "parallel")),
        name=name,
    )(a, w)


def _mm_acc_kernel(a_ref, w_ref, o_ref, acc_ref, *, nk):
    kk = pl.program_id(2)

    @pl.when(kk == 0)
    def _():
        acc_ref[...] = jnp.zeros_like(acc_ref)

    acc_ref[...] += jnp.dot(a_ref[...], w_ref[...], preferred_element_type=F32)

    @pl.when(kk == nk - 1)
    def _():
        o_ref[...] = acc_ref[...]


def _mm_acc(a, w, tm, tn, tk):
    m, k = a.shape
    n = w.shape[1]
    nk = k // tk
    return pl.pallas_call(
        functools.partial(_mm_acc_kernel, nk=nk),
        out_shape=jax.ShapeDtypeStruct((m, n), F32),
        grid=(n // tn, m // tm, nk),
        in_specs=[pl.BlockSpec((tm, tk), lambda j, i, kk: (i, kk)),
                  pl.BlockSpec((tk, tn), lambda j, i, kk: (kk, j))],
        out_specs=pl.BlockSpec((tm, tn), lambda j, i, kk: (i, j)),
        scratch_shapes=[pltpu.VMEM((tm, tn), F32)],
        compiler_params=_cparams(("parallel", "parallel", "arbitrary")),
        name="mm_acc",
    )(a, w)


def _mm_resid_kernel(a_ref, w_ref, x_ref, g_ref, gt_ref, o_ref):
    y = jnp.dot(a_ref[...], w_ref[...], preferred_element_type=F32)
    o_ref[...] = x_ref[...] + gt_ref[0] * _rms(y, g_ref[...])


def _mm_resid(a, w, x, g, gt):
    k = a.shape[1]
    return pl.pallas_call(
        _mm_resid_kernel,
        out_shape=jax.ShapeDtypeStruct((N_TOK, D_MODEL), F32),
        grid=(N_TILES,),
        in_specs=[_row_spec(k), pl.BlockSpec((k, D_MODEL), lambda i: (0, 0)),
                  _row_spec(), _vec_spec(), _tile_spec()],
        out_specs=_row_spec(),
        compiler_params=_cparams(("parallel",)),
        name="mm_resid",
    )(a, w, x, g.reshape(1, D_MODEL), gt)


def _mm2_resid_kernel(a1_ref, a2_ref, w_ref, x_ref, g_ref, gt_ref, o_ref):
    k1 = a1_ref.shape[1]
    y = (jnp.dot(a1_ref[...], w_ref[:k1], preferred_element_type=F32)
         + jnp.dot(a2_ref[...], w_ref[k1:], preferred_element_type=F32))
    o_ref[...] = x_ref[...] + gt_ref[0] * _rms(y, g_ref[...])


def _mm2_resid(a1, a2, w, x, g, gt):
    k1, k2 = a1.shape[1], a2.shape[1]
    return pl.pallas_call(
        _mm2_resid_kernel,
        out_shape=jax.ShapeDtypeStruct((N_TOK, D_MODEL), F32),
        grid=(N_TILES,),
        in_specs=[_row_spec(k1), _row_spec(k2), pl.BlockSpec((k1 + k2, D_MODEL), lambda i: (0, 0)),
                  _row_spec(), _vec_spec(), _tile_spec()],
        out_specs=_row_spec(),
        compiler_params=_cparams(("parallel",)),
        name="mm2_resid",
    )(a1, a2, w, x, g.reshape(1, D_MODEL), gt)


def _swiglu_kernel(a_ref, wg_ref, wu_ref, gate_ref, o_ref):
    a = a_ref[...]
    g = jnp.dot(a, wg_ref[0], preferred_element_type=F32)
    u = jnp.dot(a, wu_ref[0], preferred_element_type=F32)
    o_ref[...] = (gate_ref[0] * (g * jax.nn.sigmoid(g)) * u).astype(o_ref.dtype)


def _swiglu(a, wg, wu, gate, tm=512, tn=1408):
    n_e, _, f = wg.shape
    m = a.shape[0]
    nt = f // tn
    return pl.pallas_call(
        _swiglu_kernel,
        out_shape=jax.ShapeDtypeStruct((m, n_e * f), BF16),
        grid=(n_e, nt, m // tm),
        in_specs=[pl.BlockSpec((tm, D_MODEL), lambda e, j, i: (i, 0)),
                  pl.BlockSpec((1, D_MODEL, tn), lambda e, j, i: (e, 0, j)),
                  pl.BlockSpec((1, D_MODEL, tn), lambda e, j, i: (e, 0, j)),
                  pl.BlockSpec((1, tm, 1), lambda e, j, i: (e, i, 0))],
        out_specs=pl.BlockSpec((tm, tn), lambda e, j, i: (i, e * nt + j)),
        compiler_params=_cparams(("parallel", "parallel", "parallel")),
        name="swiglu",
    )(a, wg, wu, gate)


def _lora_kernel(x_ref, w1_ref, w2_ref, b_ref, o_ref, *, mid, out):
    t = mid(jnp.dot(x_ref[...], w1_ref[...], preferred_element_type=F32))
    o_ref[...] = out(jnp.dot(t.astype(BF16), w2_ref[...], preferred_element_type=F32) + b_ref[...])


def _lora(x, w1, w2, b, mid, out, tm=512):
    m = x.shape[0]
    r = w1.shape[1]
    return pl.pallas_call(
        functools.partial(_lora_kernel, mid=mid, out=out),
        out_shape=jax.ShapeDtypeStruct((m, D_MODEL), F32),
        grid=(m // tm,),
        in_specs=[pl.BlockSpec((tm, D_MODEL), lambda i: (i, 0)),
                  pl.BlockSpec((D_MODEL, r), lambda i: (0, 0)),
                  pl.BlockSpec((r, D_MODEL), lambda i: (0, 0)),
                  pl.BlockSpec((1, D_MODEL), lambda i: (0, 0))],
        out_specs=pl.BlockSpec((tm, D_MODEL), lambda i: (i, 0)),
        compiler_params=_cparams(("parallel",)),
        name="lora",
    )(x, w1.astype(BF16), w2.astype(BF16), b.reshape(1, D_MODEL))


def _identity(x):
    return x


def _decay(wr):
    return jnp.exp(-float(np.exp(-0.5)) * jax.nn.sigmoid(wr))


def _block_ones():
    i = np.arange(LANES)
    return jnp.asarray((i[:, None] // HEAD == i[None, :] // HEAD).astype(np.float32))


def _head_sum(x, bo):
    parts = [jnp.dot(x[:, c * LANES:(c + 1) * LANES], bo, precision=HIGHEST, preferred_element_type=F32)
             for c in range(x.shape[1] // LANES)]
    return jnp.concatenate(parts, axis=-1)


class _ScanMode(NamedTuple):
    prompt: bool
    groups: int
    rep: int
    r2: int
    tb: int


def _scan_gather(mode, src_ref, fwd_ref, bwd_ref, s):
    sb = mode.tb - 1 - s
    if mode.prompt:
        src_ref[0:HEAD, :] = fwd_ref[s]
        src_ref[HEAD:2 * HEAD, :] = bwd_ref[sb]
        return src_ref[...].T
    unit = 4 * mode.r2
    for q in range(mode.rep):
        for d, (ref, tt) in enumerate(((fwd_ref, s), (bwd_ref, sb))):
            for b in range(DEC_BATCH):
                r0 = q * unit + (2 * d + b) * mode.r2
                src_ref[r0:r0 + mode.r2, 0:HEAD] = ref[b, tt]
    return src_ref[...].T[0:HEAD]


def _scan_value_slab(mode, vt):
    vsub = HEAD // mode.rep
    out = vt[0:vsub]
    if mode.rep > 1:
        lane = lax.broadcasted_iota(jnp.int32, (vsub, LANES), 1)
        for q in range(1, mode.rep):
            out = jnp.where(lane >= q * (LANES // mode.rep), vt[q * vsub:(q + 1) * vsub], out)
    return out


def _scan_scatter(mode, src_ref, ys_ref, yf_ref, yb_ref, s):
    sb = mode.tb - 1 - s
    if mode.prompt:
        tr = jnp.concatenate([ys_ref[s, 0], ys_ref[s, 1]], axis=0).T
        yf_ref[s] = tr[0:HEAD]
        yb_ref[sb] = tr[HEAD:2 * HEAD]
        return
    vsub = HEAD // mode.rep
    y = ys_ref[s, 0]
    for q in range(mode.rep):
        src_ref[q * vsub:(q + 1) * vsub, :] = y
    tr = src_ref[...].T
    unit = 4 * mode.r2
    out = tr[0:unit]
    lane = lax.broadcasted_iota(jnp.int32, (unit, LANES), 1)
    for q in range(1, mode.rep):
        out = jnp.where(lane >= q * vsub, tr[q * unit:(q + 1) * unit], out)
    for d, (ref, tt) in enumerate(((yf_ref, s), (yb_ref, sb))):
        for b in range(DEC_BATCH):
            r0 = (2 * d + b) * mode.r2
            ref[b, tt] = out[r0:r0 + mode.r2, 0:HEAD]


def _scan_init(mode, t_axis, s_ref, s0_ref, src_ref):
    @pl.when(pl.program_id(t_axis) == 0)
    def _():
        if s0_ref is None:
            s_ref[...] = jnp.zeros_like(s_ref)
        else:
            s_ref[...] = s0_ref[...]

    if not mode.prompt:
        src_ref[...] = jnp.zeros_like(src_ref)


def _hgrn_scan_kernel(*refs, mode, has_s0, t_axis):
    qf, qb, xf, xb, vf, vb, lb_ref = refs[:7]
    refs = refs[7:]
    s0_ref = None
    if has_s0:
        s0_ref, refs = refs[0], refs[1:]
    if not mode.prompt:
        refs = refs[2:]
    of_ref, ob_ref, s_ref, src_ref, q_t, f_t, k_t, v_t, ys_ref = refs
    vsub = HEAD // mode.rep
    _scan_init(mode, t_axis, s_ref, s0_ref, src_ref)
    lb = lb_ref[...]

    def prologue(s, carry):
        q_t[s] = _scan_gather(mode, src_ref, qf, qb, s)
        x = _scan_gather(mode, src_ref, xf, xb, s)
        f_t[s] = lb + (1.0 - lb) * jax.nn.sigmoid(x)
        k_t[s] = (1.0 - lb) * jax.nn.sigmoid(-x)
        v_t[s] = _scan_gather(mode, src_ref, vf, vb, s)
        return carry

    lax.fori_loop(0, mode.tb, prologue, 0)

    def step(s, carry):
        for g in range(mode.groups):
            r0 = g * HEAD
            vv = _scan_value_slab(mode, v_t[s, r0:r0 + HEAD, :])
            acc = [jnp.zeros((vsub, LANES), F32), jnp.zeros((vsub, LANES), F32)]
            for d in range(HEAD):
                r = r0 + d
                sn = s_ref[g, d] * f_t[s, r:r + 1, :] + vv * k_t[s, r:r + 1, :]
                s_ref[g, d] = sn
                acc[d % 2] = acc[d % 2] + sn * q_t[s, r:r + 1, :]
            ys_ref[s, g] = acc[0] + acc[1]
        return carry

    lax.fori_loop(0, mode.tb, step, 0)

    def epilogue(s, carry):
        _scan_scatter(mode, src_ref, ys_ref, of_ref, ob_ref, s)
        return carry

    lax.fori_loop(0, mode.tb, epilogue, 0)


def _rwkv_scan_kernel(*refs, mode, has_s0, t_axis):
    ins = refs[:12]
    refs = refs[12:]
    s0_ref = None
    if has_s0:
        s0_ref, refs = refs[0], refs[1:]
    if not mode.prompt:
        refs = refs[2:]
    yf_ref, yb_ref, s_ref, src_ref, r_t, w_t, k_t, v_t, kk_t, a_t, ys_ref = refs
    vsub = HEAD // mode.rep
    _scan_init(mode, t_axis, s_ref, s0_ref, src_ref)

    def prologue(s, carry):
        for i, x_t in enumerate((r_t, w_t, k_t, v_t, kk_t, a_t)):
            x_t[s] = _scan_gather(mode, src_ref, ins[2 * i], ins[2 * i + 1], s)
        return carry

    lax.fori_loop(0, mode.tb, prologue, 0)

    def step(s, carry):
        for g in range(mode.groups):
            r0 = g * HEAD
            acc = [jnp.zeros((vsub, LANES), F32), jnp.zeros((vsub, LANES), F32)]
            for d in range(HEAD):
                acc[d % 2] = acc[d % 2] + s_ref[g, d] * kk_t[s, r0 + d:r0 + d + 1, :]
            sa = -(acc[0] + acc[1])
            vv = _scan_value_slab(mode, v_t[s, r0:r0 + HEAD, :])
            acc = [jnp.zeros((vsub, LANES), F32), jnp.zeros((vsub, LANES), F32)]
            for d in range(HEAD):
                r = r0 + d
                b = kk_t[s, r:r + 1, :] * a_t[s, r:r + 1, :]
                sn = s_ref[g, d] * w_t[s, r:r + 1, :] + sa * b + vv * k_t[s, r:r + 1, :]
                s_ref[g, d] = sn
                acc[d % 2] = acc[d % 2] + sn * r_t[s, r:r + 1, :]
            ys_ref[s, g] = acc[0] + acc[1]
        return carry

    lax.fori_loop(0, mode.tb, step, 0)

    def epilogue(s, carry):
        _scan_scatter(mode, src_ref, ys_ref, yf_ref, yb_ref, s)
        return carry

    lax.fori_loop(0, mode.tb, epilogue, 0)


def _scan_scratch(mode, n_streams):
    rows = mode.groups * HEAD
    vsub = HEAD // mode.rep
    return ([pltpu.VMEM((LANES, LANES), F32)]
            + [pltpu.VMEM((mode.tb, rows, LANES), F32) for _ in range(n_streams)]
            + [pltpu.VMEM((mode.tb, mode.groups, vsub, LANES), F32)])


def _hgrn_scans(slabs, lb, state0):
    mode = _ScanMode(prompt=True, groups=2, rep=1, r2=0, tb=16)
    n_t = SEQ // mode.tb
    view = slabs.reshape(5, N_TOK * 4 // HEAD, HEAD, LANES)

    def spec(p, rev):
        if rev:
            return pl.BlockSpec((None, mode.tb, HEAD, LANES), lambda t: (p, n_t - 1 - t, 0, 0))
        return pl.BlockSpec((None, mode.tb, HEAD, LANES), lambda t: (p, t, 0, 0))

    def ospec(rev):
        if rev:
            return pl.BlockSpec((mode.tb, HEAD, LANES), lambda t: (n_t - 1 - t, 0, 0))
        return pl.BlockSpec((mode.tb, HEAD, LANES), lambda t: (t, 0, 0))

    lb4 = lb.reshape(2, H_A // 2, 2, HEAD)
    lb_p = jnp.broadcast_to(jnp.transpose(lb4, (2, 3, 0, 1))[:, :, :, None, :], (2, HEAD, 2, BATCH, H_A // 2))
    lb_p = lb_p.reshape(2 * HEAD, LANES)
    st_spec = pl.BlockSpec((2, HEAD, HEAD, LANES), lambda t: (0, 0, 0, 0))
    o_shape = jax.ShapeDtypeStruct((N_TOK * 4 // HEAD, HEAD, LANES), F32)
    o_f, o_b, s_fin = pl.pallas_call(
        functools.partial(_hgrn_scan_kernel, mode=mode, has_s0=False, t_axis=0),
        out_shape=(o_shape, o_shape, jax.ShapeDtypeStruct((2, HEAD, HEAD, LANES), F32)),
        grid=(n_t,),
        in_specs=[spec(0, False), spec(0, True), spec(1, False), spec(2, True), spec(3, False), spec(3, True),
                  pl.BlockSpec((2 * HEAD, LANES), lambda t: (0, 0))],
        out_specs=(ospec(False), ospec(True), st_spec),
        scratch_shapes=_scan_scratch(mode, 4),
        compiler_params=_cparams(("arbitrary",)),
        name="hgrn_scan_prompt",
    )(view, view, view, view, view, view, lb_p)
    s_fin = jnp.transpose(s_fin.reshape(2, HEAD, HEAD, 2, BATCH, H_A // 2), (4, 3, 5, 0, 1, 2))
    s_fin = s_fin.reshape(BATCH, 2, H_A, DK_A, DV_A)

    mode = _ScanMode(prompt=False, groups=1, rep=4, r2=H_A, tb=32)
    n_t = DEC_SEQ // mode.tb
    n_seq = N_TOK // DEC_SEQ
    first = N_PROMPT // DEC_SEQ // DEC_BATCH
    view = slabs.reshape(5, n_seq, DEC_SEQ, H_A, HEAD)

    def spec(p, rev):
        if rev:
            return pl.BlockSpec((None, DEC_BATCH, mode.tb, H_A, HEAD), lambda t: (p, first, n_t - 1 - t, 0, 0))
        return pl.BlockSpec((None, DEC_BATCH, mode.tb, H_A, HEAD), lambda t: (p, first, t, 0, 0))

    def ospec(rev):
        if rev:
            return pl.BlockSpec((DEC_BATCH, mode.tb, H_A, HEAD), lambda t: (first, n_t - 1 - t, 0, 0))
        return pl.BlockSpec((DEC_BATCH, mode.tb, H_A, HEAD), lambda t: (first, t, 0, 0))

    vsub = HEAD // mode.rep
    lb_s = jnp.transpose(lb.reshape(2, H_A, HEAD), (2, 0, 1))
    lb_s = jnp.broadcast_to(lb_s[:, None, :, None, :], (HEAD, mode.rep, 2, DEC_BATCH, H_A)).reshape(HEAD, LANES)
    s0 = state0.reshape(DEC_BATCH, 2, H_A, DK_A, mode.rep, vsub)
    s0 = jnp.transpose(s0, (3, 5, 4, 1, 0, 2)).reshape(1, HEAD, vsub, LANES)
    st_spec = pl.BlockSpec((1, HEAD, vsub, LANES), lambda t: (0, 0, 0, 0))
    o_shape = jax.ShapeDtypeStruct((n_seq, DEC_SEQ, H_A, HEAD), F32)
    any_spec = pl.BlockSpec(memory_space=pl.ANY)
    o_f, o_b, _ = pl.pallas_call(
        functools.partial(_hgrn_scan_kernel, mode=mode, has_s0=True, t_axis=0),
        out_shape=(o_shape, o_shape, jax.ShapeDtypeStruct((1, HEAD, vsub, LANES), F32)),
        grid=(n_t,),
        in_specs=[spec(0, False), spec(0, True), spec(1, False), spec(2, True), spec(3, False), spec(3, True),
                  pl.BlockSpec((HEAD, LANES), lambda t: (0, 0)), st_spec, any_spec, any_spec],
        out_specs=(ospec(False), ospec(True), st_spec),
        scratch_shapes=_scan_scratch(mode, 4),
        input_output_aliases={8: 0, 9: 1},
        compiler_params=_cparams(("arbitrary",)),
        name="hgrn_scan_latent",
    )(view, view, view, view, view, view, lb_s, s0,
      o_f.reshape(n_seq, DEC_SEQ, H_A, HEAD), o_b.reshape(n_seq, DEC_SEQ, H_A, HEAD))
    return o_f.reshape(N_TOK, H_A * HEAD), o_b.reshape(N_TOK, H_A * HEAD), s_fin


def _rwkv_scans(r, w_f, w_b, k, v, kk, a, state0):
    streams = ((r, r), (w_f, w_b), (k, k), (v, v), (kk, kk), (a, a))
    mode = _ScanMode(prompt=True, groups=2, rep=1, r2=0, tb=16)
    n_t = SEQ // mode.tb
    n_half = 2

    def view(x):
        return x.reshape(N_TOK * 8 // LANES, LANES, LANES)

    fwd = pl.BlockSpec((mode.tb, HEAD, LANES), lambda h, t: (t, h, 0))
    bwd = pl.BlockSpec((mode.tb, HEAD, LANES), lambda h, t: (n_t - 1 - t, h, 0))
    st_spec = pl.BlockSpec((None, 2, HEAD, HEAD, LANES), lambda h, t: (h, 0, 0, 0, 0))
    y_shape = jax.ShapeDtypeStruct((N_TOK * 8 // LANES, LANES, LANES), F32)
    args = []
    for x_f, x_b in streams:
        args += [view(x_f), view(x_b)]
    y_f, y_b, s_fin = pl.pallas_call(
        functools.partial(_rwkv_scan_kernel, mode=mode, has_s0=False, t_axis=1),
        out_shape=(y_shape, y_shape, jax.ShapeDtypeStruct((n_half, 2, HEAD, HEAD, LANES), F32)),
        grid=(n_half, n_t),
        in_specs=[fwd, bwd] * 6,
        out_specs=(fwd, bwd, st_spec),
        scratch_shapes=_scan_scratch(mode, 6),
        compiler_params=_cparams(("parallel", "arbitrary")),
        name="rwkv_scan_prompt",
    )(*args)
    s_fin = s_fin.reshape(n_half, 2, HEAD, HEAD, 2, BATCH // n_half, H_C // 2)
    s_fin = jnp.transpose(s_fin, (0, 5, 4, 6, 1, 3, 2)).reshape(BATCH, 2, H_C, N_C, N_C)

    mode = _ScanMode(prompt=False, groups=1, rep=2, r2=H_C, tb=32)
    n_t = DEC_SEQ // mode.tb
    n_seq = N_TOK // DEC_SEQ
    first = N_PROMPT // DEC_SEQ // DEC_BATCH

    def view(x):
        return x.reshape(n_seq, DEC_SEQ, H_C, HEAD)

    fwd = pl.BlockSpec((DEC_BATCH, mode.tb, H_C, HEAD), lambda t: (first, t, 0, 0))
    bwd = pl.BlockSpec((DEC_BATCH, mode.tb, H_C, HEAD), lambda t: (first, n_t - 1 - t, 0, 0))
    vsub = HEAD // mode.rep
    s0 = state0.reshape(DEC_BATCH, 2, H_C, mode.rep, vsub, N_C)
    s0 = jnp.transpose(s0, (5, 4, 3, 1, 0, 2)).reshape(1, HEAD, vsub, LANES)
    st_spec = pl.BlockSpec((1, HEAD, vsub, LANES), lambda t: (0, 0, 0, 0))
    y_shape = jax.ShapeDtypeStruct((n_seq, DEC_SEQ, H_C, HEAD), F32)
    any_spec = pl.BlockSpec(memory_space=pl.ANY)
    args = []
    for x_f, x_b in streams:
        args += [view(x_f), view(x_b)]
    y_f, y_b, _ = pl.pallas_call(
        functools.partial(_rwkv_scan_kernel, mode=mode, has_s0=True, t_axis=0),
        out_shape=(y_shape, y_shape, jax.ShapeDtypeStruct((1, HEAD, vsub, LANES), F32)),
        grid=(n_t,),
        in_specs=[fwd, bwd] * 6 + [st_spec, any_spec, any_spec],
        out_specs=(fwd, bwd, st_spec),
        scratch_shapes=_scan_scratch(mode, 6),
        input_output_aliases={13: 0, 14: 1},
        compiler_params=_cparams(("arbitrary",)),
        name="rwkv_scan_latent",
    )(*args, s0, view(y_f), view(y_b))
    return y_f.reshape(N_TOK, D_MODEL), y_b.reshape(N_TOK, D_MODEL), s_fin


def _hgrn_post_kernel(of_ref, ob_ref, g_ref, gain_ref, bo_ref, o_ref):
    o = of_ref[...] + ob_ref[...]
    ms = _head_sum(o * o, bo_ref[...]) * (1.0 / DV_A)
    y = o * lax.rsqrt(ms + EPS) * gain_ref[...]
    g = g_ref[...]
    o_ref[...] = (y * (g * jax.nn.sigmoid(g))).astype(o_ref.dtype)


def _hgrn_post(o_f, o_b, slabs, gain):
    w = H_A * DV_A
    return pl.pallas_call(
        _hgrn_post_kernel,
        out_shape=jax.ShapeDtypeStruct((N_TOK, w), BF16),
        grid=(N_TILES,),
        in_specs=[_row_spec(w), _row_spec(w),
                  pl.BlockSpec((None, ROW_TILE, w), lambda i: (4, i, 0)),
                  _vec_spec(w), pl.BlockSpec((LANES, LANES), lambda i: (0, 0))],
        out_specs=_row_spec(w),
        compiler_params=_cparams(("parallel",)),
        name="hgrn_post",
    )(o_f, o_b, slabs, jnp.tile(gain, H_A).reshape(1, w), _block_ones())


def _q_up_kernel(cq_ref, gain_ref, w_ref, cos_ref, sin_ref, qn_ref, qr_ref):
    cq = _rms(cq_ref[...], gain_ref[...]).astype(BF16)
    q = jnp.dot(cq, w_ref[...], preferred_element_type=F32)
    qn_ref[...] = q[:, :512].astype(BF16)
    qr_ref[...] = (q[:, 512:768] * cos_ref[...] + q[:, 768:1024] * sin_ref[...]).astype(BF16)


def _q_up(tail, gain, w, cos, sin):
    return pl.pallas_call(
        _q_up_kernel,
        out_shape=(jax.ShapeDtypeStruct((N_TOK, 512), BF16), jax.ShapeDtypeStruct((N_TOK, 256), BF16)),
        grid=(N_TILES,),
        in_specs=[_row_spec(Q_RANK, T_CQ // Q_RANK),
                  _vec_spec(Q_RANK), pl.BlockSpec((Q_RANK, 1024), lambda i: (0, 0)),
                  _row_spec(256), _row_spec(256)],
        out_specs=(_row_spec(512), _row_spec(256)),
        compiler_params=_cparams(("parallel",)),
        name="q_up",
    )(tail, gain.reshape(1, Q_RANK), w, cos, sin)


def _kv_up_kernel(ckv_ref, kr_ref, krot_ref, gain_ref, w_ref, cos_ref, sin_ref, ckvn_ref, kn_ref, v_ref, krope_ref):
    ckv = _rms(ckv_ref[...], gain_ref[...])
    ckvn_ref[...] = ckv
    kv = jnp.dot(ckv.astype(BF16), w_ref[...], preferred_element_type=F32)
    kn_ref[...] = kv[:, :512].astype(BF16)
    v_ref[...] = kv[:, 512:].astype(BF16)
    krope_ref[...] = (kr_ref[...] * cos_ref[...] + krot_ref[...] * sin_ref[...]).astype(BF16)


def _kv_up(tail, gain, w, cos4, sin4):
    return pl.pallas_call(
        _kv_up_kernel,
        out_shape=(jax.ShapeDtypeStruct((N_TOK, KV_RANK), F32), jax.ShapeDtypeStruct((N_TOK, 512), BF16),
                   jax.ShapeDtypeStruct((N_TOK, 512), BF16), jax.ShapeDtypeStruct((N_TOK, LANES), BF16)),
        grid=(N_TILES,),
        in_specs=[_row_spec(LANES, T_CKV // LANES), _row_spec(LANES, T_KR // LANES), _row_spec(LANES, T_KROT // LANES),
                  _vec_spec(KV_RANK), pl.BlockSpec((KV_RANK, 1024), lambda i: (0, 0)),
                  _row_spec(LANES), _row_spec(LANES)],
        out_specs=(_row_spec(KV_RANK), _row_spec(512), _row_spec(512), _row_spec(LANES)),
        compiler_params=_cparams(("parallel",)),
        name="kv_up",
    )(tail, tail, tail, gain.reshape(1, KV_RANK), w, cos4, sin4)


def _attn_kernel(qn_ref, qr_ref, kn_ref, kr_ref, v_ref, o_ref, *, scale):
    hp = pl.program_id(1)
    qn = qn_ref[0]
    qr = qr_ref[0]
    kcat = jnp.concatenate([kn_ref[0], kr_ref[0]], axis=-1)
    v = v_ref[0]
    lane = lax.broadcasted_iota(jnp.int32, (1, LANES), 1)
    zero = jnp.zeros((), BF16)
    outs = []
    for j in range(2):
        mn = (lane // D_NOPE) == j
        mr = (lane // D_ROPE) == (2 * hp + j) % 4
        qcat = jnp.concatenate([jnp.where(mn, qn, zero), jnp.where(mr, qr, zero)], axis=-1)
        s = lax.dot_general(qcat, kcat, (((1,), (1,)), ((), ())), preferred_element_type=F32) * scale
        p = jnp.exp(s - jnp.max(s, axis=-1, keepdims=True))
        l = jnp.sum(p, axis=-1, keepdims=True)
        outs.append(jnp.dot(p.astype(BF16), v, preferred_element_type=F32) / l)
    o_ref[0] = jnp.where(lane < DV_B, outs[0], outs[1]).astype(o_ref.dtype)


def _attention(qn, qr, kn, kr, v, bsz, time_major, tq=256):
    t_len, tk = qn.shape[1], kn.shape[1]

    def spec(is_query, per_seq, col):
        rows = tq if is_query else tk
        if time_major:
            return pl.BlockSpec((1, rows, LANES), lambda b, h, q: (0, q if is_query else 0, b * per_seq + col(h)))
        return pl.BlockSpec((1, rows, LANES), lambda b, h, q: (b, q if is_query else 0, col(h)))

    return pl.pallas_call(
        functools.partial(_attn_kernel, scale=float((D_NOPE + D_ROPE) ** -0.5)),
        out_shape=jax.ShapeDtypeStruct(qn.shape, BF16),
        grid=(bsz, 4, t_len // tq),
        in_specs=[spec(True, 4, lambda h: h), spec(True, 2, lambda h: h // 2),
                  spec(False, 4, lambda h: h), spec(False, 1, lambda h: 0), spec(False, 4, lambda h: h)],
        out_specs=spec(True, 4, lambda h: h),
        compiler_params=_cparams(("parallel", "parallel", "parallel")),
        name="attention",
    )(qn, qr, kn, kr, v)


def _odd_pre_kernel(x_ref, xp_ref, xn_ref, g_ref, sc_ref, sh_ref, mu_ref, *o_refs):
    i = pl.program_id(0)
    g, sc, sh = g_ref[...], sc_ref[0], sh_ref[0]

    def nm(x):
        return _rms(x, g) * (1.0 + sc) + sh

    h = nm(x_ref[...])
    hp = nm(xp_ref[...])
    hn = nm(xn_ref[...])
    is_prompt = i < BATCH
    hp_p = jnp.where(i > 0, hp, 0.0)
    hn_p = jnp.where(i < BATCH - 1, hn, 0.0)
    prev_p = jnp.concatenate([hp_p, h[:ROW_TILE - BATCH]], axis=0)
    next_p = jnp.concatenate([h[BATCH:], hn_p], axis=0)
    part = (i - BATCH) % 4
    hp_s = jnp.where(part != 0, hp[HALO - 1:HALO], 0.0)
    hn_s = jnp.where(part != 3, hn[0:1], 0.0)
    row = lax.broadcasted_iota(jnp.int32, (ROW_TILE, 1), 0)
    prev_s = jnp.where(row == 0, hp_s, pltpu.roll(h, 1, axis=0))
    next_s = jnp.where(row == ROW_TILE - 1, hn_s, pltpu.roll(h, ROW_TILE - 1, axis=0))
    prev = jnp.where(is_prompt, prev_p, prev_s)
    nxt = jnp.where(is_prompt, next_p, next_s)
    xx = 0.5 * (prev + nxt) - h
    for j, o_ref in enumerate(o_refs):
        o_ref[...] = (h + xx * mu_ref[j:j + 1, :]).astype(o_ref.dtype)


def _odd_pre(x, g, sc, sh, mu):
    assert HALO == BATCH
    per = ROW_TILE // HALO
    last = N_TOK // HALO - 1
    return pl.pallas_call(
        _odd_pre_kernel,
        out_shape=tuple(jax.ShapeDtypeStruct((N_TOK, D_MODEL), BF16) for _ in range(6)),
        grid=(N_TILES,),
        in_specs=[_row_spec(),
                  pl.BlockSpec((HALO, D_MODEL), lambda i: (jnp.maximum(i * per - 1, 0), 0)),
                  pl.BlockSpec((HALO, D_MODEL), lambda i: (jnp.minimum((i + 1) * per, last), 0)),
                  _vec_spec(), _tile_spec(), _tile_spec(),
                  pl.BlockSpec((8, D_MODEL), lambda i: (0, 0))],
        out_specs=tuple(_row_spec() for _ in range(6)),
        compiler_params=_cparams(("parallel",)),
        name="odd_pre",
    )(x, x, x, g.reshape(1, D_MODEL), sc, sh, jnp.pad(mu, ((0, 2), (0, 0))))


def _rwkv_prep_kernel(k_ref, a_ref, kk_w_ref, ka_w_ref, bo_ref, kk_ref, k2_ref):
    k = k_ref[...]
    kk = k * kk_w_ref[...]
    kk_ref[...] = kk * lax.rsqrt(_head_sum(kk * kk, bo_ref[...]) + 1e-12)
    k2_ref[...] = k * (1.0 + (a_ref[...] - 1.0) * ka_w_ref[...])


def _rwkv_prep(k, a, k_k, k_a):
    return pl.pallas_call(
        _rwkv_prep_kernel,
        out_shape=(jax.ShapeDtypeStruct((N_TOK, D_MODEL), F32), jax.ShapeDtypeStruct((N_TOK, D_MODEL), F32)),
        grid=(N_TILES,),
        in_specs=[_row_spec(), _row_spec(), _vec_spec(), _vec_spec(),
                  pl.BlockSpec((LANES, LANES), lambda i: (0, 0))],
        out_specs=(_row_spec(), _row_spec()),
        compiler_params=_cparams(("parallel",)),
        name="rwkv_prep",
    )(k, a, k_k.reshape(1, D_MODEL), k_a.reshape(1, D_MODEL), _block_ones())


def _rwkv_post_kernel(yf_ref, yb_ref, r_ref, k_ref, v_ref, g_ref, rk_ref, lnw_ref, lnb_ref, bo_ref, o_ref):
    bo = bo_ref[...]
    y = yf_ref[...] + yb_ref[...]
    d = y - _head_sum(y, bo) * (1.0 / N_C)
    var = _head_sum(d * d, bo) * (1.0 / N_C)
    yn = d * lax.rsqrt(var + GN_EPS) * lnw_ref[...] + lnb_ref[...]
    bonus = _head_sum(r_ref[...] * k_ref[...] * rk_ref[...], bo) * v_ref[...]
    o_ref[...] = ((yn + bonus) * g_ref[...]).astype(o_ref.dtype)


def _rwkv_post(y_f, y_b, r, k2, v, g, r_k, ln_w, ln_b):
    return pl.pallas_call(
        _rwkv_post_kernel,
        out_shape=jax.ShapeDtypeStruct((N_TOK, D_MODEL), BF16),
        grid=(N_TILES,),
        in_specs=[_row_spec()] * 6 + [_vec_spec()] * 3 + [pl.BlockSpec((LANES, LANES), lambda i: (0, 0))],
        out_specs=_row_spec(),
        compiler_params=_cparams(("parallel",)),
        name="rwkv_post",
    )(y_f, y_b, r, k2, v, g, r_k.reshape(1, D_MODEL), ln_w.reshape(1, D_MODEL), ln_b.reshape(1, D_MODEL),
      _block_ones())


def _rot_cols(w):
    wb = w.reshape(w.shape[:-1] + (2, 2, D_ROPE // 4))
    return jnp.concatenate([-wb[..., 1:, :], wb[..., :1, :]], axis=-2).reshape(w.shape)


def _rope_tables():
    rows = DEC_SEQ // GRID_W
    row, col = np.meshgrid(np.arange(rows), np.arange(GRID_W), indexing='ij')
    row = jnp.asarray(row.reshape(-1), F32)
    col = jnp.asarray(col.reshape(-1), F32)
    n_freq = D_ROPE // 4
    inv = 1.0 / (ROPE_BASE ** (jnp.arange(n_freq, dtype=F32) / n_freq))
    ar = row[:, None] * inv
    ac = col[:, None] * inv
    ang = jnp.concatenate([ar, ar, ac, ac], axis=-1)
    cos, sin = jnp.cos(ang), jnp.sin(ang)
    cos = jnp.concatenate([jnp.ones((N_PROMPT, D_ROPE), F32), jnp.tile(cos, (DEC_BATCH, 1))], axis=0)
    sin = jnp.concatenate([jnp.zeros((N_PROMPT, D_ROPE), F32), jnp.tile(sin, (DEC_BATCH, 1))], axis=0)
    return cos, sin


def _even_mixer(j, h, lb_all, cos, sin, cache_ckv, cache_krope, state_hgrn,
                ev_w_in, hgrn_norm, mla_q_norm, mla_w_q_up, mla_kv_norm, mla_w_kv_up):
    w_in = ev_w_in[j]
    n_a = 5 * H_A * DK_A
    w_kr = w_in[:, n_a + Q_RANK + KV_RANK:]
    w_tail = jnp.concatenate([w_in[:, n_a:n_a + Q_RANK + KV_RANK], jnp.tile(w_kr, (1, 4)),
                              jnp.tile(_rot_cols(w_kr), (1, 4))], axis=1).astype(BF16)
    slabs = _mm_split(h, w_in[:, :n_a].astype(BF16), H_A * DK_A, name="hgrn_proj")
    tail = _mm(h, w_tail, F32, tm=512, tn=T_N, name="mla_proj")

    o_f, o_b, s_h = _hgrn_scans(slabs, lb_all[j], state_hgrn[:, j])
    o_a = _hgrn_post(o_f, o_b, slabs, hgrn_norm[j])

    wq = mla_w_q_up[j].reshape(Q_RANK, H_B, D_NOPE + D_ROPE)
    wq_n = wq[:, :, :D_NOPE].reshape(Q_RANK, H_B * D_NOPE)
    wq_r = wq[:, :, D_NOPE:]
    wq_aug = jnp.concatenate([wq_n, wq_r.reshape(Q_RANK, -1), _rot_cols(wq_r).reshape(Q_RANK, -1)], axis=1).astype(BF16)
    qn, qr = _q_up(tail, mla_q_norm[j], wq_aug, jnp.tile(cos, (1, H_B)), jnp.tile(sin, (1, H_B)))
    wkv = mla_w_kv_up[j].reshape(KV_RANK, H_B, D_NOPE + DV_B)
    wkv_aug = jnp.concatenate([wkv[:, :, :D_NOPE].reshape(KV_RANK, -1), wkv[:, :, D_NOPE:].reshape(KV_RANK, -1)],
                              axis=1).astype(BF16)
    ckv_n, kn, vv, kr = _kv_up(tail, mla_kv_norm[j], wkv_aug, jnp.tile(cos, (1, 4)), jnp.tile(sin, (1, 4)))
    kv_ctx = _mm(cache_ckv[:, j].reshape(DEC_BATCH * PAST_LEN, KV_RANK), wkv_aug, BF16, name="kv_ctx")
    kn_c = kv_ctx[:, :512].reshape(DEC_BATCH, PAST_LEN, 512)
    v_c = kv_ctx[:, 512:].reshape(DEC_BATCH, PAST_LEN, 512)
    kr_c = jnp.tile(cache_krope[:, j], (1, 1, 4)).astype(BF16)

    def prompt(x):
        return x[:N_PROMPT].reshape(1, SEQ, BATCH * x.shape[-1])

    def latent(x):
        return x[N_PROMPT:].reshape(DEC_BATCH, DEC_SEQ, x.shape[-1])

    o_p = _attention(prompt(qn), prompt(qr), prompt(kn), prompt(kr), prompt(vv), BATCH, True)
    o_s = _attention(latent(qn), latent(qr),
                     jnp.concatenate([kn_c, latent(kn)], axis=1),
                     jnp.concatenate([kr_c, latent(kr)], axis=1),
                     jnp.concatenate([v_c, latent(vv)], axis=1), DEC_BATCH, False)
    o_att = jnp.concatenate([o_p.reshape(N_PROMPT, 512), o_s.reshape(N_SAMPLE, 512)], axis=0)

    new_ckv = jnp.transpose(ckv_n[:N_PROMPT].reshape(SEQ, BATCH, KV_RANK), (1, 0, 2))
    new_krope = jnp.transpose(tail[:N_PROMPT, T_KR:T_KR + D_ROPE].reshape(SEQ, BATCH, D_ROPE), (1, 0, 2))
    return o_a, o_att, (s_h, new_ckv, new_krope)


def _odd_mixer(j, x, g0, sc, sh, state_rwkv, rw_mu, rw_w_r, rw_w_k, rw_w_v, rw_w0, rw_w1, rw_w2, rw_a0, rw_a1, rw_a2,
               rw_g1, rw_g2, rw_k_k, rw_k_a, rw_r_k, rw_ln_w, rw_ln_b):
    xr, xw, xk, xv, xa, xg = _odd_pre(x, g0, sc, sh, rw_mu[j])
    r = _mm(xr, rw_w_r[j].astype(BF16), name="rwkv_r")
    k = _mm(xk, rw_w_k[j].astype(BF16), name="rwkv_k")
    v = _mm(xv, rw_w_v[j].astype(BF16), name="rwkv_v")
    a = _lora(xa, rw_a1[j], rw_a2[j], rw_a0[j], _identity, jax.nn.sigmoid)
    g = _lora(xg, rw_g1[j], rw_g2[j], jnp.zeros((D_MODEL,), F32), jax.nn.sigmoid, _identity)
    w_f = _lora(xw, rw_w1[j, 0], rw_w2[j, 0], rw_w0[j, 0], jnp.tanh, _decay)
    w_b = _lora(xw, rw_w1[j, 1], rw_w2[j, 1], rw_w0[j, 1], jnp.tanh, _decay)
    kk, k2 = _rwkv_prep(k, a, rw_k_k[j], rw_k_a[j])
    y_f, y_b, s_r = _rwkv_scans(r, w_f, w_b, k2, v, kk, a, state_rwkv[:, j])
    mix = _rwkv_post(y_f, y_b, r, k2, v, g, rw_r_k[j], rw_ln_w[j], rw_ln_b[j])
    return mix, s_r


def kernel(x_prompt, x_sample, cache_ckv, cache_krope, state_hgrn, state_rwkv, c, c_ctx, ada_w, ada_b, norm_gains, ev_w_in, hgrn_lb_logits, hgrn_norm, mla_q_norm, mla_w_q_up, mla_kv_norm, mla_w_kv_up, ev_w_out, rw_mu, rw_w_r, rw_w_k, rw_w_v, rw_w_o, rw_w0, rw_w1, rw_w2, rw_a0, rw_a1, rw_a2, rw_g1, rw_g2, rw_k_k, rw_k_a, rw_r_k, rw_ln_w, rw_ln_b, ffn_w_gate, ffn_w_up, ffn_w_down, moe_router, moe_w_gate, moe_w_up, moe_w_down):
    lb_all = jnp.cumsum(jax.nn.softmax(hgrn_lb_logits.astype(F32), axis=0), axis=0)
    lb_all = lb_all - lb_all[:1]
    cos, sin = _rope_tables()

    cond8 = jnp.zeros((8, D_MODEL), F32).at[0].set(c_ctx).at[1:1 + DEC_BATCH].set(c)
    mod = _modulation(cond8, ada_w, ada_b)
    tile_row = np.array([0] * BATCH + [1 + t // 4 for t in range(N_TILES - BATCH)])
    modt = mod[:, tile_row, :].reshape(DEPTH, N_TILES, 6, 1, D_MODEL)

    x = jnp.concatenate([jnp.transpose(x_prompt, (1, 0, 2)).reshape(N_PROMPT, D_MODEL),
                         x_sample.reshape(N_SAMPLE, D_MODEL)], axis=0)
    ckv_l, krope_l, hgrn_l, rwkv_l = [], [], [], []
    for l in range(DEPTH):
        j = l // 2
        sh_m, sc_m, gt_m, sh_f, sc_f, gt_f = [modt[l, :, i] for i in range(6)]
        if l % 2 == 0:
            h = _norm_mod(x, norm_gains[l, 0], sc_m, sh_m)
            o_a, o_att, (s_h, n_ckv, n_kr) = _even_mixer(j, h, lb_all, cos, sin, cache_ckv, cache_krope, state_hgrn,
                                                        ev_w_in, hgrn_norm, mla_q_norm, mla_w_q_up, mla_kv_norm,
                                                        mla_w_kv_up)
            hgrn_l.append(s_h)
            ckv_l.append(n_ckv)
            krope_l.append(n_kr)
            x = _mm2_resid(o_a, o_att, ev_w_out[j].astype(BF16), x, norm_gains[l, 1], gt_m)
            h = _norm_mod(x, norm_gains[l, 2], sc_f, sh_f)
            ones = jnp.ones((1, N_TOK, 1), F32)
            hid = _swiglu(h, ffn_w_gate[j][None].astype(BF16), ffn_w_up[j][None].astype(BF16), ones)
            x = _mm_resid(hid, ffn_w_down[j].astype(BF16), x, norm_gains[l, 3], gt_f)
        else:
            mix, s_r = _odd_mixer(j, x, norm_gains[l, 0], sc_m, sh_m, state_rwkv, rw_mu, rw_w_r, rw_w_k, rw_w_v,
                                  rw_w0, rw_w1, rw_w2, rw_a0, rw_a1, rw_a2, rw_g1, rw_g2, rw_k_k, rw_k_a, rw_r_k,
                                  rw_ln_w, rw_ln_b)
            rwkv_l.append(s_r)
            x = _mm_resid(mix, rw_w_o[j].astype(BF16), x, norm_gains[l, 1], gt_m)
            h, gate = _norm_mod_router(x, norm_gains[l, 2], sc_f, sh_f, moe_router[j])
            gate_e = jnp.transpose(gate[:, :N_EXPERTS])[:, :, None]
            hid = _swiglu(h, moe_w_gate[j].astype(BF16), moe_w_up[j].astype(BF16), gate_e)
            ff = _mm_acc(hid, moe_w_down[j].reshape(N_EXPERTS * D_FF, D_MODEL).astype(BF16), 1024, 1024, D_FF)
            x = _resid_norm(x, ff, norm_gains[l, 3], gt_f)

    y_prompt = jnp.transpose(x[:N_PROMPT].reshape(SEQ, BATCH, D_MODEL), (1, 0, 2))
    y_sample = x[N_PROMPT:].reshape(DEC_BATCH, DEC_SEQ, D_MODEL)
    return (y_prompt, y_sample, jnp.stack(ckv_l, axis=1), jnp.stack(krope_l, axis=1),
            jnp.stack(hgrn_l, axis=1), jnp.stack(rwkv_l, axis=1))
```

```python
import functools
from typing import NamedTuple

import numpy as np
import jax
import jax.numpy as jnp
from jax import lax
from jax.experimental import pallas as pl
from jax.experimental.pallas import tpu as pltpu

D_MODEL = 1024
BATCH = 16
SEQ = 256
DEPTH = 4
DEC_BATCH = 2
DEC_SEQ = 1024
PAST_LEN = 256
GRID_W = 64
H_A = 8
DK_A = 64
DV_A = 64
H_B = 8
Q_RANK = 256
KV_RANK = 128
D_NOPE = 64
D_ROPE = 32
DV_B = 64
ROPE_BASE = 10000.0
H_C = 16
N_C = 64
D_FF = 2816
N_EXPERTS = 8
EPS = 1e-6
GN_EPS = 64e-5

F32 = jnp.float32
BF16 = jnp.bfloat16
HIGHEST = lax.Precision.HIGHEST

N_PROMPT = BATCH * SEQ
N_SAMPLE = DEC_BATCH * DEC_SEQ
N_TOK = N_PROMPT + N_SAMPLE
ROW_TILE = 256
N_TILES = N_TOK // ROW_TILE
LANES = 128
HEAD = 64
VMEM_LIMIT = 48 * 1024 * 1024
HALO = 16
PAIRS = 4
RWKV_CHUNK = 64

T_CQ, T_CKV, T_KR, T_KROT, T_N = 0, 256, 384, 512, 640


def _cparams(sem):
    return pltpu.CompilerParams(dimension_semantics=sem, vmem_limit_bytes=VMEM_LIMIT)


def _mod_kernel(c_ref, w_ref, b_ref, o_ref):
    c = c_ref[...]
    s = (c * jax.nn.sigmoid(c)).astype(BF16)
    o_ref[0] = jnp.dot(s, w_ref[0].astype(BF16), preferred_element_type=F32) + b_ref[0]


def _modulation(cond8, ada_w, ada_b):
    tn = 768
    return pl.pallas_call(
        _mod_kernel,
        out_shape=jax.ShapeDtypeStruct((DEPTH, 8, 6 * D_MODEL), F32),
        grid=(DEPTH, 6 * D_MODEL // tn),
        in_specs=[pl.BlockSpec((8, D_MODEL), lambda l, n: (0, 0)),
                  pl.BlockSpec((1, D_MODEL, tn), lambda l, n: (l, 0, n)),
                  pl.BlockSpec((1, 1, tn), lambda l, n: (l, 0, n))],
        out_specs=pl.BlockSpec((1, 8, tn), lambda l, n: (l, 0, n)),
        compiler_params=_cparams(("parallel", "parallel")),
        name="modulation",
    )(cond8, ada_w, ada_b.reshape(DEPTH, 1, 6 * D_MODEL))


def _rms(x, g):
    return x * lax.rsqrt(jnp.mean(x * x, axis=-1, keepdims=True) + EPS) * g


def _norm_mod_kernel(x_ref, g_ref, sc_ref, sh_ref, o_ref, *lo_ref):
    h = _rms(x_ref[...], g_ref[...]) * (1.0 + sc_ref[0]) + sh_ref[0]
    hi = h.astype(BF16)
    o_ref[...] = hi
    if lo_ref:
        lo_ref[0][...] = (h - hi.astype(F32)).astype(BF16)


def _tile_spec():
    return pl.BlockSpec((1, 1, D_MODEL), lambda i: (i, 0, 0))


def _row_spec(width=D_MODEL, col=0):
    return pl.BlockSpec((ROW_TILE, width), lambda i: (i, col))


def _vec_spec(width=D_MODEL):
    return pl.BlockSpec((1, width), lambda i: (0, 0))


def _norm_mod(x, g, sc, sh, with_residual=False):
    shape = jax.ShapeDtypeStruct((N_TOK, D_MODEL), BF16)
    return pl.pallas_call(
        _norm_mod_kernel,
        out_shape=(shape, shape) if with_residual else shape,
        grid=(N_TILES,),
        in_specs=[_row_spec(), _vec_spec(), _tile_spec(), _tile_spec()],
        out_specs=(_row_spec(), _row_spec()) if with_residual else _row_spec(),
        compiler_params=_cparams(("parallel",)),
        name="norm_mod",
    )(x, g.reshape(1, D_MODEL), sc, sh)


def _norm_mod_router_kernel(x_ref, g_ref, sc_ref, sh_ref, wr_ref, o_ref, gate_ref):
    h = _rms(x_ref[...], g_ref[...]) * (1.0 + sc_ref[0]) + sh_ref[0]
    o_ref[...] = h.astype(o_ref.dtype)
    logits = jnp.dot(h, wr_ref[...], precision=HIGHEST, preferred_element_type=F32)
    lane = lax.broadcasted_iota(jnp.int32, logits.shape, 1).astype(F32)
    neg = jnp.float32(-jnp.inf)
    lg = jnp.where(lane < N_EXPERTS, logits, neg)
    m1 = jnp.max(lg, axis=-1, keepdims=True)
    i1 = jnp.min(jnp.where(lg == m1, lane, float(LANES)), axis=-1, keepdims=True)
    lg2 = jnp.where(lane == i1, neg, lg)
    m2 = jnp.max(lg2, axis=-1, keepdims=True)
    i2 = jnp.min(jnp.where(lg2 == m2, lane, float(LANES)), axis=-1, keepdims=True)
    e = jnp.exp(m2 - m1)
    w1 = 1.0 / (1.0 + e)
    w2 = e * w1
    gate_ref[...] = jnp.where(lane == i1, w1, jnp.where(lane == i2, w2, 0.0))


def _norm_mod_router(x, g, sc, sh, w_router):
    wr = jnp.pad(w_router, ((0, 0), (0, LANES - N_EXPERTS)))
    return pl.pallas_call(
        _norm_mod_router_kernel,
        out_shape=(jax.ShapeDtypeStruct((N_TOK, D_MODEL), BF16),
                   jax.ShapeDtypeStruct((N_TOK, LANES), F32)),
        grid=(N_TILES,),
        in_specs=[_row_spec(), _vec_spec(), _tile_spec(), _tile_spec(),
                  pl.BlockSpec((D_MODEL, LANES), lambda i: (0, 0))],
        out_specs=(_row_spec(), _row_spec(LANES)),
        compiler_params=_cparams(("parallel",)),
        name="norm_mod_router",
    )(x, g.reshape(1, D_MODEL), sc, sh, wr)


def _resid_norm_kernel(x_ref, y_ref, g_ref, gt_ref, o_ref):
    o_ref[...] = x_ref[...] + gt_ref[0] * _rms(y_ref[...], g_ref[...])


def _resid_norm(x, y, g, gt):
    return pl.pallas_call(
        _resid_norm_kernel,
        out_shape=jax.ShapeDtypeStruct((N_TOK, D_MODEL), F32),
        grid=(N_TILES,),
        in_specs=[_row_spec(), _row_spec(), _vec_spec(), _tile_spec()],
        out_specs=_row_spec(),
        compiler_params=_cparams(("parallel",)),
        name="resid_norm",
    )(x, y, g.reshape(1, D_MODEL), gt)


def _mm_kernel(a_ref, w_ref, o_ref):
    o_ref[...] = jnp.dot(a_ref[...].astype(BF16), w_ref[...], preferred_element_type=F32).astype(o_ref.dtype)


def _mm(a, w, out_dtype=F32, tm=512, tn=512, name="mm"):
    m, k = a.shape
    n = w.shape[1]
    tm, tn = min(tm, m), min(tn, n)
    return pl.pallas_call(
        _mm_kernel,
        out_shape=jax.ShapeDtypeStruct((m, n), out_dtype),
        grid=(n // tn, m // tm),
        in_specs=[pl.BlockSpec((tm, k), lambda j, i: (i, 0)),
                  pl.BlockSpec((k, tn), lambda j, i: (0, j))],
        out_specs=pl.BlockSpec((tm, tn), lambda j, i: (i, j)),
        compiler_params=_cparams(("parallel", "parallel")),
        name=name,
    )(a, w)


def _mm_split_kernel(a_ref, alo_ref, w_ref, wlo_ref, o_ref, *, precise):
    a, w = a_ref[...], w_ref[...]
    acc = jnp.dot(a, w, preferred_element_type=F32)
    j = pl.program_id(0)
    is_precise = functools.reduce(jnp.logical_or, [j == p for p in precise])

    @pl.when(is_precise)
    def _():
        o_ref[...] = (acc + jnp.dot(a, wlo_ref[...], preferred_element_type=F32)
                      + jnp.dot(alo_ref[...], w, preferred_element_type=F32))

    @pl.when(jnp.logical_not(is_precise))
    def _():
        o_ref[...] = acc


def _mm_split(a, a_lo, w, tn, precise, tm=512, name="mm_split"):
    m, k = a.shape
    p = w.shape[1] // tn
    w_hi = w.astype(BF16)
    w_lo = (w - w_hi.astype(F32)).astype(BF16)
    a_spec = pl.BlockSpec((tm, k), lambda j, i: (i, 0))
    w_spec = pl.BlockSpec((k, tn), lambda j, i: (0, j))
    return pl.pallas_call(
        functools.partial(_mm_split_kernel, precise=precise),
        out_shape=jax.ShapeDtypeStruct((p, m, tn), F32),
        grid=(p, m // tm),
        in_specs=[a_spec, a_spec, w_spec, w_spec],
        out_specs=pl.BlockSpec((None, tm, tn), lambda j, i: (j, i, 0)),
        compiler_params=_cparams(("parallel", "parallel")),
        name=name,
    )(a, a_lo, w_hi, w_lo)


def _mm_acc_kernel(a_ref, w_ref, o_ref, acc_ref, *, nk):
    kk = pl.program_id(2)

    @pl.when(kk == 0)
    def _():
        acc_ref[...] = jnp.zeros_like(acc_ref)

    acc_ref[...] += jnp.dot(a_ref[...], w_ref[...], preferred_element_type=F32)

    @pl.when(kk == nk - 1)
    def _():
        o_ref[...] = acc_ref[...]


def _mm_acc(a, w, tm, tn, tk):
    m, k = a.shape
    n = w.shape[1]
    nk = k // tk
    return pl.pallas_call(
        functools.partial(_mm_acc_kernel, nk=nk),
        out_shape=jax.ShapeDtypeStruct((m, n), F32),
        grid=(n // tn, m // tm, nk),
        in_specs=[pl.BlockSpec((tm, tk), lambda j, i, kk: (i, kk)),
                  pl.BlockSpec((tk, tn), lambda j, i, kk: (kk, j))],
        out_specs=pl.BlockSpec((tm, tn), lambda j, i, kk: (i, j)),
        scratch_shapes=[pltpu.VMEM((tm, tn), F32)],
        compiler_params=_cparams(("parallel", "parallel", "arbitrary")),
        name="mm_acc",
    )(a, w)


def _mm_resid_kernel(a_ref, w_ref, x_ref, g_ref, gt_ref, o_ref):
    y = jnp.dot(a_ref[...], w_ref[...], preferred_element_type=F32)
    o_ref[...] = x_ref[...] + gt_ref[0] * _rms(y, g_ref[...])


def _mm_resid(a, w, x, g, gt):
    k = a.shape[1]
    return pl.pallas_call(
        _mm_resid_kernel,
        out_shape=jax.ShapeDtypeStruct((N_TOK, D_MODEL), F32),
        grid=(N_TILES,),
        in_specs=[_row_spec(k), pl.BlockSpec((k, D_MODEL), lambda i: (0, 0)),
                  _row_spec(), _vec_spec(), _tile_spec()],
        out_specs=_row_spec(),
        compiler_params=_cparams(("parallel",)),
        name="mm_resid",
    )(a, w, x, g.reshape(1, D_MODEL), gt)


def _mm2_resid_kernel(a1_ref, a2_ref, w_ref, x_ref, g_ref, gt_ref, o_ref):
    k1 = a1_ref.shape[1]
    y = (jnp.dot(a1_ref[...], w_ref[:k1], preferred_element_type=F32)
         + jnp.dot(a2_ref[...], w_ref[k1:], preferred_element_type=F32))
    o_ref[...] = x_ref[...] + gt_ref[0] * _rms(y, g_ref[...])


def _mm2_resid(a1, a2, w, x, g, gt):
    k1, k2 = a1.shape[1], a2.shape[1]
    return pl.pallas_call(
        _mm2_resid_kernel,
        out_shape=jax.ShapeDtypeStruct((N_TOK, D_MODEL), F32),
        grid=(N_TILES,),
        in_specs=[_row_spec(k1), _row_spec(k2), pl.BlockSpec((k1 + k2, D_MODEL), lambda i: (0, 0)),
                  _row_spec(), _vec_spec(), _tile_spec()],
        out_specs=_row_spec(),
        compiler_params=_cparams(("parallel",)),
        name="mm2_resid",
    )(a1, a2, w, x, g.reshape(1, D_MODEL), gt)


def _swiglu_kernel(a_ref, wg_ref, wu_ref, gate_ref, o_ref):
    a = a_ref[...]
    g = jnp.dot(a, wg_ref[0], preferred_element_type=F32)
    u = jnp.dot(a, wu_ref[0], preferred_element_type=F32)
    o_ref[...] = (gate_ref[0] * (g * jax.nn.sigmoid(g)) * u).astype(o_ref.dtype)


def _swiglu(a, wg, wu, gate, tm=512, tn=1408):
    n_e, _, f = wg.shape
    m = a.shape[0]
    nt = f // tn
    return pl.pallas_call(
        _swiglu_kernel,
        out_shape=jax.ShapeDtypeStruct((m, n_e * f), BF16),
        grid=(n_e, nt, m // tm),
        in_specs=[pl.BlockSpec((tm, D_MODEL), lambda e, j, i: (i, 0)),
                  pl.BlockSpec((1, D_MODEL, tn), lambda e, j, i: (e, 0, j)),
                  pl.BlockSpec((1, D_MODEL, tn), lambda e, j, i: (e, 0, j)),
                  pl.BlockSpec((1, tm, 1), lambda e, j, i: (e, i, 0))],
        out_specs=pl.BlockSpec((tm, tn), lambda e, j, i: (i, e * nt + j)),
        compiler_params=_cparams(("parallel", "parallel", "parallel")),
        name="swiglu",
    )(a, wg, wu, gate)


def _lora_kernel(x_ref, w1_ref, w2_ref, b_ref, o_ref, *, mid, out):
    t = mid(jnp.dot(x_ref[...], w1_ref[...], preferred_element_type=F32))
    o_ref[...] = out(jnp.dot(t.astype(BF16), w2_ref[...], preferred_element_type=F32) + b_ref[...])


def _lora(x, w1, w2, b, mid, out, tm=512):
    m = x.shape[0]
    r = w1.shape[1]
    return pl.pallas_call(
        functools.partial(_lora_kernel, mid=mid, out=out),
        out_shape=jax.ShapeDtypeStruct((m, D_MODEL), F32),
        grid=(m // tm,),
        in_specs=[pl.BlockSpec((tm, D_MODEL), lambda i: (i, 0)),
                  pl.BlockSpec((D_MODEL, r), lambda i: (0, 0)),
                  pl.BlockSpec((r, D_MODEL), lambda i: (0, 0)),
                  pl.BlockSpec((1, D_MODEL), lambda i: (0, 0))],
        out_specs=pl.BlockSpec((tm, D_MODEL), lambda i: (i, 0)),
        compiler_params=_cparams(("parallel",)),
        name="lora",
    )(x, w1.astype(BF16), w2.astype(BF16), b.reshape(1, D_MODEL))


def _identity(x):
    return x


def _decay(wr):
    return jnp.exp(-float(np.exp(-0.5)) * jax.nn.sigmoid(wr))


def _block_ones():
    i = np.arange(LANES)
    return jnp.asarray((i[:, None] // HEAD == i[None, :] // HEAD).astype(np.float32))


def _head_sum(x, bo):
    parts = [jnp.dot(x[:, c * LANES:(c + 1) * LANES], bo, precision=HIGHEST, preferred_element_type=F32)
             for c in range(x.shape[1] // LANES)]
    return jnp.concatenate(parts, axis=-1)


class _ScanMode(NamedTuple):
    prompt: bool
    groups: int
    rep: int
    r2: int
    tb: int


def _scan_gather(mode, src_ref, fwd_ref, bwd_ref, s):
    sb = mode.tb - 1 - s
    if mode.prompt:
        for d, (ref, tt) in enumerate(((fwd_ref, s), (bwd_ref, sb))):
            z = ref[pl.ds(pl.multiple_of(tt * BATCH, BATCH), BATCH), :]
            for hp in range(PAIRS):
                r0 = d * HEAD + hp * BATCH
                src_ref[r0:r0 + BATCH, :] = z[:, hp * LANES:(hp + 1) * LANES]
        return src_ref[...].T
    unit = 4 * mode.r2
    for q in range(mode.rep):
        for d, (ref, tt) in enumerate(((fwd_ref, s), (bwd_ref, sb))):
            for b in range(DEC_BATCH):
                r0 = q * unit + (2 * d + b) * mode.r2
                src_ref[r0:r0 + mode.r2, 0:HEAD] = ref[b, tt]
    return src_ref[...].T[0:HEAD]


def _scan_value_slab(mode, vt):
    vsub = HEAD // mode.rep
    out = vt[0:vsub]
    if mode.rep > 1:
        lane = lax.broadcasted_iota(jnp.int32, (vsub, LANES), 1)
        for q in range(1, mode.rep):
            out = jnp.where(lane >= q * (LANES // mode.rep), vt[q * vsub:(q + 1) * vsub], out)
    return out


def _scan_scatter(mode, src_ref, ys_ref, yf_ref, yb_ref, s):
    sb = mode.tb - 1 - s
    if mode.prompt:
        tr = jnp.concatenate([ys_ref[s, 0], ys_ref[s, 1]], axis=0).T
        for d, (ref, tt) in enumerate(((yf_ref, s), (yb_ref, sb))):
            row0 = pl.multiple_of(tt * BATCH, BATCH)
            for hp in range(PAIRS):
                r0 = d * HEAD + hp * BATCH
                ref[pl.ds(row0, BATCH), hp * LANES:(hp + 1) * LANES] = tr[r0:r0 + BATCH]
        return
    vsub = HEAD // mode.rep
    y = ys_ref[s, 0]
    for q in range(mode.rep):
        src_ref[q * vsub:(q + 1) * vsub, :] = y
    tr = src_ref[...].T
    unit = 4 * mode.r2
    out = tr[0:unit]
    lane = lax.broadcasted_iota(jnp.int32, (unit, LANES), 1)
    for q in range(1, mode.rep):
        out = jnp.where(lane >= q * vsub, tr[q * unit:(q + 1) * unit], out)
    for d, (ref, tt) in enumerate(((yf_ref, s), (yb_ref, sb))):
        for b in range(DEC_BATCH):
            r0 = (2 * d + b) * mode.r2
            ref[b, tt] = out[r0:r0 + mode.r2, 0:HEAD]


def _n_acc(vsub):
    return 1 if vsub == HEAD else 2


def _scan_init(mode, t_axis, s_ref, s0_ref, src_ref):
    @pl.when(pl.program_id(t_axis) == 0)
    def _():
        if s0_ref is None:
            s_ref[...] = jnp.zeros_like(s_ref)
        else:
            s_ref[...] = s0_ref[...]

    if not mode.prompt:
        src_ref[...] = jnp.zeros_like(src_ref)


def _hgrn_scan_kernel(*refs, mode, has_s0, t_axis):
    qf, qb, xf, xb, vf, vb, lb_ref = refs[:7]
    refs = refs[7:]
    s0_ref = None
    if has_s0:
        s0_ref, refs = refs[0], refs[1:]
    of_ref, ob_ref, s_ref, src_ref, q_t, f_t, k_t, v_t, ys_ref = refs
    vsub = HEAD // mode.rep
    n_acc = _n_acc(vsub)
    _scan_init(mode, t_axis, s_ref, s0_ref, src_ref)
    lb = lb_ref[...]

    def prologue(s, carry):
        q_t[s] = _scan_gather(mode, src_ref, qf, qb, s)
        x = _scan_gather(mode, src_ref, xf, xb, s)
        f_t[s] = lb + (1.0 - lb) * jax.nn.sigmoid(x)
        k_t[s] = (1.0 - lb) * jax.nn.sigmoid(-x)
        v_t[s] = _scan_gather(mode, src_ref, vf, vb, s)
        return carry

    lax.fori_loop(0, mode.tb, prologue, 0)

    def step(s, carry):
        for g in range(mode.groups):
            r0 = g * HEAD
            vv = _scan_value_slab(mode, v_t[s, r0:r0 + HEAD, :])
            acc = [jnp.zeros((vsub, LANES), F32) for _ in range(n_acc)]
            for d in range(HEAD):
                r = r0 + d
                sn = s_ref[g, d] * f_t[s, r:r + 1, :] + vv * k_t[s, r:r + 1, :]
                s_ref[g, d] = sn
                acc[d % n_acc] = acc[d % n_acc] +sn * q_t[s, r:r + 1, :]
            ys_ref[s, g] = sum(acc[1:], acc[0])
        return carry

    lax.fori_loop(0, mode.tb, step, 0)

    def epilogue(s, carry):
        _scan_scatter(mode, src_ref, ys_ref, of_ref, ob_ref, s)
        return carry

    lax.fori_loop(0, mode.tb, epilogue, 0)


def _rwkv_scan_kernel(*refs, mode, has_s0, t_axis):
    ins = refs[:12]
    refs = refs[12:]
    s0_ref = None
    if has_s0:
        s0_ref, refs = refs[0], refs[1:]
    yf_ref, yb_ref, s_ref, src_ref, r_t, w_t, k_t, v_t, kk_t, a_t, ys_ref = refs
    vsub = HEAD // mode.rep
    vc = min(vsub, RWKV_CHUNK)
    n_acc = _n_acc(vc)
    _scan_init(mode, t_axis, s_ref, s0_ref, src_ref)

    def prologue(s, carry):
        for i, x_t in enumerate((r_t, w_t, k_t, v_t, kk_t, a_t)):
            x_t[s] = _scan_gather(mode, src_ref, ins[2 * i], ins[2 * i + 1], s)
        return carry

    lax.fori_loop(0, mode.tb, prologue, 0)

    def step(s, carry):
        for g in range(mode.groups):
            r0 = g * HEAD
            vv_all = _scan_value_slab(mode, v_t[s, r0:r0 + HEAD, :])
            for c in range(vsub // vc):
                rows = slice(c * vc, (c + 1) * vc)

                def tile(x_t, i):
                    return x_t[s, pl.ds(pl.multiple_of(r0 + i * 8, 8), 8), :]

                def pass1(i, acc):
                    kk8 = tile(kk_t, i)
                    for j in range(8):
                        acc = acc + s_ref[g, i * 8 + j, rows, :] * kk8[j:j + 1, :]
                    return acc

                sa = -lax.fori_loop(0, HEAD // 8, pass1, jnp.zeros((vc, LANES), F32))
                vv = vv_all[rows]

                def pass2(i, acc):
                    kk8, a8, w8, k8, r8 = (tile(x_t, i) for x_t in (kk_t, a_t, w_t, k_t, r_t))
                    b8 = kk8 * a8
                    for j in range(8):
                        d = i * 8 + j
                        sn = s_ref[g, d, rows, :] * w8[j:j + 1, :] + sa * b8[j:j + 1, :] + vv * k8[j:j + 1, :]
                        s_ref[g, d, rows, :] = sn
                        acc = acc + sn * r8[j:j + 1, :]
                    return acc

                ys_ref[s, g, rows, :] = lax.fori_loop(0, HEAD // 8, pass2, jnp.zeros((vc, LANES), F32))
        return carry

    lax.fori_loop(0, mode.tb, step, 0)

    def epilogue(s, carry):
        _scan_scatter(mode, src_ref, ys_ref, yf_ref, yb_ref, s)
        return carry

    lax.fori_loop(0, mode.tb, epilogue, 0)


def _scan_scratch(mode, n_streams):
    rows = mode.groups * HEAD
    vsub = HEAD // mode.rep
    return ([pltpu.VMEM((LANES, LANES), F32)]
            + [pltpu.VMEM((mode.tb, rows, LANES), F32) for _ in range(n_streams)]
            + [pltpu.VMEM((mode.tb, mode.groups, vsub, LANES), F32)])


def _hgrn_scans(slabs, lb, state0):
    mode = _ScanMode(prompt=True, groups=2, rep=1, r2=0, tb=16)
    n_t = SEQ // mode.tb
    rows = mode.tb * BATCH
    width = PAIRS * LANES
    assert width == H_A * HEAD

    def spec(p, rev):
        if rev:
            return pl.BlockSpec((None, rows, width), lambda t: (p, n_t - 1 - t, 0))
        return pl.BlockSpec((None, rows, width), lambda t: (p, t, 0))

    def ospec(rev):
        if rev:
            return pl.BlockSpec((rows, width), lambda t: (n_t - 1 - t, 0))
        return pl.BlockSpec((rows, width), lambda t: (t, 0))

    lb4 = lb.reshape(2, PAIRS, 2, HEAD)
    lb_p = jnp.broadcast_to(jnp.transpose(lb4, (2, 3, 0, 1))[..., None], (2, HEAD, 2, PAIRS, BATCH))
    lb_p = lb_p.reshape(2 * HEAD, LANES)
    st_spec = pl.BlockSpec((2, HEAD, HEAD, LANES), lambda t: (0, 0, 0, 0))
    o_shape = jax.ShapeDtypeStruct((N_PROMPT, width), F32)
    view = slabs
    o_fp, o_bp, s_fin = pl.pallas_call(
        functools.partial(_hgrn_scan_kernel, mode=mode, has_s0=False, t_axis=0),
        out_shape=(o_shape, o_shape, jax.ShapeDtypeStruct((2, HEAD, HEAD, LANES), F32)),
        grid=(n_t,),
        in_specs=[spec(0, False), spec(0, True), spec(1, False), spec(2, True), spec(3, False), spec(3, True),
                  pl.BlockSpec((2 * HEAD, LANES), lambda t: (0, 0))],
        out_specs=(ospec(False), ospec(True), st_spec),
        scratch_shapes=_scan_scratch(mode, 4),
        compiler_params=_cparams(("arbitrary",)),
        name="hgrn_scan_prompt",
    )(view, view, view, view, view, view, lb_p)
    s_fin = jnp.transpose(s_fin.reshape(2, HEAD, HEAD, 2, PAIRS, BATCH), (5, 3, 4, 0, 1, 2))
    s_fin = s_fin.reshape(BATCH, 2, H_A, DK_A, DV_A)

    mode = _ScanMode(prompt=False, groups=1, rep=4, r2=H_A, tb=32)
    n_t = DEC_SEQ // mode.tb
    view = slabs[:, N_PROMPT:].reshape(5, DEC_BATCH, DEC_SEQ, H_A, HEAD)

    def spec(p, rev):
        if rev:
            return pl.BlockSpec((None, DEC_BATCH, mode.tb, H_A, HEAD), lambda t: (p, 0, n_t - 1 - t, 0, 0))
        return pl.BlockSpec((None, DEC_BATCH, mode.tb, H_A, HEAD), lambda t: (p, 0, t, 0, 0))

    def ospec(rev):
        if rev:
            return pl.BlockSpec((DEC_BATCH, mode.tb, H_A, HEAD), lambda t: (0, n_t - 1 - t, 0, 0))
        return pl.BlockSpec((DEC_BATCH, mode.tb, H_A, HEAD), lambda t: (0, t, 0, 0))

    vsub = HEAD // mode.rep
    lb_s = jnp.transpose(lb.reshape(2, H_A, HEAD), (2, 0, 1))
    lb_s = jnp.broadcast_to(lb_s[:, None, :, None, :], (HEAD, mode.rep, 2, DEC_BATCH, H_A)).reshape(HEAD, LANES)
    s0 = state0.reshape(DEC_BATCH, 2, H_A, DK_A, mode.rep, vsub)
    s0 = jnp.transpose(s0, (3, 5, 4, 1, 0, 2)).reshape(1, HEAD, vsub, LANES)
    st_spec = pl.BlockSpec((1, HEAD, vsub, LANES), lambda t: (0, 0, 0, 0))
    o_shape = jax.ShapeDtypeStruct((DEC_BATCH, DEC_SEQ, H_A, HEAD), F32)
    o_fl, o_bl, _ = pl.pallas_call(
        functools.partial(_hgrn_scan_kernel, mode=mode, has_s0=True, t_axis=0),
        out_shape=(o_shape, o_shape, jax.ShapeDtypeStruct((1, HEAD, vsub, LANES), F32)),
        grid=(n_t,),
        in_specs=[spec(0, False), spec(0, True), spec(1, False), spec(2, True), spec(3, False), spec(3, True),
                  pl.BlockSpec((HEAD, LANES), lambda t: (0, 0)), st_spec],
        out_specs=(ospec(False), ospec(True), st_spec),
        scratch_shapes=_scan_scratch(mode, 4),
        compiler_params=_cparams(("arbitrary",)),
        name="hgrn_scan_latent",
    )(view, view, view, view, view, view, lb_s, s0)
    return o_fp, o_bp, o_fl.reshape(N_SAMPLE, width), o_bl.reshape(N_SAMPLE, width), s_fin


def _rwkv_scans(r, w_f, w_b, k, v, kk, a, state0):
    streams = ((r, r), (w_f, w_b), (k, k), (v, v), (kk, kk), (a, a))
    mode = _ScanMode(prompt=True, groups=2, rep=1, r2=0, tb=16)
    n_t = SEQ // mode.tb
    rows = mode.tb * BATCH
    width = PAIRS * LANES
    n_half = D_MODEL // width

    fwd = pl.BlockSpec((rows, width), lambda h, t: (t, h))
    bwd = pl.BlockSpec((rows, width), lambda h, t: (n_t - 1 - t, h))
    st_spec = pl.BlockSpec((None, 2, HEAD, HEAD, LANES), lambda h, t: (h, 0, 0, 0, 0))
    y_shape = jax.ShapeDtypeStruct((N_PROMPT, D_MODEL), F32)
    args = []
    for x_f, x_b in streams:
        args += [x_f, x_b]
    y_fp, y_bp, s_fin = pl.pallas_call(
        functools.partial(_rwkv_scan_kernel, mode=mode, has_s0=False, t_axis=1),
        out_shape=(y_shape, y_shape, jax.ShapeDtypeStruct((n_half, 2, HEAD, HEAD, LANES), F32)),
        grid=(n_half, n_t),
        in_specs=[fwd, bwd] * 6,
        out_specs=(fwd, bwd, st_spec),
        scratch_shapes=_scan_scratch(mode, 6),
        compiler_params=_cparams(("parallel", "arbitrary")),
        name="rwkv_scan_prompt",
    )(*args)
    s_fin = s_fin.reshape(n_half, 2, HEAD, HEAD, 2, PAIRS, BATCH)
    s_fin = jnp.transpose(s_fin, (6, 4, 0, 5, 1, 3, 2)).reshape(BATCH, 2, H_C, N_C, N_C)

    mode = _ScanMode(prompt=False, groups=1, rep=2, r2=H_C, tb=32)
    n_t = DEC_SEQ // mode.tb

    def view(x):
        return x[N_PROMPT:].reshape(DEC_BATCH, DEC_SEQ, H_C, HEAD)

    fwd = pl.BlockSpec((DEC_BATCH, mode.tb, H_C, HEAD), lambda t: (0, t, 0, 0))
    bwd = pl.BlockSpec((DEC_BATCH, mode.tb, H_C, HEAD), lambda t: (0, n_t - 1 - t, 0, 0))
    vsub = HEAD // mode.rep
    s0 = state0.reshape(DEC_BATCH, 2, H_C, mode.rep, vsub, N_C)
    s0 = jnp.transpose(s0, (5, 4, 3, 1, 0, 2)).reshape(1, HEAD, vsub, LANES)
    st_spec = pl.BlockSpec((1, HEAD, vsub, LANES), lambda t: (0, 0, 0, 0))
    y_shape = jax.ShapeDtypeStruct((DEC_BATCH, DEC_SEQ, H_C, HEAD), F32)
    args = []
    for x_f, x_b in streams:
        x_fv = view(x_f)
        args += [x_fv, x_fv if x_b is x_f else view(x_b)]
    y_fl, y_bl, _ = pl.pallas_call(
        functools.partial(_rwkv_scan_kernel, mode=mode, has_s0=True, t_axis=0),
        out_shape=(y_shape, y_shape, jax.ShapeDtypeStruct((1, HEAD, vsub, LANES), F32)),
        grid=(n_t,),
        in_specs=[fwd, bwd] * 6 + [st_spec],
        out_specs=(fwd, bwd, st_spec),
        scratch_shapes=_scan_scratch(mode, 6),
        compiler_params=_cparams(("arbitrary",)),
        name="rwkv_scan_latent",
    )(*args, s0)
    return y_fp, y_bp, y_fl.reshape(N_SAMPLE, D_MODEL), y_bl.reshape(N_SAMPLE, D_MODEL), s_fin


def _prompt_spec(width):
    return pl.BlockSpec((ROW_TILE, width), lambda i: (jnp.minimum(i, BATCH - 1), 0))


def _latent_spec(width):
    return pl.BlockSpec((ROW_TILE, width), lambda i: (jnp.maximum(i - BATCH, 0), 0))


def _hgrn_post_kernel(ofp_ref, obp_ref, ofl_ref, obl_ref, g_ref, gain_ref, bo_ref, o_ref):
    is_prompt = pl.program_id(0) < BATCH
    o = jnp.where(is_prompt, ofp_ref[...] + obp_ref[...], ofl_ref[...] + obl_ref[...])
    ms = _head_sum(o * o, bo_ref[...]) * (1.0 / DV_A)
    y = o * lax.rsqrt(ms + EPS) * gain_ref[...]
    g = g_ref[...]
    o_ref[...] = (y * (g * jax.nn.sigmoid(g))).astype(o_ref.dtype)


def _hgrn_post(o_fp, o_bp, o_fl, o_bl, slabs, gain):
    w = H_A * DV_A
    return pl.pallas_call(
        _hgrn_post_kernel,
        out_shape=jax.ShapeDtypeStruct((N_TOK, w), BF16),
        grid=(N_TILES,),
        in_specs=[_prompt_spec(w), _prompt_spec(w), _latent_spec(w), _latent_spec(w),
                  pl.BlockSpec((None, ROW_TILE, w), lambda i: (4, i, 0)),
                  _vec_spec(w), pl.BlockSpec((LANES, LANES), lambda i: (0, 0))],
        out_specs=_row_spec(w),
        compiler_params=_cparams(("parallel",)),
        name="hgrn_post",
    )(o_fp, o_bp, o_fl, o_bl, slabs, jnp.tile(gain, H_A).reshape(1, w), _block_ones())


def _q_up_kernel(cq_ref, gain_ref, w_ref, cos_ref, sin_ref, qn_ref, qr_ref):
    cq = _rms(cq_ref[...], gain_ref[...]).astype(BF16)
    q = jnp.dot(cq, w_ref[...], preferred_element_type=F32)
    qn_ref[...] = q[:, :512].astype(BF16)
    qr_ref[...] = (q[:, 512:768] * cos_ref[...] + q[:, 768:1024] * sin_ref[...]).astype(BF16)


def _q_up(tail, gain, w, cos, sin):
    return pl.pallas_call(
        _q_up_kernel,
        out_shape=(jax.ShapeDtypeStruct((N_TOK, 512), BF16), jax.ShapeDtypeStruct((N_TOK, 256), BF16)),
        grid=(N_TILES,),
        in_specs=[_row_spec(Q_RANK, T_CQ // Q_RANK),
                  _vec_spec(Q_RANK), pl.BlockSpec((Q_RANK, 1024), lambda i: (0, 0)),
                  _row_spec(256), _row_spec(256)],
        out_specs=(_row_spec(512), _row_spec(256)),
        compiler_params=_cparams(("parallel",)),
        name="q_up",
    )(tail, gain.reshape(1, Q_RANK), w, cos, sin)


def _kv_up_kernel(ckv_ref, kr_ref, krot_ref, gain_ref, w_ref, cos_ref, sin_ref, ckvn_ref, kn_ref, v_ref, krope_ref):
    ckv = _rms(ckv_ref[...], gain_ref[...])
    ckvn_ref[...] = ckv
    kv = jnp.dot(ckv.astype(BF16), w_ref[...], preferred_element_type=F32)
    kn_ref[...] = kv[:, :512].astype(BF16)
    v_ref[...] = kv[:, 512:].astype(BF16)
    krope_ref[...] = (kr_ref[...] * cos_ref[...] + krot_ref[...] * sin_ref[...]).astype(BF16)


def _kv_up(tail, gain, w, cos4, sin4):
    return pl.pallas_call(
        _kv_up_kernel,
        out_shape=(jax.ShapeDtypeStruct((N_TOK, KV_RANK), F32), jax.ShapeDtypeStruct((N_TOK, 512), BF16),
                   jax.ShapeDtypeStruct((N_TOK, 512), BF16), jax.ShapeDtypeStruct((N_TOK, LANES), BF16)),
        grid=(N_TILES,),
        in_specs=[_row_spec(LANES, T_CKV // LANES), _row_spec(LANES, T_KR // LANES), _row_spec(LANES, T_KROT // LANES),
                  _vec_spec(KV_RANK), pl.BlockSpec((KV_RANK, 1024), lambda i: (0, 0)),
                  _row_spec(LANES), _row_spec(LANES)],
        out_specs=(_row_spec(KV_RANK), _row_spec(512), _row_spec(512), _row_spec(LANES)),
        compiler_params=_cparams(("parallel",)),
        name="kv_up",
    )(tail, tail, tail, gain.reshape(1, KV_RANK), w, cos4, sin4)


def _attn_kernel(qn_ref, qr_ref, kn_ref, kr_ref, v_ref, o_ref, *, scale):
    hp = pl.program_id(1)
    qn = qn_ref[0]
    qr = qr_ref[0]
    kcat = jnp.concatenate([kn_ref[0], kr_ref[0]], axis=-1)
    v = v_ref[0]
    lane = lax.broadcasted_iota(jnp.int32, (1, LANES), 1)
    zero = jnp.zeros((), BF16)
    outs = []
    for j in range(2):
        mn = (lane // D_NOPE) == j
        mr = (lane // D_ROPE) == (2 * hp + j) % 4
        qcat = jnp.concatenate([jnp.where(mn, qn, zero), jnp.where(mr, qr, zero)], axis=-1)
        s = lax.dot_general(qcat, kcat, (((1,), (1,)), ((), ())), preferred_element_type=F32) * scale
        p = jnp.exp(s - jnp.max(s, axis=-1, keepdims=True))
        l = jnp.sum(p, axis=-1, keepdims=True)
        outs.append(jnp.dot(p.astype(BF16), v, preferred_element_type=F32) / l)
    o_ref[0] = jnp.where(lane < DV_B, outs[0], outs[1]).astype(o_ref.dtype)


def _attention(qn, qr, kn, kr, v, bsz, time_major, tq=256):
    t_len, tk = qn.shape[1], kn.shape[1]

    def spec(is_query, per_seq, col):
        rows = tq if is_query else tk
        if time_major:
            return pl.BlockSpec((1, rows, LANES), lambda b, h, q: (0, q if is_query else 0, b * per_seq + col(h)))
        return pl.BlockSpec((1, rows, LANES), lambda b, h, q: (b, q if is_query else 0, col(h)))

    return pl.pallas_call(
        functools.partial(_attn_kernel, scale=float((D_NOPE + D_ROPE) ** -0.5)),
        out_shape=jax.ShapeDtypeStruct(qn.shape, BF16),
        grid=(bsz, 4, t_len // tq),
        in_specs=[spec(True, 4, lambda h: h), spec(True, 2, lambda h: h // 2),
                  spec(False, 4, lambda h: h), spec(False, 1, lambda h: 0), spec(False, 4, lambda h: h)],
        out_specs=spec(True, 4, lambda h: h),
        compiler_params=_cparams(("parallel", "parallel", "parallel")),
        name="attention",
    )(qn, qr, kn, kr, v)


def _odd_pre_kernel(x_ref, xp_ref, xn_ref, g_ref, sc_ref, sh_ref, mu_ref, *o_refs):
    i = pl.program_id(0)
    g, sc, sh = g_ref[...], sc_ref[0], sh_ref[0]

    def nm(x):
        return _rms(x, g) * (1.0 + sc) + sh

    h = nm(x_ref[...])
    hp = nm(xp_ref[...])
    hn = nm(xn_ref[...])
    is_prompt = i < BATCH
    hp_p = jnp.where(i > 0, hp, 0.0)
    hn_p = jnp.where(i < BATCH - 1, hn, 0.0)
    prev_p = jnp.concatenate([hp_p, h[:ROW_TILE - BATCH]], axis=0)
    next_p = jnp.concatenate([h[BATCH:], hn_p], axis=0)
    part = (i - BATCH) % 4
    hp_s = jnp.where(part != 0, hp[HALO - 1:HALO], 0.0)
    hn_s = jnp.where(part != 3, hn[0:1], 0.0)
    row = lax.broadcasted_iota(jnp.int32, (ROW_TILE, 1), 0)
    prev_s = jnp.where(row == 0, hp_s, pltpu.roll(h, 1, axis=0))
    next_s = jnp.where(row == ROW_TILE - 1, hn_s, pltpu.roll(h, ROW_TILE - 1, axis=0))
    prev = jnp.where(is_prompt, prev_p, prev_s)
    nxt = jnp.where(is_prompt, next_p, next_s)
    xx = 0.5 * (prev + nxt) - h
    for j, o_ref in enumerate(o_refs):
        o_ref[...] = (h + xx * mu_ref[j:j + 1, :]).astype(o_ref.dtype)


def _odd_pre(x, g, sc, sh, mu):
    assert HALO == BATCH
    per = ROW_TILE // HALO
    last = N_TOK // HALO - 1
    return pl.pallas_call(
        _odd_pre_kernel,
        out_shape=tuple(jax.ShapeDtypeStruct((N_TOK, D_MODEL), BF16) for _ in range(6)),
        grid=(N_TILES,),
        in_specs=[_row_spec(),
                  pl.BlockSpec((HALO, D_MODEL), lambda i: (jnp.maximum(i * per - 1, 0), 0)),
                  pl.BlockSpec((HALO, D_MODEL), lambda i: (jnp.minimum((i + 1) * per, last), 0)),
                  _vec_spec(), _tile_spec(), _tile_spec(),
                  pl.BlockSpec((8, D_MODEL), lambda i: (0, 0))],
        out_specs=tuple(_row_spec() for _ in range(6)),
        compiler_params=_cparams(("parallel",)),
        name="odd_pre",
    )(x, x, x, g.reshape(1, D_MODEL), sc, sh, jnp.pad(mu, ((0, 2), (0, 0))))


def _rwkv_prep_kernel(k_ref, a_ref, kk_w_ref, ka_w_ref, bo_ref, kk_ref, k2_ref):
    k = k_ref[...]
    kk = k * kk_w_ref[...]
    kk_ref[...] = kk * lax.rsqrt(_head_sum(kk * kk, bo_ref[...]) + 1e-12)
    k2_ref[...] = k * (1.0 + (a_ref[...] - 1.0) * ka_w_ref[...])


def _rwkv_prep(k, a, k_k, k_a):
    return pl.pallas_call(
        _rwkv_prep_kernel,
        out_shape=(jax.ShapeDtypeStruct((N_TOK, D_MODEL), F32), jax.ShapeDtypeStruct((N_TOK, D_MODEL), F32)),
        grid=(N_TILES,),
        in_specs=[_row_spec(), _row_spec(), _vec_spec(), _vec_spec(),
                  pl.BlockSpec((LANES, LANES), lambda i: (0, 0))],
        out_specs=(_row_spec(), _row_spec()),
        compiler_params=_cparams(("parallel",)),
        name="rwkv_prep",
    )(k, a, k_k.reshape(1, D_MODEL), k_a.reshape(1, D_MODEL), _block_ones())


def _rwkv_post_kernel(yfp_ref, ybp_ref, yfl_ref, ybl_ref, r_ref, k_ref, v_ref, g_ref, rk_ref, lnw_ref, lnb_ref,
                      bo_ref, o_ref):
    bo = bo_ref[...]
    y = jnp.where(pl.program_id(0) < BATCH, yfp_ref[...] + ybp_ref[...], yfl_ref[...] + ybl_ref[...])
    d = y - _head_sum(y, bo) * (1.0 / N_C)
    var = _head_sum(d * d, bo) * (1.0 / N_C)
    yn = d * lax.rsqrt(var + GN_EPS) * lnw_ref[...] + lnb_ref[...]
    bonus = _head_sum(r_ref[...] * k_ref[...] * rk_ref[...], bo) * v_ref[...]
    o_ref[...] = ((yn + bonus) * g_ref[...]).astype(o_ref.dtype)


def _rwkv_post(y_fp, y_bp, y_fl, y_bl, r, k2, v, g, r_k, ln_w, ln_b):
    return pl.pallas_call(
        _rwkv_post_kernel,
        out_shape=jax.ShapeDtypeStruct((N_TOK, D_MODEL), BF16),
        grid=(N_TILES,),
        in_specs=([_prompt_spec(D_MODEL)] * 2 + [_latent_spec(D_MODEL)] * 2 + [_row_spec()] * 4 + [_vec_spec()] * 3
                  + [pl.BlockSpec((LANES, LANES), lambda i: (0, 0))]),
        out_specs=_row_spec(),
        compiler_params=_cparams(("parallel",)),
        name="rwkv_post",
    )(y_fp, y_bp, y_fl, y_bl, r, k2, v, g, r_k.reshape(1, D_MODEL), ln_w.reshape(1, D_MODEL),
      ln_b.reshape(1, D_MODEL), _block_ones())


def _rot_cols(w):
    wb = w.reshape(w.shape[:-1] + (2, 2, D_ROPE // 4))
    return jnp.concatenate([-wb[..., 1:, :], wb[..., :1, :]], axis=-2).reshape(w.shape)


def _rope_tables():
    rows = DEC_SEQ // GRID_W
    row, col = np.meshgrid(np.arange(rows), np.arange(GRID_W), indexing='ij')
    row = jnp.asarray(row.reshape(-1), F32)
    col = jnp.asarray(col.reshape(-1), F32)
    n_freq = D_ROPE // 4
    inv = 1.0 / (ROPE_BASE ** (jnp.arange(n_freq, dtype=F32) / n_freq))
    ar = row[:, None] * inv
    ac = col[:, None] * inv
    ang = jnp.concatenate([ar, ar, ac, ac], axis=-1)
    cos, sin = jnp.cos(ang), jnp.sin(ang)
    cos = jnp.concatenate([jnp.ones((N_PROMPT, D_ROPE), F32), jnp.tile(cos, (DEC_BATCH, 1))], axis=0)
    sin = jnp.concatenate([jnp.zeros((N_PROMPT, D_ROPE), F32), jnp.tile(sin, (DEC_BATCH, 1))], axis=0)
    return cos, sin


def _even_mixer(j, h, h_lo, lb_all, cos, sin, cache_ckv, cache_krope, state_hgrn,
                ev_w_in, hgrn_norm, mla_q_norm, mla_w_q_up, mla_kv_norm, mla_w_kv_up):
    w_in = ev_w_in[j]
    n_a = 5 * H_A * DK_A
    w_kr = w_in[:, n_a + Q_RANK + KV_RANK:]
    w_tail = jnp.concatenate([w_in[:, n_a:n_a + Q_RANK + KV_RANK], jnp.tile(w_kr, (1, 4)),
                              jnp.tile(_rot_cols(w_kr), (1, 4))], axis=1).astype(BF16)
    slabs = _mm_split(h, h_lo, w_in[:, :n_a], H_A * DK_A, precise=(1, 2), name="hgrn_proj")
    tail = _mm(h, w_tail, F32, tm=512, tn=T_N, name="mla_proj")

    o_fp, o_bp, o_fl, o_bl, s_h = _hgrn_scans(slabs, lb_all[j], state_hgrn[:, j])
    o_a = _hgrn_post(o_fp, o_bp, o_fl, o_bl, slabs, hgrn_norm[j])

    wq = mla_w_q_up[j].reshape(Q_RANK, H_B, D_NOPE + D_ROPE)
    wq_n = wq[:, :, :D_NOPE].reshape(Q_RANK, H_B * D_NOPE)
    wq_r = wq[:, :, D_NOPE:]
    wq_aug = jnp.concatenate([wq_n, wq_r.reshape(Q_RANK, -1), _rot_cols(wq_r).reshape(Q_RANK, -1)], axis=1).astype(BF16)
    qn, qr = _q_up(tail, mla_q_norm[j], wq_aug, jnp.tile(cos, (1, H_B)), jnp.tile(sin, (1, H_B)))
    wkv = mla_w_kv_up[j].reshape(KV_RANK, H_B, D_NOPE + DV_B)
    wkv_aug = jnp.concatenate([wkv[:, :, :D_NOPE].reshape(KV_RANK, -1), wkv[:, :, D_NOPE:].reshape(KV_RANK, -1)],
                              axis=1).astype(BF16)
    ckv_n, kn, vv, kr = _kv_up(tail, mla_kv_norm[j], wkv_aug, jnp.tile(cos, (1, 4)), jnp.tile(sin, (1, 4)))
    kv_ctx = _mm(cache_ckv[:, j].reshape(DEC_BATCH * PAST_LEN, KV_RANK), wkv_aug, BF16, name="kv_ctx")
    kn_c = kv_ctx[:, :512].reshape(DEC_BATCH, PAST_LEN, 512)
    v_c = kv_ctx[:, 512:].reshape(DEC_BATCH, PAST_LEN, 512)
    kr_c = jnp.tile(cache_krope[:, j], (1, 1, 4)).astype(BF16)

    def prompt(x):
        return x[:N_PROMPT].reshape(1, SEQ, BATCH * x.shape[-1])

    def latent(x):
        return x[N_PROMPT:].reshape(DEC_BATCH, DEC_SEQ, x.shape[-1])

    o_p = _attention(prompt(qn), prompt(qr), prompt(kn), prompt(kr), prompt(vv), BATCH, True)
    o_s = _attention(latent(qn), latent(qr),
                     jnp.concatenate([kn_c, latent(kn)], axis=1),
                     jnp.concatenate([kr_c, latent(kr)], axis=1),
                     jnp.concatenate([v_c, latent(vv)], axis=1), DEC_BATCH, False)
    o_att = jnp.concatenate([o_p.reshape(N_PROMPT, 512), o_s.reshape(N_SAMPLE, 512)], axis=0)

    new_ckv = jnp.transpose(ckv_n[:N_PROMPT].reshape(SEQ, BATCH, KV_RANK), (1, 0, 2))
    new_krope = jnp.transpose(tail[:N_PROMPT, T_KR:T_KR + D_ROPE].reshape(SEQ, BATCH, D_ROPE), (1, 0, 2))
    return o_a, o_att, (s_h, new_ckv, new_krope)


def _odd_mixer(j, x, g0, sc, sh, state_rwkv, rw_mu, rw_w_r, rw_w_k, rw_w_v, rw_w0, rw_w1, rw_w2, rw_a0, rw_a1, rw_a2,
               rw_g1, rw_g2, rw_k_k, rw_k_a, rw_r_k, rw_ln_w, rw_ln_b):
    xr, xw, xk, xv, xa, xg = _odd_pre(x, g0, sc, sh, rw_mu[j])
    r = _mm(xr, rw_w_r[j].astype(BF16), name="rwkv_r")
    k = _mm(xk, rw_w_k[j].astype(BF16), name="rwkv_k")
    v = _mm(xv, rw_w_v[j].astype(BF16), name="rwkv_v")
    a = _lora(xa, rw_a1[j], rw_a2[j], rw_a0[j], _identity, jax.nn.sigmoid)
    g = _lora(xg, rw_g1[j], rw_g2[j], jnp.zeros((D_MODEL,), F32), jax.nn.sigmoid, _identity)
    w_f = _lora(xw, rw_w1[j, 0], rw_w2[j, 0], rw_w0[j, 0], jnp.tanh, _decay)
    w_b = _lora(xw, rw_w1[j, 1], rw_w2[j, 1], rw_w0[j, 1], jnp.tanh, _decay)
    kk, k2 = _rwkv_prep(k, a, rw_k_k[j], rw_k_a[j])
    y_fp, y_bp, y_fl, y_bl, s_r = _rwkv_scans(r, w_f, w_b, k2, v, kk, a, state_rwkv[:, j])
    mix = _rwkv_post(y_fp, y_bp, y_fl, y_bl, r, k2, v, g, rw_r_k[j], rw_ln_w[j], rw_ln_b[j])
    return mix, s_r


def kernel(x_prompt, x_sample, cache_ckv, cache_krope, state_hgrn, state_rwkv, c, c_ctx, ada_w, ada_b, norm_gains, ev_w_in, hgrn_lb_logits, hgrn_norm, mla_q_norm, mla_w_q_up, mla_kv_norm, mla_w_kv_up, ev_w_out, rw_mu, rw_w_r, rw_w_k, rw_w_v, rw_w_o, rw_w0, rw_w1, rw_w2, rw_a0, rw_a1, rw_a2, rw_g1, rw_g2, rw_k_k, rw_k_a, rw_r_k, rw_ln_w, rw_ln_b, ffn_w_gate, ffn_w_up, ffn_w_down, moe_router, moe_w_gate, moe_w_up, moe_w_down):
    lb_all = jnp.cumsum(jax.nn.softmax(hgrn_lb_logits.astype(F32), axis=0), axis=0)
    lb_all = lb_all - lb_all[:1]
    cos, sin = _rope_tables()

    cond8 = jnp.zeros((8, D_MODEL), F32).at[0].set(c_ctx).at[1:1 + DEC_BATCH].set(c)
    mod = _modulation(cond8, ada_w, ada_b)
    tile_row = np.array([0] * BATCH + [1 + t // 4 for t in range(N_TILES - BATCH)])
    modt = mod[:, tile_row, :].reshape(DEPTH, N_TILES, 6, 1, D_MODEL)

    x = jnp.concatenate([jnp.transpose(x_prompt, (1, 0, 2)).reshape(N_PROMPT, D_MODEL),
                         x_sample.reshape(N_SAMPLE, D_MODEL)], axis=0)
    ckv_l, krope_l, hgrn_l, rwkv_l = [], [], [], []
    for l in range(DEPTH):
        j = l // 2
        sh_m, sc_m, gt_m, sh_f, sc_f, gt_f = [modt[l, :, i] for i in range(6)]
        if l % 2 == 0:
            h, h_lo = _norm_mod(x, norm_gains[l, 0], sc_m, sh_m, with_residual=True)
            o_a, o_att, (s_h, n_ckv, n_kr) = _even_mixer(j, h, h_lo, lb_all, cos, sin, cache_ckv, cache_krope, state_hgrn,
                                                        ev_w_in, hgrn_norm, mla_q_norm, mla_w_q_up, mla_kv_norm,
                                                        mla_w_kv_up)
            hgrn_l.append(s_h)
            ckv_l.append(n_ckv)
            krope_l.append(n_kr)
            x = _mm2_resid(o_a, o_att, ev_w_out[j].astype(BF16), x, norm_gains[l, 1], gt_m)
            h = _norm_mod(x, norm_gains[l, 2], sc_f, sh_f)
            ones = jnp.ones((1, N_TOK, 1), F32)
            hid = _swiglu(h, ffn_w_gate[j][None].astype(BF16), ffn_w_up[j][None].astype(BF16), ones)
            x = _mm_resid(hid, ffn_w_down[j].astype(BF16), x, norm_gains[l, 3], gt_f)
        else:
            mix, s_r = _odd_mixer(j, x, norm_gains[l, 0], sc_m, sh_m, state_rwkv, rw_mu, rw_w_r, rw_w_k, rw_w_v,
                                  rw_w0, rw_w1, rw_w2, rw_a0, rw_a1, rw_a2, rw_g1, rw_g2, rw_k_k, rw_k_a, rw_r_k,
                                  rw_ln_w, rw_ln_b)
            rwkv_l.append(s_r)
            x = _mm_resid(mix, rw_w_o[j].astype(BF16), x, norm_gains[l, 1], gt_m)
            h, gate = _norm_mod_router(x, norm_gains[l, 2], sc_f, sh_f, moe_router[j])
            gate_e = jnp.transpose(gate[:, :N_EXPERTS])[:, :, None]
            hid = _swiglu(h, moe_w_gate[j].astype(BF16), moe_w_up[j].astype(BF16), gate_e)
            ff = _mm_acc(hid, moe_w_down[j].reshape(N_EXPERTS * D_FF, D_MODEL).astype(BF16), 1024, 1024, D_FF)
            x = _resid_norm(x, ff, norm_gains[l, 3], gt_f)

    y_prompt = jnp.transpose(x[:N_PROMPT].reshape(SEQ, BATCH, D_MODEL), (1, 0, 2))
    y_sample = x[N_PROMPT:].reshape(DEC_BATCH, DEC_SEQ, D_MODEL)
    return (y_prompt, y_sample, jnp.stack(ckv_l, axis=1), jnp.stack(krope_l, axis=1),
            jnp.stack(hgrn_l, axis=1), jnp.stack(rwkv_l, axis=1))
```

```python
import functools
from typing import NamedTuple

import numpy as np
import jax
import jax.numpy as jnp
from jax import lax
from jax.experimental import pallas as pl
from jax.experimental.pallas import tpu as pltpu

D_MODEL = 1024
BATCH = 16
SEQ = 256
DEPTH = 4
DEC_BATCH = 2
DEC_SEQ = 1024
PAST_LEN = 256
GRID_W = 64
H_A = 8
DK_A = 64
DV_A = 64
H_B = 8
Q_RANK = 256
KV_RANK = 128
D_NOPE = 64
D_ROPE = 32
DV_B = 64
ROPE_BASE = 10000.0
H_C = 16
N_C = 64
D_FF = 2816
N_EXPERTS = 8
EPS = 1e-6
GN_EPS = 64e-5

F32 = jnp.float32
BF16 = jnp.bfloat16
HIGHEST = lax.Precision.HIGHEST

N_PROMPT = BATCH * SEQ
N_SAMPLE = DEC_BATCH * DEC_SEQ
N_TOK = N_PROMPT + N_SAMPLE
ROW_TILE = 256
N_TILES = N_TOK // ROW_TILE
LANES = 128
HEAD = 64
VMEM_LIMIT = 48 * 1024 * 1024
HALO = 16
PAIRS = 4
FF_TN = 1408
MOE_TM = 256
MOE_ROWS = 2 * N_TOK + N_EXPERTS * MOE_TM
MOE_TILES = MOE_ROWS // MOE_TM
RWKV_CHUNK = 64

T_CQ, T_CKV, T_KR, T_KROT, T_N = 0, 256, 384, 512, 640


def _cparams(sem):
    return pltpu.CompilerParams(dimension_semantics=sem, vmem_limit_bytes=VMEM_LIMIT)


def _mod_kernel(c_ref, w_ref, b_ref, o_ref):
    c = c_ref[...]
    s = (c * jax.nn.sigmoid(c)).astype(BF16)
    o_ref[0] = jnp.dot(s, w_ref[0].astype(BF16), preferred_element_type=F32) + b_ref[0]


def _modulation(cond8, ada_w, ada_b):
    tn = 768
    return pl.pallas_call(
        _mod_kernel,
        out_shape=jax.ShapeDtypeStruct((DEPTH, 8, 6 * D_MODEL), F32),
        grid=(DEPTH, 6 * D_MODEL // tn),
        in_specs=[pl.BlockSpec((8, D_MODEL), lambda l, n: (0, 0)),
                  pl.BlockSpec((1, D_MODEL, tn), lambda l, n: (l, 0, n)),
                  pl.BlockSpec((1, 1, tn), lambda l, n: (l, 0, n))],
        out_specs=pl.BlockSpec((1, 8, tn), lambda l, n: (l, 0, n)),
        compiler_params=_cparams(("parallel", "parallel")),
        name="modulation",
    )(cond8, ada_w, ada_b.reshape(DEPTH, 1, 6 * D_MODEL))


def _rms(x, g):
    return x * lax.rsqrt(jnp.mean(x * x, axis=-1, keepdims=True) + EPS) * g


def _norm_mod_kernel(x_ref, g_ref, sc_ref, sh_ref, o_ref):
    h = _rms(x_ref[...], g_ref[...]) * (1.0 + sc_ref[0]) + sh_ref[0]
    o_ref[...] = h.astype(o_ref.dtype)


def _tile_spec():
    return pl.BlockSpec((1, 1, D_MODEL), lambda i: (i, 0, 0))


def _row_spec(width=D_MODEL, col=0):
    return pl.BlockSpec((ROW_TILE, width), lambda i: (i, col))


def _vec_spec(width=D_MODEL):
    return pl.BlockSpec((1, width), lambda i: (0, 0))


def _norm_mod(x, g, sc, sh):
    return pl.pallas_call(
        _norm_mod_kernel,
        out_shape=jax.ShapeDtypeStruct((N_TOK, D_MODEL), BF16),
        grid=(N_TILES,),
        in_specs=[_row_spec(), _vec_spec(), _tile_spec(), _tile_spec()],
        out_specs=_row_spec(),
        compiler_params=_cparams(("parallel",)),
        name="norm_mod",
    )(x, g.reshape(1, D_MODEL), sc, sh)


def _norm_mod_router_kernel(x_ref, g_ref, sc_ref, sh_ref, wr_ref, o_ref, route_ref):
    h = _rms(x_ref[...], g_ref[...]) * (1.0 + sc_ref[0]) + sh_ref[0]
    bits = lax.bitcast_convert_type(h.astype(BF16).astype(F32), jnp.uint32)
    half = D_MODEL // 2
    o_ref[...] = lax.shift_right_logical(bits[:, :half], jnp.uint32(16)) | (bits[:, half:] & jnp.uint32(0xFFFF0000))
    logits = jnp.dot(h, wr_ref[...], precision=HIGHEST, preferred_element_type=F32)
    lane = lax.broadcasted_iota(jnp.int32, logits.shape, 1).astype(F32)
    neg = jnp.float32(-jnp.inf)
    lg = jnp.where(lane < N_EXPERTS, logits, neg)
    m1 = jnp.max(lg, axis=-1, keepdims=True)
    i1 = jnp.min(jnp.where(lg == m1, lane, float(LANES)), axis=-1, keepdims=True)
    lg2 = jnp.where(lane == i1, neg, lg)
    m2 = jnp.max(lg2, axis=-1, keepdims=True)
    i2 = jnp.min(jnp.where(lg2 == m2, lane, float(LANES)), axis=-1, keepdims=True)
    e = jnp.exp(m2 - m1)
    w1 = 1.0 / (1.0 + e)
    w2 = e * w1
    route_ref[...] = jnp.where(lane == 0.0, i1, jnp.where(lane == 1.0, i2, jnp.where(lane == 2.0, w1,
                                                                                    jnp.where(lane == 3.0, w2, 0.0))))


def _norm_mod_router(x, g, sc, sh, w_router):
    wr = jnp.pad(w_router, ((0, 0), (0, LANES - N_EXPERTS)))
    return pl.pallas_call(
        _norm_mod_router_kernel,
        out_shape=(jax.ShapeDtypeStruct((N_TOK, D_MODEL // 2), jnp.uint32),
                   jax.ShapeDtypeStruct((N_TOK, LANES), F32)),
        grid=(N_TILES,),
        in_specs=[_row_spec(), _vec_spec(), _tile_spec(), _tile_spec(),
                  pl.BlockSpec((D_MODEL, LANES), lambda i: (0, 0))],
        out_specs=(_row_spec(D_MODEL // 2), _row_spec(LANES)),
        compiler_params=_cparams(("parallel",)),
        name="norm_mod_router",
    )(x, g.reshape(1, D_MODEL), sc, sh, wr)


def _mm_kernel(a_ref, w_ref, o_ref):
    o_ref[...] = jnp.dot(a_ref[...].astype(BF16), w_ref[...], preferred_element_type=F32).astype(o_ref.dtype)


def _mm(a, w, out_dtype=F32, tm=512, tn=512, name="mm"):
    m, k = a.shape
    n = w.shape[1]
    tm, tn = min(tm, m), min(tn, n)
    return pl.pallas_call(
        _mm_kernel,
        out_shape=jax.ShapeDtypeStruct((m, n), out_dtype),
        grid=(n // tn, m // tm),
        in_specs=[pl.BlockSpec((tm, k), lambda j, i: (i, 0)),
                  pl.BlockSpec((k, tn), lambda j, i: (0, j))],
        out_specs=pl.BlockSpec((tm, tn), lambda j, i: (i, j)),
        compiler_params=_cparams(("parallel", "parallel")),
        name=name,
    )(a, w)


def _mm_split(a, w, tn, tm=512, name="mm_split"):
    m, k = a.shape
    p = w.shape[1] // tn
    return pl.pallas_call(
        _mm_kernel,
        out_shape=jax.ShapeDtypeStruct((p, m, tn), F32),
        grid=(p, m // tm),
        in_specs=[pl.BlockSpec((tm, k), lambda j, i: (i, 0)),
                  pl.BlockSpec((k, tn), lambda j, i: (0, j))],
        out_specs=pl.BlockSpec((None, tm, tn), lambda j, i: (j, i, 0)),
        compiler_params=_cparams(("parallel", "parallel")),
        name=name,
    )(a, w)


def _mm_resid_kernel(a_ref, w_ref, x_ref, g_ref, gt_ref, o_ref):
    y = jnp.dot(a_ref[...], w_ref[...], preferred_element_type=F32)
    o_ref[...] = x_ref[...] + gt_ref[0] * _rms(y, g_ref[...])


def _mm_resid(a, w, x, g, gt):
    k = a.shape[1]
    return pl.pallas_call(
        _mm_resid_kernel,
        out_shape=jax.ShapeDtypeStruct((N_TOK, D_MODEL), F32),
        grid=(N_TILES,),
        in_specs=[_row_spec(k), pl.BlockSpec((k, D_MODEL), lambda i: (0, 0)),
                  _row_spec(), _vec_spec(), _tile_spec()],
        out_specs=_row_spec(),
        compiler_params=_cparams(("parallel",)),
        name="mm_resid",
    )(a, w, x, g.reshape(1, D_MODEL), gt)


def _mm2_resid_kernel(a1_ref, a2_ref, w_ref, x_ref, g_ref, gt_ref, o_ref):
    k1 = a1_ref.shape[1]
    y = (jnp.dot(a1_ref[...], w_ref[:k1], preferred_element_type=F32)
         + jnp.dot(a2_ref[...], w_ref[k1:], preferred_element_type=F32))
    o_ref[...] = x_ref[...] + gt_ref[0] * _rms(y, g_ref[...])


def _mm2_resid(a1, a2, w, x, g, gt):
    k1, k2 = a1.shape[1], a2.shape[1]
    return pl.pallas_call(
        _mm2_resid_kernel,
        out_shape=jax.ShapeDtypeStruct((N_TOK, D_MODEL), F32),
        grid=(N_TILES,),
        in_specs=[_row_spec(k1), _row_spec(k2), pl.BlockSpec((k1 + k2, D_MODEL), lambda i: (0, 0)),
                  _row_spec(), _vec_spec(), _tile_spec()],
        out_specs=_row_spec(),
        compiler_params=_cparams(("parallel",)),
        name="mm2_resid",
    )(a1, a2, w, x, g.reshape(1, D_MODEL), gt)


def _swiglu_kernel(a_ref, wg_ref, wu_ref, o_ref):
    a = a_ref[...]
    g = jnp.dot(a, wg_ref[...], preferred_element_type=F32)
    u = jnp.dot(a, wu_ref[...], preferred_element_type=F32)
    o_ref[...] = ((g * jax.nn.sigmoid(g)) * u).astype(o_ref.dtype)


def _swiglu(a, wg, wu, tm=512, tn=FF_TN):
    f = wg.shape[1]
    m = a.shape[0]
    return pl.pallas_call(
        _swiglu_kernel,
        out_shape=jax.ShapeDtypeStruct((m, f), BF16),
        grid=(f // tn, m // tm),
        in_specs=[pl.BlockSpec((tm, D_MODEL), lambda j, i: (i, 0)),
                  pl.BlockSpec((D_MODEL, tn), lambda j, i: (0, j)),
                  pl.BlockSpec((D_MODEL, tn), lambda j, i: (0, j))],
        out_specs=pl.BlockSpec((tm, tn), lambda j, i: (i, j)),
        compiler_params=_cparams(("parallel", "parallel")),
        name="swiglu",
    )(a, wg, wu)


def _moe_plan(route):
    e_flat = route[:, :2].astype(jnp.int32).reshape(-1)
    w_flat = route[:, 2:4].reshape(-1)
    onehot = (e_flat[:, None] == jnp.arange(N_EXPERTS, dtype=jnp.int32)[None, :]).astype(jnp.int32)
    pos = jnp.cumsum(onehot, axis=0) - onehot
    padded = (jnp.sum(onehot, axis=0) + MOE_TM - 1) // MOE_TM * MOE_TM
    ends = jnp.cumsum(padded)
    dest = jnp.sum(onehot * ((ends - padded)[None, :] + pos), axis=1)
    token = jnp.arange(2 * N_TOK, dtype=jnp.int32) // 2
    src_tok = jnp.zeros((MOE_ROWS,), jnp.int32).at[dest].set(token)
    row_w = jnp.zeros((MOE_ROWS,), F32).at[dest].set(w_flat)
    tile_start = jnp.arange(MOE_TILES, dtype=jnp.int32) * MOE_TM
    tile_e = jnp.sum((tile_start[:, None] >= ends[None, :]).astype(jnp.int32), axis=1)
    tile_e = jnp.minimum(tile_e, N_EXPERTS - 1)
    n_used = (ends[-1:] // MOE_TM).astype(jnp.int32)
    return dest, src_tok, row_w.reshape(MOE_ROWS, 1), tile_e, n_used


def _row_gather_kernel(idx_ref, src_ref, o_ref, sem):
    rows = o_ref.shape[0]
    base = pl.program_id(0) * rows

    def issue(r, carry):
        pltpu.make_async_copy(src_ref.at[idx_ref[base + r]], o_ref.at[r], sem).start()
        return carry

    lax.fori_loop(0, rows, issue, 0)

    def wait(r, carry):
        pltpu.make_async_copy(src_ref.at[0], o_ref.at[0], sem).wait()
        return carry

    lax.fori_loop(0, rows, wait, 0)


def _row_gather(idx, src, n_rows, tm):
    w = src.shape[1]
    return pl.pallas_call(
        _row_gather_kernel,
        out_shape=jax.ShapeDtypeStruct((n_rows, w), src.dtype),
        grid_spec=pltpu.PrefetchScalarGridSpec(
            num_scalar_prefetch=1, grid=(n_rows // tm,),
            in_specs=[pl.BlockSpec(memory_space=pl.ANY)],
            out_specs=pl.BlockSpec((tm, w), lambda i, idx_ref: (i, 0)),
            scratch_shapes=[pltpu.SemaphoreType.DMA(())]),
        compiler_params=_cparams(("arbitrary",)),
        name="moe_gather",
    )(idx, src)


def _unpack_halves(w):
    lo = lax.bitcast_convert_type(lax.shift_left(w, jnp.uint32(16)), F32)
    hi = lax.bitcast_convert_type(w & jnp.uint32(0xFFFF0000), F32)
    return lo.astype(BF16), hi.astype(BF16)


def _expert_changed(te_ref, i):
    return jnp.logical_or(i == 0, te_ref[i] != te_ref[jnp.maximum(i - 1, 0)])


def _moe_swiglu_kernel(te_ref, nu_ref, a_ref, wg_ref, wu_ref, rw_ref, o_ref, wg_s, wu_s):
    i = pl.program_id(1)

    @pl.when(_expert_changed(te_ref, i))
    def _():
        wg_s[...] = wg_ref[0].astype(BF16)
        wu_s[...] = wu_ref[0].astype(BF16)

    @pl.when(i < nu_ref[0])
    def _():
        lo, hi = _unpack_halves(a_ref[...])
        half = D_MODEL // 2
        g = (jnp.dot(lo, wg_s[:half], preferred_element_type=F32)
             + jnp.dot(hi, wg_s[half:], preferred_element_type=F32))
        u = (jnp.dot(lo, wu_s[:half], preferred_element_type=F32)
             + jnp.dot(hi, wu_s[half:], preferred_element_type=F32))
        o_ref[...] = (rw_ref[...] * (g * jax.nn.sigmoid(g)) * u).astype(o_ref.dtype)

    @pl.when(i >= nu_ref[0])
    def _():
        o_ref[...] = jnp.zeros_like(o_ref)


def _moe_swiglu(tile_e, n_used, a, wg, wu, row_w, tn=FF_TN):
    f = wg.shape[2]
    w_spec = pl.BlockSpec((1, D_MODEL, tn), lambda j, i, te, nu: (te[i], 0, j))
    return pl.pallas_call(
        _moe_swiglu_kernel,
        out_shape=jax.ShapeDtypeStruct((MOE_ROWS, f), BF16),
        grid_spec=pltpu.PrefetchScalarGridSpec(
            num_scalar_prefetch=2, grid=(f // tn, MOE_TILES),
            in_specs=[pl.BlockSpec((MOE_TM, D_MODEL // 2), lambda j, i, te, nu: (i, 0)), w_spec, w_spec,
                      pl.BlockSpec((MOE_TM, 1), lambda j, i, te, nu: (i, 0))],
            out_specs=pl.BlockSpec((MOE_TM, tn), lambda j, i, te, nu: (i, j)),
            scratch_shapes=[pltpu.VMEM((D_MODEL, tn), BF16), pltpu.VMEM((D_MODEL, tn), BF16)]),
        compiler_params=_cparams(("parallel", "arbitrary")),
        name="moe_swiglu",
    )(tile_e, n_used, a, wg, wu, row_w)


def _moe_down_kernel(te_ref, nu_ref, h_ref, wd_ref, o_ref, wd_s):
    i = pl.program_id(0)

    @pl.when(_expert_changed(te_ref, i))
    def _():
        wd_s[...] = wd_ref[0].astype(BF16)

    @pl.when(i < nu_ref[0])
    def _():
        o_ref[...] = jnp.dot(h_ref[...], wd_s[...], preferred_element_type=F32)

    @pl.when(i >= nu_ref[0])
    def _():
        o_ref[...] = jnp.zeros_like(o_ref)


def _moe_down(tile_e, n_used, hid, wd):
    f = wd.shape[1]
    return pl.pallas_call(
        _moe_down_kernel,
        out_shape=jax.ShapeDtypeStruct((MOE_ROWS, D_MODEL), F32),
        grid_spec=pltpu.PrefetchScalarGridSpec(
            num_scalar_prefetch=2, grid=(MOE_TILES,),
            in_specs=[pl.BlockSpec((MOE_TM, f), lambda i, te, nu: (i, 0)),
                      pl.BlockSpec((1, f, D_MODEL), lambda i, te, nu: (te[i], 0, 0))],
            out_specs=pl.BlockSpec((MOE_TM, D_MODEL), lambda i, te, nu: (i, 0)),
            scratch_shapes=[pltpu.VMEM((f, D_MODEL), BF16)]),
        compiler_params=_cparams(("arbitrary",)),
        name="moe_down",
    )(tile_e, n_used, hid, wd)


def _moe_combine_kernel(dest_ref, y_ref, x_ref, g_ref, gt_ref, o_ref, buf, sem):
    base = pl.program_id(0) * ROW_TILE

    def issue(r, carry):
        a = 2 * (base + r)
        pltpu.make_async_copy(y_ref.at[dest_ref[a]], buf.at[0, r], sem).start()
        pltpu.make_async_copy(y_ref.at[dest_ref[a + 1]], buf.at[1, r], sem).start()
        return carry

    lax.fori_loop(0, ROW_TILE, issue, 0)

    def wait(r, carry):
        pltpu.make_async_copy(y_ref.at[0], buf.at[0, 0], sem).wait()
        pltpu.make_async_copy(y_ref.at[0], buf.at[0, 0], sem).wait()
        return carry

    lax.fori_loop(0, ROW_TILE, wait, 0)
    o_ref[...] = x_ref[...] + gt_ref[0] * _rms(buf[0] + buf[1], g_ref[...])


def _moe_combine(dest, y_sorted, x, g, gt):
    return pl.pallas_call(
        _moe_combine_kernel,
        out_shape=jax.ShapeDtypeStruct((N_TOK, D_MODEL), F32),
        grid_spec=pltpu.PrefetchScalarGridSpec(
            num_scalar_prefetch=1, grid=(N_TILES,),
            in_specs=[pl.BlockSpec(memory_space=pl.ANY),
                      pl.BlockSpec((ROW_TILE, D_MODEL), lambda i, d: (i, 0)),
                      pl.BlockSpec((1, D_MODEL), lambda i, d: (0, 0)),
                      pl.BlockSpec((1, 1, D_MODEL), lambda i, d: (i, 0, 0))],
            out_specs=pl.BlockSpec((ROW_TILE, D_MODEL), lambda i, d: (i, 0)),
            scratch_shapes=[pltpu.VMEM((2, ROW_TILE, D_MODEL), F32), pltpu.SemaphoreType.DMA(())]),
        compiler_params=_cparams(("arbitrary",)),
        name="moe_combine",
    )(dest, y_sorted, x, g.reshape(1, D_MODEL), gt)


def _lora_kernel(x_ref, w1_ref, w2_ref, b_ref, o_ref, *, mid, out):
    t = mid(jnp.dot(x_ref[...], w1_ref[...], preferred_element_type=F32))
    o_ref[...] = out(jnp.dot(t.astype(BF16), w2_ref[...], preferred_element_type=F32) + b_ref[...])


def _lora(x, w1, w2, b, mid, out, tm=512):
    m = x.shape[0]
    r = w1.shape[1]
    return pl.pallas_call(
        functools.partial(_lora_kernel, mid=mid, out=out),
        out_shape=jax.ShapeDtypeStruct((m, D_MODEL), F32),
        grid=(m // tm,),
        in_specs=[pl.BlockSpec((tm, D_MODEL), lambda i: (i, 0)),
                  pl.BlockSpec((D_MODEL, r), lambda i: (0, 0)),
                  pl.BlockSpec((r, D_MODEL), lambda i: (0, 0)),
                  pl.BlockSpec((1, D_MODEL), lambda i: (0, 0))],
        out_specs=pl.BlockSpec((tm, D_MODEL), lambda i: (i, 0)),
        compiler_params=_cparams(("parallel",)),
        name="lora",
    )(x, w1.astype(BF16), w2.astype(BF16), b.reshape(1, D_MODEL))


def _identity(x):
    return x


def _decay(wr):
    return jnp.exp(-float(np.exp(-0.5)) * jax.nn.sigmoid(wr))


def _block_ones():
    i = np.arange(LANES)
    return jnp.asarray((i[:, None] // HEAD == i[None, :] // HEAD).astype(np.float32))


def _head_sum(x, bo):
    parts = [jnp.dot(x[:, c * LANES:(c + 1) * LANES], bo, precision=HIGHEST, preferred_element_type=F32)
             for c in range(x.shape[1] // LANES)]
    return jnp.concatenate(parts, axis=-1)


class _ScanMode(NamedTuple):
    prompt: bool
    groups: int
    rep: int
    r2: int
    tb: int


def _scan_gather(mode, src_ref, fwd_ref, bwd_ref, s):
    sb = mode.tb - 1 - s
    if mode.prompt:
        for d, (ref, tt) in enumerate(((fwd_ref, s), (bwd_ref, sb))):
            z = ref[pl.ds(pl.multiple_of(tt * BATCH, BATCH), BATCH), :]
            for hp in range(PAIRS):
                r0 = d * HEAD + hp * BATCH
                src_ref[r0:r0 + BATCH, :] = z[:, hp * LANES:(hp + 1) * LANES]
        return src_ref[...].T
    unit = 4 * mode.r2
    for q in range(mode.rep):
        for d, (ref, tt) in enumerate(((fwd_ref, s), (bwd_ref, sb))):
            for b in range(DEC_BATCH):
                r0 = q * unit + (2 * d + b) * mode.r2
                src_ref[r0:r0 + mode.r2, 0:HEAD] = ref[b, tt]
    return src_ref[...].T[0:HEAD]


def _scan_value_slab(mode, vt):
    vsub = HEAD // mode.rep
    out = vt[0:vsub]
    if mode.rep > 1:
        lane = lax.broadcasted_iota(jnp.int32, (vsub, LANES), 1)
        for q in range(1, mode.rep):
            out = jnp.where(lane >= q * (LANES // mode.rep), vt[q * vsub:(q + 1) * vsub], out)
    return out


def _scan_scatter(mode, src_ref, ys_ref, yf_ref, yb_ref, s):
    sb = mode.tb - 1 - s
    if mode.prompt:
        tr = jnp.concatenate([ys_ref[s, 0], ys_ref[s, 1]], axis=0).T
        for d, (ref, tt) in enumerate(((yf_ref, s), (yb_ref, sb))):
            row0 = pl.multiple_of(tt * BATCH, BATCH)
            for hp in range(PAIRS):
                r0 = d * HEAD + hp * BATCH
                ref[pl.ds(row0, BATCH), hp * LANES:(hp + 1) * LANES] = tr[r0:r0 + BATCH]
        return
    vsub = HEAD // mode.rep
    y = ys_ref[s, 0]
    for q in range(mode.rep):
        src_ref[q * vsub:(q + 1) * vsub, :] = y
    tr = src_ref[...].T
    unit = 4 * mode.r2
    out = tr[0:unit]
    lane = lax.broadcasted_iota(jnp.int32, (unit, LANES), 1)
    for q in range(1, mode.rep):
        out = jnp.where(lane >= q * vsub, tr[q * unit:(q + 1) * unit], out)
    for d, (ref, tt) in enumerate(((yf_ref, s), (yb_ref, sb))):
        for b in range(DEC_BATCH):
            r0 = (2 * d + b) * mode.r2
            ref[b, tt] = out[r0:r0 + mode.r2, 0:HEAD]


def _n_acc(vsub):
    return 1 if vsub == HEAD else 2


def _scan_init(mode, t_axis, s_ref, s0_ref, src_ref):
    @pl.when(pl.program_id(t_axis) == 0)
    def _():
        if s0_ref is None:
            s_ref[...] = jnp.zeros_like(s_ref)
        else:
            s_ref[...] = s0_ref[...]

    if not mode.prompt:
        src_ref[...] = jnp.zeros_like(src_ref)


def _hgrn_scan_kernel(*refs, mode, has_s0, t_axis):
    qf, qb, xf, xb, vf, vb, lb_ref = refs[:7]
    refs = refs[7:]
    s0_ref = None
    if has_s0:
        s0_ref, refs = refs[0], refs[1:]
    of_ref, ob_ref, s_ref, src_ref, q_t, f_t, k_t, v_t, ys_ref = refs
    vsub = HEAD // mode.rep
    n_acc = _n_acc(vsub)
    _scan_init(mode, t_axis, s_ref, s0_ref, src_ref)
    lb = lb_ref[...]

    def prologue(s, carry):
        q_t[s] = _scan_gather(mode, src_ref, qf, qb, s)
        x = _scan_gather(mode, src_ref, xf, xb, s)
        f_t[s] = lb + (1.0 - lb) * jax.nn.sigmoid(x)
        k_t[s] = (1.0 - lb) * jax.nn.sigmoid(-x)
        v_t[s] = _scan_gather(mode, src_ref, vf, vb, s)
        return carry

    lax.fori_loop(0, mode.tb, prologue, 0)

    def step(s, carry):
        for g in range(mode.groups):
            r0 = g * HEAD
            vv = _scan_value_slab(mode, v_t[s, r0:r0 + HEAD, :])
            acc = [jnp.zeros((vsub, LANES), F32) for _ in range(n_acc)]
            for d in range(HEAD):
                r = r0 + d
                sn = s_ref[g, d] * f_t[s, r:r + 1, :] + vv * k_t[s, r:r + 1, :]
                s_ref[g, d] = sn
                acc[d % n_acc] = acc[d % n_acc] +sn * q_t[s, r:r + 1, :]
            ys_ref[s, g] = sum(acc[1:], acc[0])
        return carry

    lax.fori_loop(0, mode.tb, step, 0)

    def epilogue(s, carry):
        _scan_scatter(mode, src_ref, ys_ref, of_ref, ob_ref, s)
        return carry

    lax.fori_loop(0, mode.tb, epilogue, 0)


def _rwkv_scan_kernel(*refs, mode, has_s0, t_axis):
    ins = refs[:12]
    refs = refs[12:]
    s0_ref = None
    if has_s0:
        s0_ref, refs = refs[0], refs[1:]
    yf_ref, yb_ref, s_ref, src_ref, r_t, w_t, k_t, v_t, kk_t, a_t, ys_ref = refs
    vsub = HEAD // mode.rep
    vc = min(vsub, RWKV_CHUNK)
    n_acc = _n_acc(vc)
    _scan_init(mode, t_axis, s_ref, s0_ref, src_ref)

    def prologue(s, carry):
        for i, x_t in enumerate((r_t, w_t, k_t, v_t, kk_t, a_t)):
            x_t[s] = _scan_gather(mode, src_ref, ins[2 * i], ins[2 * i + 1], s)
        return carry

    lax.fori_loop(0, mode.tb, prologue, 0)

    def step(s, carry):
        for g in range(mode.groups):
            r0 = g * HEAD
            vv_all = _scan_value_slab(mode, v_t[s, r0:r0 + HEAD, :])
            for c in range(vsub // vc):
                rows = slice(c * vc, (c + 1) * vc)

                def tile(x_t, i):
                    return x_t[s, pl.ds(pl.multiple_of(r0 + i * 8, 8), 8), :]

                def pass1(i, acc):
                    kk8 = tile(kk_t, i)
                    for j in range(8):
                        acc = acc + s_ref[g, i * 8 + j, rows, :] * kk8[j:j + 1, :]
                    return acc

                sa = -lax.fori_loop(0, HEAD // 8, pass1, jnp.zeros((vc, LANES), F32))
                vv = vv_all[rows]

                def pass2(i, acc):
                    kk8, a8, w8, k8, r8 = (tile(x_t, i) for x_t in (kk_t, a_t, w_t, k_t, r_t))
                    b8 = kk8 * a8
                    for j in range(8):
                        d = i * 8 + j
                        sn = s_ref[g, d, rows, :] * w8[j:j + 1, :] + sa * b8[j:j + 1, :] + vv * k8[j:j + 1, :]
                        s_ref[g, d, rows, :] = sn
                        acc = acc + sn * r8[j:j + 1, :]
                    return acc

                ys_ref[s, g, rows, :] = lax.fori_loop(0, HEAD // 8, pass2, jnp.zeros((vc, LANES), F32))
        return carry

    lax.fori_loop(0, mode.tb, step, 0)

    def epilogue(s, carry):
        _scan_scatter(mode, src_ref, ys_ref, yf_ref, yb_ref, s)
        return carry

    lax.fori_loop(0, mode.tb, epilogue, 0)


def _scan_scratch(mode, n_streams):
    rows = mode.groups * HEAD
    vsub = HEAD // mode.rep
    return ([pltpu.VMEM((LANES, LANES), F32)]
            + [pltpu.VMEM((mode.tb, rows, LANES), F32) for _ in range(n_streams)]
            + [pltpu.VMEM((mode.tb, mode.groups, vsub, LANES), F32)])


def _hgrn_scans(slabs, lb, state0):
    mode = _ScanMode(prompt=True, groups=2, rep=1, r2=0, tb=16)
    n_t = SEQ // mode.tb
    rows = mode.tb * BATCH
    width = PAIRS * LANES
    assert width == H_A * HEAD

    def spec(p, rev):
        if rev:
            return pl.BlockSpec((None, rows, width), lambda t: (p, n_t - 1 - t, 0))
        return pl.BlockSpec((None, rows, width), lambda t: (p, t, 0))

    def ospec(rev):
        if rev:
            return pl.BlockSpec((rows, width), lambda t: (n_t - 1 - t, 0))
        return pl.BlockSpec((rows, width), lambda t: (t, 0))

    lb4 = lb.reshape(2, PAIRS, 2, HEAD)
    lb_p = jnp.broadcast_to(jnp.transpose(lb4, (2, 3, 0, 1))[..., None], (2, HEAD, 2, PAIRS, BATCH))
    lb_p = lb_p.reshape(2 * HEAD, LANES)
    st_spec = pl.BlockSpec((2, HEAD, HEAD, LANES), lambda t: (0, 0, 0, 0))
    o_shape = jax.ShapeDtypeStruct((N_PROMPT, width), F32)
    view = slabs
    o_fp, o_bp, s_fin = pl.pallas_call(
        functools.partial(_hgrn_scan_kernel, mode=mode, has_s0=False, t_axis=0),
        out_shape=(o_shape, o_shape, jax.ShapeDtypeStruct((2, HEAD, HEAD, LANES), F32)),
        grid=(n_t,),
        in_specs=[spec(0, False), spec(0, True), spec(1, False), spec(2, True), spec(3, False), spec(3, True),
                  pl.BlockSpec((2 * HEAD, LANES), lambda t: (0, 0))],
        out_specs=(ospec(False), ospec(True), st_spec),
        scratch_shapes=_scan_scratch(mode, 4),
        compiler_params=_cparams(("arbitrary",)),
        name="hgrn_scan_prompt",
    )(view, view, view, view, view, view, lb_p)
    s_fin = jnp.transpose(s_fin.reshape(2, HEAD, HEAD, 2, PAIRS, BATCH), (5, 3, 4, 0, 1, 2))
    s_fin = s_fin.reshape(BATCH, 2, H_A, DK_A, DV_A)

    mode = _ScanMode(prompt=False, groups=1, rep=4, r2=H_A, tb=32)
    n_t = DEC_SEQ // mode.tb
    view = slabs[:, N_PROMPT:].reshape(5, DEC_BATCH, DEC_SEQ, H_A, HEAD)

    def spec(p, rev):
        if rev:
            return pl.BlockSpec((None, DEC_BATCH, mode.tb, H_A, HEAD), lambda t: (p, 0, n_t - 1 - t, 0, 0))
        return pl.BlockSpec((None, DEC_BATCH, mode.tb, H_A, HEAD), lambda t: (p, 0, t, 0, 0))

    def ospec(rev):
        if rev:
            return pl.BlockSpec((DEC_BATCH, mode.tb, H_A, HEAD), lambda t: (0, n_t - 1 - t, 0, 0))
        return pl.BlockSpec((DEC_BATCH, mode.tb, H_A, HEAD), lambda t: (0, t, 0, 0))

    vsub = HEAD // mode.rep
    lb_s = jnp.transpose(lb.reshape(2, H_A, HEAD), (2, 0, 1))
    lb_s = jnp.broadcast_to(lb_s[:, None, :, None, :], (HEAD, mode.rep, 2, DEC_BATCH, H_A)).reshape(HEAD, LANES)
    s0 = state0.reshape(DEC_BATCH, 2, H_A, DK_A, mode.rep, vsub)
    s0 = jnp.transpose(s0, (3, 5, 4, 1, 0, 2)).reshape(1, HEAD, vsub, LANES)
    st_spec = pl.BlockSpec((1, HEAD, vsub, LANES), lambda t: (0, 0, 0, 0))
    o_shape = jax.ShapeDtypeStruct((DEC_BATCH, DEC_SEQ, H_A, HEAD), F32)
    o_fl, o_bl, _ = pl.pallas_call(
        functools.partial(_hgrn_scan_kernel, mode=mode, has_s0=True, t_axis=0),
        out_shape=(o_shape, o_shape, jax.ShapeDtypeStruct((1, HEAD, vsub, LANES), F32)),
        grid=(n_t,),
        in_specs=[spec(0, False), spec(0, True), spec(1, False), spec(2, True), spec(3, False), spec(3, True),
                  pl.BlockSpec((HEAD, LANES), lambda t: (0, 0)), st_spec],
        out_specs=(ospec(False), ospec(True), st_spec),
        scratch_shapes=_scan_scratch(mode, 4),
        compiler_params=_cparams(("arbitrary",)),
        name="hgrn_scan_latent",
    )(view, view, view, view, view, view, lb_s, s0)
    return o_fp, o_bp, o_fl.reshape(N_SAMPLE, width), o_bl.reshape(N_SAMPLE, width), s_fin


def _rwkv_scans(r, w_f, w_b, k, v, kk, a, state0):
    streams = ((r, r), (w_f, w_b), (k, k), (v, v), (kk, kk), (a, a))
    mode = _ScanMode(prompt=True, groups=2, rep=1, r2=0, tb=16)
    n_t = SEQ // mode.tb
    rows = mode.tb * BATCH
    width = PAIRS * LANES
    n_half = D_MODEL // width

    fwd = pl.BlockSpec((rows, width), lambda h, t: (t, h))
    bwd = pl.BlockSpec((rows, width), lambda h, t: (n_t - 1 - t, h))
    st_spec = pl.BlockSpec((None, 2, HEAD, HEAD, LANES), lambda h, t: (h, 0, 0, 0, 0))
    y_shape = jax.ShapeDtypeStruct((N_PROMPT, D_MODEL), F32)
    args = []
    for x_f, x_b in streams:
        args += [x_f, x_b]
    y_fp, y_bp, s_fin = pl.pallas_call(
        functools.partial(_rwkv_scan_kernel, mode=mode, has_s0=False, t_axis=1),
        out_shape=(y_shape, y_shape, jax.ShapeDtypeStruct((n_half, 2, HEAD, HEAD, LANES), F32)),
        grid=(n_half, n_t),
        in_specs=[fwd, bwd] * 6,
        out_specs=(fwd, bwd, st_spec),
        scratch_shapes=_scan_scratch(mode, 6),
        compiler_params=_cparams(("parallel", "arbitrary")),
        name="rwkv_scan_prompt",
    )(*args)
    s_fin = s_fin.reshape(n_half, 2, HEAD, HEAD, 2, PAIRS, BATCH)
    s_fin = jnp.transpose(s_fin, (6, 4, 0, 5, 1, 3, 2)).reshape(BATCH, 2, H_C, N_C, N_C)

    mode = _ScanMode(prompt=False, groups=1, rep=2, r2=H_C, tb=32)
    n_t = DEC_SEQ // mode.tb

    def view(x):
        return x[N_PROMPT:].reshape(DEC_BATCH, DEC_SEQ, H_C, HEAD)

    fwd = pl.BlockSpec((DEC_BATCH, mode.tb, H_C, HEAD), lambda t: (0, t, 0, 0))
    bwd = pl.BlockSpec((DEC_BATCH, mode.tb, H_C, HEAD), lambda t: (0, n_t - 1 - t, 0, 0))
    vsub = HEAD // mode.rep
    s0 = state0.reshape(DEC_BATCH, 2, H_C, mode.rep, vsub, N_C)
    s0 = jnp.transpose(s0, (5, 4, 3, 1, 0, 2)).reshape(1, HEAD, vsub, LANES)
    st_spec = pl.BlockSpec((1, HEAD, vsub, LANES), lambda t: (0, 0, 0, 0))
    y_shape = jax.ShapeDtypeStruct((DEC_BATCH, DEC_SEQ, H_C, HEAD), F32)
    args = []
    for x_f, x_b in streams:
        x_fv = view(x_f)
        args += [x_fv, x_fv if x_b is x_f else view(x_b)]
    y_fl, y_bl, _ = pl.pallas_call(
        functools.partial(_rwkv_scan_kernel, mode=mode, has_s0=True, t_axis=0),
        out_shape=(y_shape, y_shape, jax.ShapeDtypeStruct((1, HEAD, vsub, LANES), F32)),
        grid=(n_t,),
        in_specs=[fwd, bwd] * 6 + [st_spec],
        out_specs=(fwd, bwd, st_spec),
        scratch_shapes=_scan_scratch(mode, 6),
        compiler_params=_cparams(("arbitrary",)),
        name="rwkv_scan_latent",
    )(*args, s0)
    return y_fp, y_bp, y_fl.reshape(N_SAMPLE, D_MODEL), y_bl.reshape(N_SAMPLE, D_MODEL), s_fin


def _prompt_spec(width):
    return pl.BlockSpec((ROW_TILE, width), lambda i: (jnp.minimum(i, BATCH - 1), 0))


def _latent_spec(width):
    return pl.BlockSpec((ROW_TILE, width), lambda i: (jnp.maximum(i - BATCH, 0), 0))


def _hgrn_post_kernel(ofp_ref, obp_ref, ofl_ref, obl_ref, g_ref, gain_ref, bo_ref, o_ref):
    is_prompt = pl.program_id(0) < BATCH
    o = jnp.where(is_prompt, ofp_ref[...] + obp_ref[...], ofl_ref[...] + obl_ref[...])
    ms = _head_sum(o * o, bo_ref[...]) * (1.0 / DV_A)
    y = o * lax.rsqrt(ms + EPS) * gain_ref[...]
    g = g_ref[...]
    o_ref[...] = (y * (g * jax.nn.sigmoid(g))).astype(o_ref.dtype)


def _hgrn_post(o_fp, o_bp, o_fl, o_bl, slabs, gain):
    w = H_A * DV_A
    return pl.pallas_call(
        _hgrn_post_kernel,
        out_shape=jax.ShapeDtypeStruct((N_TOK, w), BF16),
        grid=(N_TILES,),
        in_specs=[_prompt_spec(w), _prompt_spec(w), _latent_spec(w), _latent_spec(w),
                  pl.BlockSpec((None, ROW_TILE, w), lambda i: (4, i, 0)),
                  _vec_spec(w), pl.BlockSpec((LANES, LANES), lambda i: (0, 0))],
        out_specs=_row_spec(w),
        compiler_params=_cparams(("parallel",)),
        name="hgrn_post",
    )(o_fp, o_bp, o_fl, o_bl, slabs, jnp.tile(gain, H_A).reshape(1, w), _block_ones())


def _q_up_kernel(cq_ref, gain_ref, w_ref, cos_ref, sin_ref, qn_ref, qr_ref):
    cq = _rms(cq_ref[...], gain_ref[...]).astype(BF16)
    q = jnp.dot(cq, w_ref[...], preferred_element_type=F32)
    qn_ref[...] = q[:, :512].astype(BF16)
    qr_ref[...] = (q[:, 512:768] * cos_ref[...] + q[:, 768:1024] * sin_ref[...]).astype(BF16)


def _q_up(tail, gain, w, cos, sin):
    return pl.pallas_call(
        _q_up_kernel,
        out_shape=(jax.ShapeDtypeStruct((N_TOK, 512), BF16), jax.ShapeDtypeStruct((N_TOK, 256), BF16)),
        grid=(N_TILES,),
        in_specs=[_row_spec(Q_RANK, T_CQ // Q_RANK),
                  _vec_spec(Q_RANK), pl.BlockSpec((Q_RANK, 1024), lambda i: (0, 0)),
                  _row_spec(256), _row_spec(256)],
        out_specs=(_row_spec(512), _row_spec(256)),
        compiler_params=_cparams(("parallel",)),
        name="q_up",
    )(tail, gain.reshape(1, Q_RANK), w, cos, sin)


def _kv_up_kernel(ckv_ref, kr_ref, krot_ref, gain_ref, w_ref, cos_ref, sin_ref, ckvn_ref, kn_ref, v_ref, krope_ref):
    ckv = _rms(ckv_ref[...], gain_ref[...])
    ckvn_ref[...] = ckv
    kv = jnp.dot(ckv.astype(BF16), w_ref[...], preferred_element_type=F32)
    kn_ref[...] = kv[:, :512].astype(BF16)
    v_ref[...] = kv[:, 512:].astype(BF16)
    krope_ref[...] = (kr_ref[...] * cos_ref[...] + krot_ref[...] * sin_ref[...]).astype(BF16)


def _kv_up(tail, gain, w, cos4, sin4):
    return pl.pallas_call(
        _kv_up_kernel,
        out_shape=(jax.ShapeDtypeStruct((N_TOK, KV_RANK), F32), jax.ShapeDtypeStruct((N_TOK, 512), BF16),
                   jax.ShapeDtypeStruct((N_TOK, 512), BF16), jax.ShapeDtypeStruct((N_TOK, LANES), BF16)),
        grid=(N_TILES,),
        in_specs=[_row_spec(LANES, T_CKV // LANES), _row_spec(LANES, T_KR // LANES), _row_spec(LANES, T_KROT // LANES),
                  _vec_spec(KV_RANK), pl.BlockSpec((KV_RANK, 1024), lambda i: (0, 0)),
                  _row_spec(LANES), _row_spec(LANES)],
        out_specs=(_row_spec(KV_RANK), _row_spec(512), _row_spec(512), _row_spec(LANES)),
        compiler_params=_cparams(("parallel",)),
        name="kv_up",
    )(tail, tail, tail, gain.reshape(1, KV_RANK), w, cos4, sin4)


def _attn_kernel(qn_ref, qr_ref, kn_ref, kr_ref, v_ref, o_ref, *, scale):
    hp = pl.program_id(1)
    qn = qn_ref[0]
    qr = qr_ref[0]
    kcat = jnp.concatenate([kn_ref[0], kr_ref[0]], axis=-1)
    v = v_ref[0]
    lane = lax.broadcasted_iota(jnp.int32, (1, LANES), 1)
    zero = jnp.zeros((), BF16)
    outs = []
    for j in range(2):
        mn = (lane // D_NOPE) == j
        mr = (lane // D_ROPE) == (2 * hp + j) % 4
        qcat = jnp.concatenate([jnp.where(mn, qn, zero), jnp.where(mr, qr, zero)], axis=-1)
        s = lax.dot_general(qcat, kcat, (((1,), (1,)), ((), ())), preferred_element_type=F32) * scale
        p = jnp.exp(s - jnp.max(s, axis=-1, keepdims=True))
        l = jnp.sum(p, axis=-1, keepdims=True)
        outs.append(jnp.dot(p.astype(BF16), v, preferred_element_type=F32) / l)
    o_ref[0] = jnp.where(lane < DV_B, outs[0], outs[1]).astype(o_ref.dtype)


def _attention(qn, qr, kn, kr, v, bsz, time_major, tq=256):
    t_len, tk = qn.shape[1], kn.shape[1]

    def spec(is_query, per_seq, col):
        rows = tq if is_query else tk
        if time_major:
            return pl.BlockSpec((1, rows, LANES), lambda b, h, q: (0, q if is_query else 0, b * per_seq + col(h)))
        return pl.BlockSpec((1, rows, LANES), lambda b, h, q: (b, q if is_query else 0, col(h)))

    return pl.pallas_call(
        functools.partial(_attn_kernel, scale=float((D_NOPE + D_ROPE) ** -0.5)),
        out_shape=jax.ShapeDtypeStruct(qn.shape, BF16),
        grid=(bsz, 4, t_len // tq),
        in_specs=[spec(True, 4, lambda h: h), spec(True, 2, lambda h: h // 2),
                  spec(False, 4, lambda h: h), spec(False, 1, lambda h: 0), spec(False, 4, lambda h: h)],
        out_specs=spec(True, 4, lambda h: h),
        compiler_params=_cparams(("parallel", "parallel", "parallel")),
        name="attention",
    )(qn, qr, kn, kr, v)


def _odd_pre_kernel(x_ref, xp_ref, xn_ref, g_ref, sc_ref, sh_ref, mu_ref, *o_refs):
    i = pl.program_id(0)
    g, sc, sh = g_ref[...], sc_ref[0], sh_ref[0]

    def nm(x):
        return _rms(x, g) * (1.0 + sc) + sh

    h = nm(x_ref[...])
    hp = nm(xp_ref[...])
    hn = nm(xn_ref[...])
    is_prompt = i < BATCH
    hp_p = jnp.where(i > 0, hp, 0.0)
    hn_p = jnp.where(i < BATCH - 1, hn, 0.0)
    prev_p = jnp.concatenate([hp_p, h[:ROW_TILE - BATCH]], axis=0)
    next_p = jnp.concatenate([h[BATCH:], hn_p], axis=0)
    part = (i - BATCH) % 4
    hp_s = jnp.where(part != 0, hp[HALO - 1:HALO], 0.0)
    hn_s = jnp.where(part != 3, hn[0:1], 0.0)
    row = lax.broadcasted_iota(jnp.int32, (ROW_TILE, 1), 0)
    prev_s = jnp.where(row == 0, hp_s, pltpu.roll(h, 1, axis=0))
    next_s = jnp.where(row == ROW_TILE - 1, hn_s, pltpu.roll(h, ROW_TILE - 1, axis=0))
    prev = jnp.where(is_prompt, prev_p, prev_s)
    nxt = jnp.where(is_prompt, next_p, next_s)
    xx = 0.5 * (prev + nxt) - h
    for j, o_ref in enumerate(o_refs):
        o_ref[...] = (h + xx * mu_ref[j:j + 1, :]).astype(o_ref.dtype)


def _odd_pre(x, g, sc, sh, mu):
    assert HALO == BATCH
    per = ROW_TILE // HALO
    last = N_TOK // HALO - 1
    return pl.pallas_call(
        _odd_pre_kernel,
        out_shape=tuple(jax.ShapeDtypeStruct((N_TOK, D_MODEL), BF16) for _ in range(6)),
        grid=(N_TILES,),
        in_specs=[_row_spec(),
                  pl.BlockSpec((HALO, D_MODEL), lambda i: (jnp.maximum(i * per - 1, 0), 0)),
                  pl.BlockSpec((HALO, D_MODEL), lambda i: (jnp.minimum((i + 1) * per, last), 0)),
                  _vec_spec(), _tile_spec(), _tile_spec(),
                  pl.BlockSpec((8, D_MODEL), lambda i: (0, 0))],
        out_specs=tuple(_row_spec() for _ in range(6)),
        compiler_params=_cparams(("parallel",)),
        name="odd_pre",
    )(x, x, x, g.reshape(1, D_MODEL), sc, sh, jnp.pad(mu, ((0, 2), (0, 0))))


def _rwkv_prep_kernel(k_ref, a_ref, kk_w_ref, ka_w_ref, bo_ref, kk_ref, k2_ref):
    k = k_ref[...]
    kk = k * kk_w_ref[...]
    kk_ref[...] = kk * lax.rsqrt(_head_sum(kk * kk, bo_ref[...]) + 1e-12)
    k2_ref[...] = k * (1.0 + (a_ref[...] - 1.0) * ka_w_ref[...])


def _rwkv_prep(k, a, k_k, k_a):
    return pl.pallas_call(
        _rwkv_prep_kernel,
        out_shape=(jax.ShapeDtypeStruct((N_TOK, D_MODEL), F32), jax.ShapeDtypeStruct((N_TOK, D_MODEL), F32)),
        grid=(N_TILES,),
        in_specs=[_row_spec(), _row_spec(), _vec_spec(), _vec_spec(),
                  pl.BlockSpec((LANES, LANES), lambda i: (0, 0))],
        out_specs=(_row_spec(), _row_spec()),
        compiler_params=_cparams(("parallel",)),
        name="rwkv_prep",
    )(k, a, k_k.reshape(1, D_MODEL), k_a.reshape(1, D_MODEL), _block_ones())


def _rwkv_post_kernel(yfp_ref, ybp_ref, yfl_ref, ybl_ref, r_ref, k_ref, v_ref, g_ref, rk_ref, lnw_ref, lnb_ref,
                      bo_ref, o_ref):
    bo = bo_ref[...]
    y = jnp.where(pl.program_id(0) < BATCH, yfp_ref[...] + ybp_ref[...], yfl_ref[...] + ybl_ref[...])
    d = y - _head_sum(y, bo) * (1.0 / N_C)
    var = _head_sum(d * d, bo) * (1.0 / N_C)
    yn = d * lax.rsqrt(var + GN_EPS) * lnw_ref[...] + lnb_ref[...]
    bonus = _head_sum(r_ref[...] * k_ref[...] * rk_ref[...], bo) * v_ref[...]
    o_ref[...] = ((yn + bonus) * g_ref[...]).astype(o_ref.dtype)


def _rwkv_post(y_fp, y_bp, y_fl, y_bl, r, k2, v, g, r_k, ln_w, ln_b):
    return pl.pallas_call(
        _rwkv_post_kernel,
        out_shape=jax.ShapeDtypeStruct((N_TOK, D_MODEL), BF16),
        grid=(N_TILES,),
        in_specs=([_prompt_spec(D_MODEL)] * 2 + [_latent_spec(D_MODEL)] * 2 + [_row_spec()] * 4 + [_vec_spec()] * 3
                  + [pl.BlockSpec((LANES, LANES), lambda i: (0, 0))]),
        out_specs=_row_spec(),
        compiler_params=_cparams(("parallel",)),
        name="rwkv_post",
    )(y_fp, y_bp, y_fl, y_bl, r, k2, v, g, r_k.reshape(1, D_MODEL), ln_w.reshape(1, D_MODEL),
      ln_b.reshape(1, D_MODEL), _block_ones())


def _rot_cols(w):
    wb = w.reshape(w.shape[:-1] + (2, 2, D_ROPE // 4))
    return jnp.concatenate([-wb[..., 1:, :], wb[..., :1, :]], axis=-2).reshape(w.shape)


def _rope_tables():
    rows = DEC_SEQ // GRID_W
    row, col = np.meshgrid(np.arange(rows), np.arange(GRID_W), indexing='ij')
    row = jnp.asarray(row.reshape(-1), F32)
    col = jnp.asarray(col.reshape(-1), F32)
    n_freq = D_ROPE // 4
    inv = 1.0 / (ROPE_BASE ** (jnp.arange(n_freq, dtype=F32) / n_freq))
    ar = row[:, None] * inv
    ac = col[:, None] * inv
    ang = jnp.concatenate([ar, ar, ac, ac], axis=-1)
    cos, sin = jnp.cos(ang), jnp.sin(ang)
    cos = jnp.concatenate([jnp.ones((N_PROMPT, D_ROPE), F32), jnp.tile(cos, (DEC_BATCH, 1))], axis=0)
    sin = jnp.concatenate([jnp.zeros((N_PROMPT, D_ROPE), F32), jnp.tile(sin, (DEC_BATCH, 1))], axis=0)
    return cos, sin


def _even_mixer(j, h, lb_all, cos, sin, cache_ckv, cache_krope, state_hgrn,
                ev_w_in, hgrn_norm, mla_q_norm, mla_w_q_up, mla_kv_norm, mla_w_kv_up):
    w_in = ev_w_in[j]
    n_a = 5 * H_A * DK_A
    w_kr = w_in[:, n_a + Q_RANK + KV_RANK:]
    w_tail = jnp.concatenate([w_in[:, n_a:n_a + Q_RANK + KV_RANK], jnp.tile(w_kr, (1, 4)),
                              jnp.tile(_rot_cols(w_kr), (1, 4))], axis=1).astype(BF16)
    slabs = _mm_split(h, w_in[:, :n_a].astype(BF16), H_A * DK_A, name="hgrn_proj")
    tail = _mm(h, w_tail, F32, tm=512, tn=T_N, name="mla_proj")

    o_fp, o_bp, o_fl, o_bl, s_h = _hgrn_scans(slabs, lb_all[j], state_hgrn[:, j])
    o_a = _hgrn_post(o_fp, o_bp, o_fl, o_bl, slabs, hgrn_norm[j])

    wq = mla_w_q_up[j].reshape(Q_RANK, H_B, D_NOPE + D_ROPE)
    wq_n = wq[:, :, :D_NOPE].reshape(Q_RANK, H_B * D_NOPE)
    wq_r = wq[:, :, D_NOPE:]
    wq_aug = jnp.concatenate([wq_n, wq_r.reshape(Q_RANK, -1), _rot_cols(wq_r).reshape(Q_RANK, -1)], axis=1).astype(BF16)
    qn, qr = _q_up(tail, mla_q_norm[j], wq_aug, jnp.tile(cos, (1, H_B)), jnp.tile(sin, (1, H_B)))
    wkv = mla_w_kv_up[j].reshape(KV_RANK, H_B, D_NOPE + DV_B)
    wkv_aug = jnp.concatenate([wkv[:, :, :D_NOPE].reshape(KV_RANK, -1), wkv[:, :, D_NOPE:].reshape(KV_RANK, -1)],
                              axis=1).astype(BF16)
    ckv_n, kn, vv, kr = _kv_up(tail, mla_kv_norm[j], wkv_aug, jnp.tile(cos, (1, 4)), jnp.tile(sin, (1, 4)))
    kv_ctx = _mm(cache_ckv[:, j].reshape(DEC_BATCH * PAST_LEN, KV_RANK), wkv_aug, BF16, name="kv_ctx")
    kn_c = kv_ctx[:, :512].reshape(DEC_BATCH, PAST_LEN, 512)
    v_c = kv_ctx[:, 512:].reshape(DEC_BATCH, PAST_LEN, 512)
    kr_c = jnp.tile(cache_krope[:, j], (1, 1, 4)).astype(BF16)

    def prompt(x):
        return x[:N_PROMPT].reshape(1, SEQ, BATCH * x.shape[-1])

    def latent(x):
        return x[N_PROMPT:].reshape(DEC_BATCH, DEC_SEQ, x.shape[-1])

    o_p = _attention(prompt(qn), prompt(qr), prompt(kn), prompt(kr), prompt(vv), BATCH, True)
    o_s = _attention(latent(qn), latent(qr),
                     jnp.concatenate([kn_c, latent(kn)], axis=1),
                     jnp.concatenate([kr_c, latent(kr)], axis=1),
                     jnp.concatenate([v_c, latent(vv)], axis=1), DEC_BATCH, False)
    o_att = jnp.concatenate([o_p.reshape(N_PROMPT, 512), o_s.reshape(N_SAMPLE, 512)], axis=0)

    new_ckv = jnp.transpose(ckv_n[:N_PROMPT].reshape(SEQ, BATCH, KV_RANK), (1, 0, 2))
    new_krope = jnp.transpose(tail[:N_PROMPT, T_KR:T_KR + D_ROPE].reshape(SEQ, BATCH, D_ROPE), (1, 0, 2))
    return o_a, o_att, (s_h, new_ckv, new_krope)


def _odd_mixer(j, x, g0, sc, sh, state_rwkv, rw_mu, rw_w_r, rw_w_k, rw_w_v, rw_w0, rw_w1, rw_w2, rw_a0, rw_a1, rw_a2,
               rw_g1, rw_g2, rw_k_k, rw_k_a, rw_r_k, rw_ln_w, rw_ln_b):
    xr, xw, xk, xv, xa, xg = _odd_pre(x, g0, sc, sh, rw_mu[j])
    r = _mm(xr, rw_w_r[j].astype(BF16), name="rwkv_r")
    k = _mm(xk, rw_w_k[j].astype(BF16), name="rwkv_k")
    v = _mm(xv, rw_w_v[j].astype(BF16), name="rwkv_v")
    a = _lora(xa, rw_a1[j], rw_a2[j], rw_a0[j], _identity, jax.nn.sigmoid)
    g = _lora(xg, rw_g1[j], rw_g2[j], jnp.zeros((D_MODEL,), F32), jax.nn.sigmoid, _identity)
    w_f = _lora(xw, rw_w1[j, 0], rw_w2[j, 0], rw_w0[j, 0], jnp.tanh, _decay)
    w_b = _lora(xw, rw_w1[j, 1], rw_w2[j, 1], rw_w0[j, 1], jnp.tanh, _decay)
    kk, k2 = _rwkv_prep(k, a, rw_k_k[j], rw_k_a[j])
    y_fp, y_bp, y_fl, y_bl, s_r = _rwkv_scans(r, w_f, w_b, k2, v, kk, a, state_rwkv[:, j])
    mix = _rwkv_post(y_fp, y_bp, y_fl, y_bl, r, k2, v, g, rw_r_k[j], rw_ln_w[j], rw_ln_b[j])
    return mix, s_r


def kernel(x_prompt, x_sample, cache_ckv, cache_krope, state_hgrn, state_rwkv, c, c_ctx, ada_w, ada_b, norm_gains, ev_w_in, hgrn_lb_logits, hgrn_norm, mla_q_norm, mla_w_q_up, mla_kv_norm, mla_w_kv_up, ev_w_out, rw_mu, rw_w_r, rw_w_k, rw_w_v, rw_w_o, rw_w0, rw_w1, rw_w2, rw_a0, rw_a1, rw_a2, rw_g1, rw_g2, rw_k_k, rw_k_a, rw_r_k, rw_ln_w, rw_ln_b, ffn_w_gate, ffn_w_up, ffn_w_down, moe_router, moe_w_gate, moe_w_up, moe_w_down):
    lb_all = jnp.cumsum(jax.nn.softmax(hgrn_lb_logits.astype(F32), axis=0), axis=0)
    lb_all = lb_all - lb_all[:1]
    cos, sin = _rope_tables()

    cond8 = jnp.zeros((8, D_MODEL), F32).at[0].set(c_ctx).at[1:1 + DEC_BATCH].set(c)
    mod = _modulation(cond8, ada_w, ada_b)
    tile_row = np.array([0] * BATCH + [1 + t // 4 for t in range(N_TILES - BATCH)])
    modt = mod[:, tile_row, :].reshape(DEPTH, N_TILES, 6, 1, D_MODEL)

    x = jnp.concatenate([jnp.transpose(x_prompt, (1, 0, 2)).reshape(N_PROMPT, D_MODEL),
                         x_sample.reshape(N_SAMPLE, D_MODEL)], axis=0)
    ckv_l, krope_l, hgrn_l, rwkv_l = [], [], [], []
    for l in range(DEPTH):
        j = l // 2
        sh_m, sc_m, gt_m, sh_f, sc_f, gt_f = [modt[l, :, i] for i in range(6)]
        if l % 2 == 0:
            h = _norm_mod(x, norm_gains[l, 0], sc_m, sh_m)
            o_a, o_att, (s_h, n_ckv, n_kr) = _even_mixer(j, h, lb_all, cos, sin, cache_ckv, cache_krope, state_hgrn,
                                                        ev_w_in, hgrn_norm, mla_q_norm, mla_w_q_up, mla_kv_norm,
                                                        mla_w_kv_up)
            hgrn_l.append(s_h)
            ckv_l.append(n_ckv)
            krope_l.append(n_kr)
            x = _mm2_resid(o_a, o_att, ev_w_out[j].astype(BF16), x, norm_gains[l, 1], gt_m)
            h = _norm_mod(x, norm_gains[l, 2], sc_f, sh_f)
            hid = _swiglu(h, ffn_w_gate[j].astype(BF16), ffn_w_up[j].astype(BF16))
            x = _mm_resid(hid, ffn_w_down[j].astype(BF16), x, norm_gains[l, 3], gt_f)
        else:
            mix, s_r = _odd_mixer(j, x, norm_gains[l, 0], sc_m, sh_m, state_rwkv, rw_mu, rw_w_r, rw_w_k, rw_w_v,
                                  rw_w0, rw_w1, rw_w2, rw_a0, rw_a1, rw_a2, rw_g1, rw_g2, rw_k_k, rw_k_a, rw_r_k,
                                  rw_ln_w, rw_ln_b)
            rwkv_l.append(s_r)
            x = _mm_resid(mix, rw_w_o[j].astype(BF16), x, norm_gains[l, 1], gt_m)
            h_packed, route = _norm_mod_router(x, norm_gains[l, 2], sc_f, sh_f, moe_router[j])
            dest, src_tok, row_w, tile_e, n_used = _moe_plan(route)
            h_sorted = _row_gather(src_tok, h_packed, MOE_ROWS, MOE_TM)
            hid = _moe_swiglu(tile_e, n_used, h_sorted, moe_w_gate[j], moe_w_up[j], row_w)
            y_sorted = _moe_down(tile_e, n_used, hid, moe_w_down[j])
            x = _moe_combine(dest, y_sorted, x, norm_gains[l, 3], gt_f)

    y_prompt = jnp.transpose(x[:N_PROMPT].reshape(SEQ, BATCH, D_MODEL), (1, 0, 2))
    y_sample = x[N_PROMPT:].reshape(DEC_BATCH, DEC_SEQ, D_MODEL)
    return (y_prompt, y_sample, jnp.stack(ckv_l, axis=1), jnp.stack(krope_l, axis=1),
            jnp.stack(hgrn_l, axis=1), jnp.stack(rwkv_l, axis=1))
```

```python
import functools
from typing import NamedTuple

import numpy as np
import jax
import jax.numpy as jnp
from jax import lax
from jax.experimental import pallas as pl
from jax.experimental.pallas import tpu as pltpu

D_MODEL = 1024
BATCH = 16
SEQ = 256
DEPTH = 4
DEC_BATCH = 2
DEC_SEQ = 1024
PAST_LEN = 256
GRID_W = 64
H_A = 8
DK_A = 64
DV_A = 64
H_B = 8
Q_RANK = 256
KV_RANK = 128
D_NOPE = 64
D_ROPE = 32
DV_B = 64
ROPE_BASE = 10000.0
H_C = 16
N_C = 64
D_FF = 2816
N_EXPERTS = 8
EPS = 1e-6
GN_EPS = 64e-5

F32 = jnp.float32
BF16 = jnp.bfloat16
HIGHEST = lax.Precision.HIGHEST

N_PROMPT = BATCH * SEQ
N_SAMPLE = DEC_BATCH * DEC_SEQ
N_TOK = N_PROMPT + N_SAMPLE
ROW_TILE = 256
N_TILES = N_TOK // ROW_TILE
LANES = 128
HEAD = 64
VMEM_LIMIT = 48 * 1024 * 1024
HALO = 16
PAIRS = 4
FF_TN = 1408
MOE_TM = 256
MOE_ROWS = 2 * N_TOK + N_EXPERTS * MOE_TM
MOE_TILES = MOE_ROWS // MOE_TM
RWKV_CHUNK = 64

T_CQ, T_CKV, T_KR, T_KROT, T_N = 0, 256, 384, 512, 640


def _cparams(sem):
    return pltpu.CompilerParams(dimension_semantics=sem, vmem_limit_bytes=VMEM_LIMIT)


def _mod_kernel(c_ref, w_ref, b_ref, o_ref):
    c = c_ref[...]
    s = (c * jax.nn.sigmoid(c)).astype(BF16)
    o_ref[0] = jnp.dot(s, w_ref[0].astype(BF16), preferred_element_type=F32) + b_ref[0]


def _modulation(cond8, ada_w, ada_b):
    tn = 768
    return pl.pallas_call(
        _mod_kernel,
        out_shape=jax.ShapeDtypeStruct((DEPTH, 8, 6 * D_MODEL), F32),
        grid=(DEPTH, 6 * D_MODEL // tn),
        in_specs=[pl.BlockSpec((8, D_MODEL), lambda l, n: (0, 0)),
                  pl.BlockSpec((1, D_MODEL, tn), lambda l, n: (l, 0, n)),
                  pl.BlockSpec((1, 1, tn), lambda l, n: (l, 0, n))],
        out_specs=pl.BlockSpec((1, 8, tn), lambda l, n: (l, 0, n)),
        compiler_params=_cparams(("parallel", "parallel")),
        name="modulation",
    )(cond8, ada_w, ada_b.reshape(DEPTH, 1, 6 * D_MODEL))


def _rms(x, g):
    return x * lax.rsqrt(jnp.mean(x * x, axis=-1, keepdims=True) + EPS) * g


def _norm_mod_kernel(x_ref, g_ref, sc_ref, sh_ref, o_ref):
    h = _rms(x_ref[...], g_ref[...]) * (1.0 + sc_ref[0]) + sh_ref[0]
    o_ref[...] = h.astype(o_ref.dtype)


def _tile_spec():
    return pl.BlockSpec((1, 1, D_MODEL), lambda i: (i, 0, 0))


def _row_spec(width=D_MODEL, col=0):
    return pl.BlockSpec((ROW_TILE, width), lambda i: (i, col))


def _vec_spec(width=D_MODEL):
    return pl.BlockSpec((1, width), lambda i: (0, 0))


def _norm_mod(x, g, sc, sh):
    return pl.pallas_call(
        _norm_mod_kernel,
        out_shape=jax.ShapeDtypeStruct((N_TOK, D_MODEL), BF16),
        grid=(N_TILES,),
        in_specs=[_row_spec(), _vec_spec(), _tile_spec(), _tile_spec()],
        out_specs=_row_spec(),
        compiler_params=_cparams(("parallel",)),
        name="norm_mod",
    )(x, g.reshape(1, D_MODEL), sc, sh)


def _norm_mod_router_kernel(x_ref, g_ref, sc_ref, sh_ref, wr_ref, o_ref, route_ref):
    h = _rms(x_ref[...], g_ref[...]) * (1.0 + sc_ref[0]) + sh_ref[0]
    o_ref[...] = h
    logits = jnp.dot(h, wr_ref[...], precision=HIGHEST, preferred_element_type=F32)
    lane = lax.broadcasted_iota(jnp.int32, logits.shape, 1).astype(F32)
    neg = jnp.float32(-jnp.inf)
    lg = jnp.where(lane < N_EXPERTS, logits, neg)
    m1 = jnp.max(lg, axis=-1, keepdims=True)
    i1 = jnp.min(jnp.where(lg == m1, lane, float(LANES)), axis=-1, keepdims=True)
    lg2 = jnp.where(lane == i1, neg, lg)
    m2 = jnp.max(lg2, axis=-1, keepdims=True)
    i2 = jnp.min(jnp.where(lg2 == m2, lane, float(LANES)), axis=-1, keepdims=True)
    e = jnp.exp(m2 - m1)
    w1 = 1.0 / (1.0 + e)
    w2 = e * w1
    route_ref[...] = jnp.where(lane == 0.0, i1, jnp.where(lane == 1.0, i2, jnp.where(lane == 2.0, w1,
                                                                                    jnp.where(lane == 3.0, w2, 0.0))))


def _norm_mod_router(x, g, sc, sh, w_router):
    wr = jnp.pad(w_router, ((0, 0), (0, LANES - N_EXPERTS)))
    return pl.pallas_call(
        _norm_mod_router_kernel,
        out_shape=(jax.ShapeDtypeStruct((N_TOK, D_MODEL), F32),
                   jax.ShapeDtypeStruct((N_TOK, LANES), F32)),
        grid=(N_TILES,),
        in_specs=[_row_spec(), _vec_spec(), _tile_spec(), _tile_spec(),
                  pl.BlockSpec((D_MODEL, LANES), lambda i: (0, 0))],
        out_specs=(_row_spec(), _row_spec(LANES)),
        compiler_params=_cparams(("parallel",)),
        name="norm_mod_router",
    )(x, g.reshape(1, D_MODEL), sc, sh, wr)


def _mm_kernel(a_ref, w_ref, o_ref):
    o_ref[...] = jnp.dot(a_ref[...].astype(BF16), w_ref[...], preferred_element_type=F32).astype(o_ref.dtype)


def _mm(a, w, out_dtype=F32, tm=512, tn=512, name="mm"):
    m, k = a.shape
    n = w.shape[1]
    tm, tn = min(tm, m), min(tn, n)
    return pl.pallas_call(
        _mm_kernel,
        out_shape=jax.ShapeDtypeStruct((m, n), out_dtype),
        grid=(n // tn, m // tm),
        in_specs=[pl.BlockSpec((tm, k), lambda j, i: (i, 0)),
                  pl.BlockSpec((k, tn), lambda j, i: (0, j))],
        out_specs=pl.BlockSpec((tm, tn), lambda j, i: (i, j)),
        compiler_params=_cparams(("parallel", "parallel")),
        name=name,
    )(a, w)


def _mm_split(a, w, tn, tm=512, name="mm_split"):
    m, k = a.shape
    p = w.shape[1] // tn
    return pl.pallas_call(
        _mm_kernel,
        out_shape=jax.ShapeDtypeStruct((p, m, tn), F32),
        grid=(p, m // tm),
        in_specs=[pl.BlockSpec((tm, k), lambda j, i: (i, 0)),
                  pl.BlockSpec((k, tn), lambda j, i: (0, j))],
        out_specs=pl.BlockSpec((None, tm, tn), lambda j, i: (j, i, 0)),
        compiler_params=_cparams(("parallel", "parallel")),
        name=name,
    )(a, w)


def _mm_resid_kernel(a_ref, w_ref, x_ref, g_ref, gt_ref, o_ref):
    y = jnp.dot(a_ref[...], w_ref[...], preferred_element_type=F32)
    o_ref[...] = x_ref[...] + gt_ref[0] * _rms(y, g_ref[...])


def _mm_resid(a, w, x, g, gt):
    k = a.shape[1]
    return pl.pallas_call(
        _mm_resid_kernel,
        out_shape=jax.ShapeDtypeStruct((N_TOK, D_MODEL), F32),
        grid=(N_TILES,),
        in_specs=[_row_spec(k), pl.BlockSpec((k, D_MODEL), lambda i: (0, 0)),
                  _row_spec(), _vec_spec(), _tile_spec()],
        out_specs=_row_spec(),
        compiler_params=_cparams(("parallel",)),
        name="mm_resid",
    )(a, w, x, g.reshape(1, D_MODEL), gt)


def _mm2_resid_kernel(a1_ref, a2_ref, w_ref, x_ref, g_ref, gt_ref, o_ref):
    k1 = a1_ref.shape[1]
    y = (jnp.dot(a1_ref[...], w_ref[:k1], preferred_element_type=F32)
         + jnp.dot(a2_ref[...], w_ref[k1:], preferred_element_type=F32))
    o_ref[...] = x_ref[...] + gt_ref[0] * _rms(y, g_ref[...])


def _mm2_resid(a1, a2, w, x, g, gt):
    k1, k2 = a1.shape[1], a2.shape[1]
    return pl.pallas_call(
        _mm2_resid_kernel,
        out_shape=jax.ShapeDtypeStruct((N_TOK, D_MODEL), F32),
        grid=(N_TILES,),
        in_specs=[_row_spec(k1), _row_spec(k2), pl.BlockSpec((k1 + k2, D_MODEL), lambda i: (0, 0)),
                  _row_spec(), _vec_spec(), _tile_spec()],
        out_specs=_row_spec(),
        compiler_params=_cparams(("parallel",)),
        name="mm2_resid",
    )(a1, a2, w, x, g.reshape(1, D_MODEL), gt)


def _swiglu_kernel(a_ref, wg_ref, wu_ref, o_ref):
    a = a_ref[...]
    g = jnp.dot(a, wg_ref[...], preferred_element_type=F32)
    u = jnp.dot(a, wu_ref[...], preferred_element_type=F32)
    o_ref[...] = ((g * jax.nn.sigmoid(g)) * u).astype(o_ref.dtype)


def _swiglu(a, wg, wu, tm=512, tn=FF_TN):
    f = wg.shape[1]
    m = a.shape[0]
    return pl.pallas_call(
        _swiglu_kernel,
        out_shape=jax.ShapeDtypeStruct((m, f), BF16),
        grid=(f // tn, m // tm),
        in_specs=[pl.BlockSpec((tm, D_MODEL), lambda j, i: (i, 0)),
                  pl.BlockSpec((D_MODEL, tn), lambda j, i: (0, j)),
                  pl.BlockSpec((D_MODEL, tn), lambda j, i: (0, j))],
        out_specs=pl.BlockSpec((tm, tn), lambda j, i: (i, j)),
        compiler_params=_cparams(("parallel", "parallel")),
        name="swiglu",
    )(a, wg, wu)


def _moe_plan(route):
    e_flat = route[:, :2].astype(jnp.int32).reshape(-1)
    onehot = (e_flat[:, None] == jnp.arange(N_EXPERTS, dtype=jnp.int32)[None, :]).astype(jnp.int32)
    pos = jnp.cumsum(onehot, axis=0) - onehot
    padded = (jnp.sum(onehot, axis=0) + MOE_TM - 1) // MOE_TM * MOE_TM
    ends = jnp.cumsum(padded)
    dest = jnp.sum(onehot * ((ends - padded)[None, :] + pos), axis=1)
    token = jnp.arange(2 * N_TOK, dtype=jnp.int32) // 2
    src_tok = jnp.zeros((MOE_ROWS,), jnp.int32).at[dest].set(token)
    tile_start = jnp.arange(MOE_TILES, dtype=jnp.int32) * MOE_TM
    tile_e = jnp.sum((tile_start[:, None] >= ends[None, :]).astype(jnp.int32), axis=1)
    tile_e = jnp.minimum(tile_e, N_EXPERTS - 1)
    n_used = (ends[-1:] // MOE_TM).astype(jnp.int32)
    return dest, src_tok, tile_e, n_used


def _row_gather_kernel(idx_ref, src_ref, o_ref, rows_s):
    rows = o_ref.shape[0]
    base = pl.program_id(0) * rows

    def copy8(c, carry):
        for u in range(8):
            r = c * 8 + u
            rows_s[pl.ds(r, 1), :] = src_ref[pl.ds(idx_ref[base + r], 1), :]
        return carry

    lax.fori_loop(0, rows // 8, copy8, 0)
    o_ref[...] = rows_s[...].astype(o_ref.dtype)


def _row_gather(idx, src, n_rows, tm):
    n_src, w = src.shape
    return pl.pallas_call(
        _row_gather_kernel,
        out_shape=jax.ShapeDtypeStruct((n_rows, w), BF16),
        grid_spec=pltpu.PrefetchScalarGridSpec(
            num_scalar_prefetch=1, grid=(n_rows // tm,),
            in_specs=[pl.BlockSpec((n_src, w), lambda i, idx_ref: (0, 0), pipeline_mode=pl.Buffered(1))],
            out_specs=pl.BlockSpec((tm, w), lambda i, idx_ref: (i, 0)),
            scratch_shapes=[pltpu.VMEM((tm, w), F32)]),
        compiler_params=_cparams(("arbitrary",)),
        name="moe_gather",
    )(idx, src)


def _expert_changed(te_ref, i):
    return jnp.logical_or(i == 0, te_ref[i] != te_ref[jnp.maximum(i - 1, 0)])


def _moe_swiglu_kernel(te_ref, nu_ref, a_ref, wg_ref, wu_ref, o_ref, wg_s, wu_s):
    i = pl.program_id(1)

    @pl.when(_expert_changed(te_ref, i))
    def _():
        wg_s[...] = wg_ref[...].astype(BF16)
        wu_s[...] = wu_ref[...].astype(BF16)

    @pl.when(i < nu_ref[0])
    def _():
        a = a_ref[...]
        g = jnp.dot(a, wg_s[...], preferred_element_type=F32)
        u = jnp.dot(a, wu_s[...], preferred_element_type=F32)
        o_ref[...] = ((g * jax.nn.sigmoid(g)) * u).astype(o_ref.dtype)

    @pl.when(i >= nu_ref[0])
    def _():
        o_ref[...] = jnp.zeros_like(o_ref)


def _moe_swiglu(tile_e, n_used, a, wg, wu, layer, tn=FF_TN):
    f = wg.shape[3]
    w_spec = pl.BlockSpec((None, None, D_MODEL, tn), lambda j, i, te, nu: (layer, te[i], 0, j))
    return pl.pallas_call(
        _moe_swiglu_kernel,
        out_shape=jax.ShapeDtypeStruct((MOE_ROWS, f), BF16),
        grid_spec=pltpu.PrefetchScalarGridSpec(
            num_scalar_prefetch=2, grid=(f // tn, MOE_TILES),
            in_specs=[pl.BlockSpec((MOE_TM, D_MODEL), lambda j, i, te, nu: (i, 0)), w_spec, w_spec],
            out_specs=pl.BlockSpec((MOE_TM, tn), lambda j, i, te, nu: (i, j)),
            scratch_shapes=[pltpu.VMEM((D_MODEL, tn), BF16), pltpu.VMEM((D_MODEL, tn), BF16)]),
        compiler_params=_cparams(("parallel", "arbitrary")),
        name="moe_swiglu",
    )(tile_e, n_used, a, wg, wu)


def _moe_down_kernel(te_ref, nu_ref, h_ref, wd_ref, o_ref, wd_s):
    i = pl.program_id(0)

    @pl.when(_expert_changed(te_ref, i))
    def _():
        wd_s[...] = wd_ref[...].astype(BF16)

    @pl.when(i < nu_ref[0])
    def _():
        o_ref[...] = jnp.dot(h_ref[...], wd_s[...], preferred_element_type=F32)

    @pl.when(i >= nu_ref[0])
    def _():
        o_ref[...] = jnp.zeros_like(o_ref)


def _moe_down(tile_e, n_used, hid, wd, layer):
    f = wd.shape[2]
    return pl.pallas_call(
        _moe_down_kernel,
        out_shape=jax.ShapeDtypeStruct((MOE_ROWS, D_MODEL), F32),
        grid_spec=pltpu.PrefetchScalarGridSpec(
            num_scalar_prefetch=2, grid=(MOE_TILES,),
            in_specs=[pl.BlockSpec((MOE_TM, f), lambda i, te, nu: (i, 0)),
                      pl.BlockSpec((None, None, f, D_MODEL), lambda i, te, nu: (layer, te[i], 0, 0))],
            out_specs=pl.BlockSpec((MOE_TM, D_MODEL), lambda i, te, nu: (i, 0)),
            scratch_shapes=[pltpu.VMEM((f, D_MODEL), BF16)]),
        compiler_params=_cparams(("arbitrary",)),
        name="moe_down",
    )(tile_e, n_used, hid, wd)


def _moe_combine_kernel(dest_ref, y_ref, route_ref, x_ref, g_ref, gt_ref, o_ref, buf, sem):
    base = pl.program_id(0) * ROW_TILE

    def issue(r, carry):
        a = 2 * (base + r)
        pltpu.make_async_copy(y_ref.at[dest_ref[a]], buf.at[0, r], sem).start()
        pltpu.make_async_copy(y_ref.at[dest_ref[a + 1]], buf.at[1, r], sem).start()
        return carry

    lax.fori_loop(0, ROW_TILE, issue, 0)

    def wait(r, carry):
        pltpu.make_async_copy(y_ref.at[0], buf.at[0, 0], sem).wait()
        pltpu.make_async_copy(y_ref.at[0], buf.at[0, 0], sem).wait()
        return carry

    lax.fori_loop(0, ROW_TILE, wait, 0)
    route = route_ref[...]
    ff = route[:, 2:3] * buf[0] + route[:, 3:4] * buf[1]
    o_ref[...] = x_ref[...] + gt_ref[0] * _rms(ff, g_ref[...])


def _moe_combine(dest, y_sorted, route, x, g, gt):
    return pl.pallas_call(
        _moe_combine_kernel,
        out_shape=jax.ShapeDtypeStruct((N_TOK, D_MODEL), F32),
        grid_spec=pltpu.PrefetchScalarGridSpec(
            num_scalar_prefetch=1, grid=(N_TILES,),
            in_specs=[pl.BlockSpec(memory_space=pl.ANY),
                      pl.BlockSpec((ROW_TILE, LANES), lambda i, d: (i, 0)),
                      pl.BlockSpec((ROW_TILE, D_MODEL), lambda i, d: (i, 0)),
                      pl.BlockSpec((1, D_MODEL), lambda i, d: (0, 0)),
                      pl.BlockSpec((1, 1, D_MODEL), lambda i, d: (i, 0, 0))],
            out_specs=pl.BlockSpec((ROW_TILE, D_MODEL), lambda i, d: (i, 0)),
            scratch_shapes=[pltpu.VMEM((2, ROW_TILE, D_MODEL), F32), pltpu.SemaphoreType.DMA(())]),
        compiler_params=_cparams(("arbitrary",)),
        name="moe_combine",
    )(dest, y_sorted, route, x, g.reshape(1, D_MODEL), gt)


def _lora_kernel(x_ref, w1_ref, w2_ref, b_ref, o_ref, *, mid, out):
    t = mid(jnp.dot(x_ref[...], w1_ref[...], preferred_element_type=F32))
    o_ref[...] = out(jnp.dot(t.astype(BF16), w2_ref[...], preferred_element_type=F32) + b_ref[...])


def _lora(x, w1, w2, b, mid, out, tm=512):
    m = x.shape[0]
    r = w1.shape[1]
    return pl.pallas_call(
        functools.partial(_lora_kernel, mid=mid, out=out),
        out_shape=jax.ShapeDtypeStruct((m, D_MODEL), F32),
        grid=(m // tm,),
        in_specs=[pl.BlockSpec((tm, D_MODEL), lambda i: (i, 0)),
                  pl.BlockSpec((D_MODEL, r), lambda i: (0, 0)),
                  pl.BlockSpec((r, D_MODEL), lambda i: (0, 0)),
                  pl.BlockSpec((1, D_MODEL), lambda i: (0, 0))],
        out_specs=pl.BlockSpec((tm, D_MODEL), lambda i: (i, 0)),
        compiler_params=_cparams(("parallel",)),
        name="lora",
    )(x, w1.astype(BF16), w2.astype(BF16), b.reshape(1, D_MODEL))


def _identity(x):
    return x


def _decay(wr):
    return jnp.exp(-float(np.exp(-0.5)) * jax.nn.sigmoid(wr))


def _block_ones():
    i = np.arange(LANES)
    return jnp.asarray((i[:, None] // HEAD == i[None, :] // HEAD).astype(np.float32))


def _head_sum(x, bo):
    parts = [jnp.dot(x[:, c * LANES:(c + 1) * LANES], bo, precision=HIGHEST, preferred_element_type=F32)
             for c in range(x.shape[1] // LANES)]
    return jnp.concatenate(parts, axis=-1)


class _ScanMode(NamedTuple):
    prompt: bool
    groups: int
    rep: int
    r2: int
    tb: int


def _scan_gather(mode, src_ref, fwd_ref, bwd_ref, s):
    sb = mode.tb - 1 - s
    if mode.prompt:
        for d, (ref, tt) in enumerate(((fwd_ref, s), (bwd_ref, sb))):
            z = ref[pl.ds(pl.multiple_of(tt * BATCH, BATCH), BATCH), :]
            for hp in range(PAIRS):
                r0 = d * HEAD + hp * BATCH
                src_ref[r0:r0 + BATCH, :] = z[:, hp * LANES:(hp + 1) * LANES]
        return src_ref[...].T
    unit = 4 * mode.r2
    for q in range(mode.rep):
        for d, (ref, tt) in enumerate(((fwd_ref, s), (bwd_ref, sb))):
            for b in range(DEC_BATCH):
                r0 = q * unit + (2 * d + b) * mode.r2
                src_ref[r0:r0 + mode.r2, 0:HEAD] = ref[b, tt]
    return src_ref[...].T[0:HEAD]


def _scan_value_slab(mode, vt):
    vsub = HEAD // mode.rep
    out = vt[0:vsub]
    if mode.rep > 1:
        lane = lax.broadcasted_iota(jnp.int32, (vsub, LANES), 1)
        for q in range(1, mode.rep):
            out = jnp.where(lane >= q * (LANES // mode.rep), vt[q * vsub:(q + 1) * vsub], out)
    return out


def _scan_scatter(mode, src_ref, ys_ref, yf_ref, yb_ref, s):
    sb = mode.tb - 1 - s
    if mode.prompt:
        tr = jnp.concatenate([ys_ref[s, 0], ys_ref[s, 1]], axis=0).T
        for d, (ref, tt) in enumerate(((yf_ref, s), (yb_ref, sb))):
            row0 = pl.multiple_of(tt * BATCH, BATCH)
            for hp in range(PAIRS):
                r0 = d * HEAD + hp * BATCH
                ref[pl.ds(row0, BATCH), hp * LANES:(hp + 1) * LANES] = tr[r0:r0 + BATCH]
        return
    vsub = HEAD // mode.rep
    y = ys_ref[s, 0]
    for q in range(mode.rep):
        src_ref[q * vsub:(q + 1) * vsub, :] = y
    tr = src_ref[...].T
    unit = 4 * mode.r2
    out = tr[0:unit]
    lane = lax.broadcasted_iota(jnp.int32, (unit, LANES), 1)
    for q in range(1, mode.rep):
        out = jnp.where(lane >= q * vsub, tr[q * unit:(q + 1) * unit], out)
    for d, (ref, tt) in enumerate(((yf_ref, s), (yb_ref, sb))):
        for b in range(DEC_BATCH):
            r0 = (2 * d + b) * mode.r2
            ref[b, tt] = out[r0:r0 + mode.r2, 0:HEAD]


def _n_acc(vsub):
    return 1 if vsub == HEAD else 2


def _scan_init(mode, t_axis, s_ref, s0_ref, src_ref):
    @pl.when(pl.program_id(t_axis) == 0)
    def _():
        if s0_ref is None:
            s_ref[...] = jnp.zeros_like(s_ref)
        else:
            s_ref[...] = s0_ref[...]

    if not mode.prompt:
        src_ref[...] = jnp.zeros_like(src_ref)


def _hgrn_scan_kernel(*refs, mode, has_s0, t_axis):
    qf, qb, xf, xb, vf, vb, lb_ref = refs[:7]
    refs = refs[7:]
    s0_ref = None
    if has_s0:
        s0_ref, refs = refs[0], refs[1:]
    of_ref, ob_ref, s_ref, src_ref, q_t, f_t, k_t, v_t, ys_ref = refs
    vsub = HEAD // mode.rep
    n_acc = _n_acc(vsub)
    _scan_init(mode, t_axis, s_ref, s0_ref, src_ref)
    lb = lb_ref[...]

    def prologue(s, carry):
        q_t[s] = _scan_gather(mode, src_ref, qf, qb, s)
        x = _scan_gather(mode, src_ref, xf, xb, s)
        f_t[s] = lb + (1.0 - lb) * jax.nn.sigmoid(x)
        k_t[s] = (1.0 - lb) * jax.nn.sigmoid(-x)
        v_t[s] = _scan_gather(mode, src_ref, vf, vb, s)
        return carry

    lax.fori_loop(0, mode.tb, prologue, 0)

    def step(s, carry):
        for g in range(mode.groups):
            r0 = g * HEAD
            vv = _scan_value_slab(mode, v_t[s, r0:r0 + HEAD, :])
            acc = [jnp.zeros((vsub, LANES), F32) for _ in range(n_acc)]
            for d in range(HEAD):
                r = r0 + d
                sn = s_ref[g, d] * f_t[s, r:r + 1, :] + vv * k_t[s, r:r + 1, :]
                s_ref[g, d] = sn
                acc[d % n_acc] = acc[d % n_acc] +sn * q_t[s, r:r + 1, :]
            ys_ref[s, g] = sum(acc[1:], acc[0])
        return carry

    lax.fori_loop(0, mode.tb, step, 0)

    def epilogue(s, carry):
        _scan_scatter(mode, src_ref, ys_ref, of_ref, ob_ref, s)
        return carry

    lax.fori_loop(0, mode.tb, epilogue, 0)


def _rwkv_scan_kernel(*refs, mode, has_s0, t_axis):
    ins = refs[:12]
    refs = refs[12:]
    s0_ref = None
    if has_s0:
        s0_ref, refs = refs[0], refs[1:]
    yf_ref, yb_ref, s_ref, src_ref, r_t, w_t, k_t, v_t, kk_t, a_t, ys_ref = refs
    vsub = HEAD // mode.rep
    vc = min(vsub, RWKV_CHUNK)
    n_acc = _n_acc(vc)
    _scan_init(mode, t_axis, s_ref, s0_ref, src_ref)

    def prologue(s, carry):
        for i, x_t in enumerate((r_t, w_t, k_t, v_t, kk_t, a_t)):
            x_t[s] = _scan_gather(mode, src_ref, ins[2 * i], ins[2 * i + 1], s)
        return carry

    lax.fori_loop(0, mode.tb, prologue, 0)

    def step(s, carry):
        for g in range(mode.groups):
            r0 = g * HEAD
            vv_all = _scan_value_slab(mode, v_t[s, r0:r0 + HEAD, :])
            for c in range(vsub // vc):
                rows = slice(c * vc, (c + 1) * vc)

                def tile(x_t, i):
                    return x_t[s, pl.ds(pl.multiple_of(r0 + i * 8, 8), 8), :]

                def pass1(i, acc):
                    kk8 = tile(kk_t, i)
                    for j in range(8):
                        acc = acc + s_ref[g, i * 8 + j, rows, :] * kk8[j:j + 1, :]
                    return acc

                sa = -lax.fori_loop(0, HEAD // 8, pass1, jnp.zeros((vc, LANES), F32))
                vv = vv_all[rows]

                def pass2(i, acc):
                    kk8, a8, w8, k8, r8 = (tile(x_t, i) for x_t in (kk_t, a_t, w_t, k_t, r_t))
                    b8 = kk8 * a8
                    for j in range(8):
                        d = i * 8 + j
                        sn = s_ref[g, d, rows, :] * w8[j:j + 1, :] + sa * b8[j:j + 1, :] + vv * k8[j:j + 1, :]
                        s_ref[g, d, rows, :] = sn
                        acc = acc + sn * r8[j:j + 1, :]
                    return acc

                ys_ref[s, g, rows, :] = lax.fori_loop(0, HEAD // 8, pass2, jnp.zeros((vc, LANES), F32))
        return carry

    lax.fori_loop(0, mode.tb, step, 0)

    def epilogue(s, carry):
        _scan_scatter(mode, src_ref, ys_ref, yf_ref, yb_ref, s)
        return carry

    lax.fori_loop(0, mode.tb, epilogue, 0)


def _scan_scratch(mode, n_streams):
    rows = mode.groups * HEAD
    vsub = HEAD // mode.rep
    return ([pltpu.VMEM((LANES, LANES), F32)]
            + [pltpu.VMEM((mode.tb, rows, LANES), F32) for _ in range(n_streams)]
            + [pltpu.VMEM((mode.tb, mode.groups, vsub, LANES), F32)])


def _hgrn_scans(slabs, lb, state0):
    mode = _ScanMode(prompt=True, groups=2, rep=1, r2=0, tb=16)
    n_t = SEQ // mode.tb
    rows = mode.tb * BATCH
    width = PAIRS * LANES
    assert width == H_A * HEAD

    def spec(p, rev):
        if rev:
            return pl.BlockSpec((None, rows, width), lambda t: (p, n_t - 1 - t, 0))
        return pl.BlockSpec((None, rows, width), lambda t: (p, t, 0))

    def ospec(rev):
        if rev:
            return pl.BlockSpec((rows, width), lambda t: (n_t - 1 - t, 0))
        return pl.BlockSpec((rows, width), lambda t: (t, 0))

    lb4 = lb.reshape(2, PAIRS, 2, HEAD)
    lb_p = jnp.broadcast_to(jnp.transpose(lb4, (2, 3, 0, 1))[..., None], (2, HEAD, 2, PAIRS, BATCH))
    lb_p = lb_p.reshape(2 * HEAD, LANES)
    st_spec = pl.BlockSpec((2, HEAD, HEAD, LANES), lambda t: (0, 0, 0, 0))
    o_shape = jax.ShapeDtypeStruct((N_PROMPT, width), F32)
    view = slabs
    o_fp, o_bp, s_fin = pl.pallas_call(
        functools.partial(_hgrn_scan_kernel, mode=mode, has_s0=False, t_axis=0),
        out_shape=(o_shape, o_shape, jax.ShapeDtypeStruct((2, HEAD, HEAD, LANES), F32)),
        grid=(n_t,),
        in_specs=[spec(0, False), spec(0, True), spec(1, False), spec(2, True), spec(3, False), spec(3, True),
                  pl.BlockSpec((2 * HEAD, LANES), lambda t: (0, 0))],
        out_specs=(ospec(False), ospec(True), st_spec),
        scratch_shapes=_scan_scratch(mode, 4),
        compiler_params=_cparams(("arbitrary",)),
        name="hgrn_scan_prompt",
    )(view, view, view, view, view, view, lb_p)
    s_fin = jnp.transpose(s_fin.reshape(2, HEAD, HEAD, 2, PAIRS, BATCH), (5, 3, 4, 0, 1, 2))
    s_fin = s_fin.reshape(BATCH, 2, H_A, DK_A, DV_A)

    mode = _ScanMode(prompt=False, groups=1, rep=4, r2=H_A, tb=32)
    n_t = DEC_SEQ // mode.tb
    view = slabs[:, N_PROMPT:].reshape(5, DEC_BATCH, DEC_SEQ, H_A, HEAD)

    def spec(p, rev):
        if rev:
            return pl.BlockSpec((None, DEC_BATCH, mode.tb, H_A, HEAD), lambda t: (p, 0, n_t - 1 - t, 0, 0))
        return pl.BlockSpec((None, DEC_BATCH, mode.tb, H_A, HEAD), lambda t: (p, 0, t, 0, 0))

    def ospec(rev):
        if rev:
            return pl.BlockSpec((DEC_BATCH, mode.tb, H_A, HEAD), lambda t: (0, n_t - 1 - t, 0, 0))
        return pl.BlockSpec((DEC_BATCH, mode.tb, H_A, HEAD), lambda t: (0, t, 0, 0))

    vsub = HEAD // mode.rep
    lb_s = jnp.transpose(lb.reshape(2, H_A, HEAD), (2, 0, 1))
    lb_s = jnp.broadcast_to(lb_s[:, None, :, None, :], (HEAD, mode.rep, 2, DEC_BATCH, H_A)).reshape(HEAD, LANES)
    s0 = state0.reshape(DEC_BATCH, 2, H_A, DK_A, mode.rep, vsub)
    s0 = jnp.transpose(s0, (3, 5, 4, 1, 0, 2)).reshape(1, HEAD, vsub, LANES)
    st_spec = pl.BlockSpec((1, HEAD, vsub, LANES), lambda t: (0, 0, 0, 0))
    o_shape = jax.ShapeDtypeStruct((DEC_BATCH, DEC_SEQ, H_A, HEAD), F32)
    o_fl, o_bl, _ = pl.pallas_call(
        functools.partial(_hgrn_scan_kernel, mode=mode, has_s0=True, t_axis=0),
        out_shape=(o_shape, o_shape, jax.ShapeDtypeStruct((1, HEAD, vsub, LANES), F32)),
        grid=(n_t,),
        in_specs=[spec(0, False), spec(0, True), spec(1, False), spec(2, True), spec(3, False), spec(3, True),
                  pl.BlockSpec((HEAD, LANES), lambda t: (0, 0)), st_spec],
        out_specs=(ospec(False), ospec(True), st_spec),
        scratch_shapes=_scan_scratch(mode, 4),
        compiler_params=_cparams(("arbitrary",)),
        name="hgrn_scan_latent",
    )(view, view, view, view, view, view, lb_s, s0)
    return o_fp, o_bp, o_fl.reshape(N_SAMPLE, width), o_bl.reshape(N_SAMPLE, width), s_fin


def _rwkv_scans(r, w_f, w_b, k, v, kk, a, state0):
    streams = ((r, r), (w_f, w_b), (k, k), (v, v), (kk, kk), (a, a))
    mode = _ScanMode(prompt=True, groups=2, rep=1, r2=0, tb=16)
    n_t = SEQ // mode.tb
    rows = mode.tb * BATCH
    width = PAIRS * LANES
    n_half = D_MODEL // width

    fwd = pl.BlockSpec((rows, width), lambda h, t: (t, h))
    bwd = pl.BlockSpec((rows, width), lambda h, t: (n_t - 1 - t, h))
    st_spec = pl.BlockSpec((None, 2, HEAD, HEAD, LANES), lambda h, t: (h, 0, 0, 0, 0))
    y_shape = jax.ShapeDtypeStruct((N_PROMPT, D_MODEL), F32)
    args = []
    for x_f, x_b in streams:
        args += [x_f, x_b]
    y_fp, y_bp, s_fin = pl.pallas_call(
        functools.partial(_rwkv_scan_kernel, mode=mode, has_s0=False, t_axis=1),
        out_shape=(y_shape, y_shape, jax.ShapeDtypeStruct((n_half, 2, HEAD, HEAD, LANES), F32)),
        grid=(n_half, n_t),
        in_specs=[fwd, bwd] * 6,
        out_specs=(fwd, bwd, st_spec),
        scratch_shapes=_scan_scratch(mode, 6),
        compiler_params=_cparams(("parallel", "arbitrary")),
        name="rwkv_scan_prompt",
    )(*args)
    s_fin = s_fin.reshape(n_half, 2, HEAD, HEAD, 2, PAIRS, BATCH)
    s_fin = jnp.transpose(s_fin, (6, 4, 0, 5, 1, 3, 2)).reshape(BATCH, 2, H_C, N_C, N_C)

    mode = _ScanMode(prompt=False, groups=1, rep=2, r2=H_C, tb=32)
    n_t = DEC_SEQ // mode.tb

    def view(x):
        return x[N_PROMPT:].reshape(DEC_BATCH, DEC_SEQ, H_C, HEAD)

    fwd = pl.BlockSpec((DEC_BATCH, mode.tb, H_C, HEAD), lambda t: (0, t, 0, 0))
    bwd = pl.BlockSpec((DEC_BATCH, mode.tb, H_C, HEAD), lambda t: (0, n_t - 1 - t, 0, 0))
    vsub = HEAD // mode.rep
    s0 = state0.reshape(DEC_BATCH, 2, H_C, mode.rep, vsub, N_C)
    s0 = jnp.transpose(s0, (5, 4, 3, 1, 0, 2)).reshape(1, HEAD, vsub, LANES)
    st_spec = pl.BlockSpec((1, HEAD, vsub, LANES), lambda t: (0, 0, 0, 0))
    y_shape = jax.ShapeDtypeStruct((DEC_BATCH, DEC_SEQ, H_C, HEAD), F32)
    args = []
    for x_f, x_b in streams:
        x_fv = view(x_f)
        args += [x_fv, x_fv if x_b is x_f else view(x_b)]
    y_fl, y_bl, _ = pl.pallas_call(
        functools.partial(_rwkv_scan_kernel, mode=mode, has_s0=True, t_axis=0),
        out_shape=(y_shape, y_shape, jax.ShapeDtypeStruct((1, HEAD, vsub, LANES), F32)),
        grid=(n_t,),
        in_specs=[fwd, bwd] * 6 + [st_spec],
        out_specs=(fwd, bwd, st_spec),
        scratch_shapes=_scan_scratch(mode, 6),
        compiler_params=_cparams(("arbitrary",)),
        name="rwkv_scan_latent",
    )(*args, s0)
    return y_fp, y_bp, y_fl.reshape(N_SAMPLE, D_MODEL), y_bl.reshape(N_SAMPLE, D_MODEL), s_fin


def _prompt_spec(width):
    return pl.BlockSpec((ROW_TILE, width), lambda i: (jnp.minimum(i, BATCH - 1), 0))


def _latent_spec(width):
    return pl.BlockSpec((ROW_TILE, width), lambda i: (jnp.maximum(i - BATCH, 0), 0))


def _hgrn_post_kernel(ofp_ref, obp_ref, ofl_ref, obl_ref, g_ref, gain_ref, bo_ref, o_ref):
    is_prompt = pl.program_id(0) < BATCH
    o = jnp.where(is_prompt, ofp_ref[...] + obp_ref[...], ofl_ref[...] + obl_ref[...])
    ms = _head_sum(o * o, bo_ref[...]) * (1.0 / DV_A)
    y = o * lax.rsqrt(ms + EPS) * gain_ref[...]
    g = g_ref[...]
    o_ref[...] = (y * (g * jax.nn.sigmoid(g))).astype(o_ref.dtype)


def _hgrn_post(o_fp, o_bp, o_fl, o_bl, slabs, gain):
    w = H_A * DV_A
    return pl.pallas_call(
        _hgrn_post_kernel,
        out_shape=jax.ShapeDtypeStruct((N_TOK, w), BF16),
        grid=(N_TILES,),
        in_specs=[_prompt_spec(w), _prompt_spec(w), _latent_spec(w), _latent_spec(w),
                  pl.BlockSpec((None, ROW_TILE, w), lambda i: (4, i, 0)),
                  _vec_spec(w), pl.BlockSpec((LANES, LANES), lambda i: (0, 0))],
        out_specs=_row_spec(w),
        compiler_params=_cparams(("parallel",)),
        name="hgrn_post",
    )(o_fp, o_bp, o_fl, o_bl, slabs, jnp.tile(gain, H_A).reshape(1, w), _block_ones())


def _q_up_kernel(cq_ref, gain_ref, w_ref, cos_ref, sin_ref, qn_ref, qr_ref):
    cq = _rms(cq_ref[...], gain_ref[...]).astype(BF16)
    q = jnp.dot(cq, w_ref[...], preferred_element_type=F32)
    qn_ref[...] = q[:, :512].astype(BF16)
    qr_ref[...] = (q[:, 512:768] * cos_ref[...] + q[:, 768:1024] * sin_ref[...]).astype(BF16)


def _q_up(tail, gain, w, cos, sin):
    return pl.pallas_call(
        _q_up_kernel,
        out_shape=(jax.ShapeDtypeStruct((N_TOK, 512), BF16), jax.ShapeDtypeStruct((N_TOK, 256), BF16)),
        grid=(N_TILES,),
        in_specs=[_row_spec(Q_RANK, T_CQ // Q_RANK),
                  _vec_spec(Q_RANK), pl.BlockSpec((Q_RANK, 1024), lambda i: (0, 0)),
                  _row_spec(256), _row_spec(256)],
        out_specs=(_row_spec(512), _row_spec(256)),
        compiler_params=_cparams(("parallel",)),
        name="q_up",
    )(tail, gain.reshape(1, Q_RANK), w, cos, sin)


def _kv_up_kernel(ckv_ref, kr_ref, krot_ref, gain_ref, w_ref, cos_ref, sin_ref, ckvn_ref, kn_ref, v_ref, krope_ref):
    ckv = _rms(ckv_ref[...], gain_ref[...])
    ckvn_ref[...] = ckv
    kv = jnp.dot(ckv.astype(BF16), w_ref[...], preferred_element_type=F32)
    kn_ref[...] = kv[:, :512].astype(BF16)
    v_ref[...] = kv[:, 512:].astype(BF16)
    krope_ref[...] = (kr_ref[...] * cos_ref[...] + krot_ref[...] * sin_ref[...]).astype(BF16)


def _kv_up(tail, gain, w, cos4, sin4):
    return pl.pallas_call(
        _kv_up_kernel,
        out_shape=(jax.ShapeDtypeStruct((N_TOK, KV_RANK), F32), jax.ShapeDtypeStruct((N_TOK, 512), BF16),
                   jax.ShapeDtypeStruct((N_TOK, 512), BF16), jax.ShapeDtypeStruct((N_TOK, LANES), BF16)),
        grid=(N_TILES,),
        in_specs=[_row_spec(LANES, T_CKV // LANES), _row_spec(LANES, T_KR // LANES), _row_spec(LANES, T_KROT // LANES),
                  _vec_spec(KV_RANK), pl.BlockSpec((KV_RANK, 1024), lambda i: (0, 0)),
                  _row_spec(LANES), _row_spec(LANES)],
        out_specs=(_row_spec(KV_RANK), _row_spec(512), _row_spec(512), _row_spec(LANES)),
        compiler_params=_cparams(("parallel",)),
        name="kv_up",
    )(tail, tail, tail, gain.reshape(1, KV_RANK), w, cos4, sin4)


def _attn_kernel(qn_ref, qr_ref, kn_ref, kr_ref, v_ref, o_ref, *, scale):
    hp = pl.program_id(1)
    qn = qn_ref[0]
    qr = qr_ref[0]
    kcat = jnp.concatenate([kn_ref[0], kr_ref[0]], axis=-1)
    v = v_ref[0]
    lane = lax.broadcasted_iota(jnp.int32, (1, LANES), 1)
    zero = jnp.zeros((), BF16)
    outs = []
    for j in range(2):
        mn = (lane // D_NOPE) == j
        mr = (lane // D_ROPE) == (2 * hp + j) % 4
        qcat = jnp.concatenate([jnp.where(mn, qn, zero), jnp.where(mr, qr, zero)], axis=-1)
        s = lax.dot_general(qcat, kcat, (((1,), (1,)), ((), ())), preferred_element_type=F32) * scale
        p = jnp.exp(s - jnp.max(s, axis=-1, keepdims=True))
        l = jnp.sum(p, axis=-1, keepdims=True)
        outs.append(jnp.dot(p.astype(BF16), v, preferred_element_type=F32) / l)
    o_ref[0] = jnp.where(lane < DV_B, outs[0], outs[1]).astype(o_ref.dtype)


def _attention(qn, qr, kn, kr, v, bsz, time_major, tq=256):
    t_len, tk = qn.shape[1], kn.shape[1]

    def spec(is_query, per_seq, col):
        rows = tq if is_query else tk
        if time_major:
            return pl.BlockSpec((1, rows, LANES), lambda b, h, q: (0, q if is_query else 0, b * per_seq + col(h)))
        return pl.BlockSpec((1, rows, LANES), lambda b, h, q: (b, q if is_query else 0, col(h)))

    return pl.pallas_call(
        functools.partial(_attn_kernel, scale=float((D_NOPE + D_ROPE) ** -0.5)),
        out_shape=jax.ShapeDtypeStruct(qn.shape, BF16),
        grid=(bsz, 4, t_len // tq),
        in_specs=[spec(True, 4, lambda h: h), spec(True, 2, lambda h: h // 2),
                  spec(False, 4, lambda h: h), spec(False, 1, lambda h: 0), spec(False, 4, lambda h: h)],
        out_specs=spec(True, 4, lambda h: h),
        compiler_params=_cparams(("parallel", "parallel", "parallel")),
        name="attention",
    )(qn, qr, kn, kr, v)


def _odd_pre_kernel(x_ref, xp_ref, xn_ref, g_ref, sc_ref, sh_ref, mu_ref, *o_refs):
    i = pl.program_id(0)
    g, sc, sh = g_ref[...], sc_ref[0], sh_ref[0]

    def nm(x):
        return _rms(x, g) * (1.0 + sc) + sh

    h = nm(x_ref[...])
    hp = nm(xp_ref[...])
    hn = nm(xn_ref[...])
    is_prompt = i < BATCH
    hp_p = jnp.where(i > 0, hp, 0.0)
    hn_p = jnp.where(i < BATCH - 1, hn, 0.0)
    prev_p = jnp.concatenate([hp_p, h[:ROW_TILE - BATCH]], axis=0)
    next_p = jnp.concatenate([h[BATCH:], hn_p], axis=0)
    part = (i - BATCH) % 4
    hp_s = jnp.where(part != 0, hp[HALO - 1:HALO], 0.0)
    hn_s = jnp.where(part != 3, hn[0:1], 0.0)
    row = lax.broadcasted_iota(jnp.int32, (ROW_TILE, 1), 0)
    prev_s = jnp.where(row == 0, hp_s, pltpu.roll(h, 1, axis=0))
    next_s = jnp.where(row == ROW_TILE - 1, hn_s, pltpu.roll(h, ROW_TILE - 1, axis=0))
    prev = jnp.where(is_prompt, prev_p, prev_s)
    nxt = jnp.where(is_prompt, next_p, next_s)
    xx = 0.5 * (prev + nxt) - h
    for j, o_ref in enumerate(o_refs):
        o_ref[...] = (h + xx * mu_ref[j:j + 1, :]).astype(o_ref.dtype)


def _odd_pre(x, g, sc, sh, mu):
    assert HALO == BATCH
    per = ROW_TILE // HALO
    last = N_TOK // HALO - 1
    return pl.pallas_call(
        _odd_pre_kernel,
        out_shape=tuple(jax.ShapeDtypeStruct((N_TOK, D_MODEL), BF16) for _ in range(6)),
        grid=(N_TILES,),
        in_specs=[_row_spec(),
                  pl.BlockSpec((HALO, D_MODEL), lambda i: (jnp.maximum(i * per - 1, 0), 0)),
                  pl.BlockSpec((HALO, D_MODEL), lambda i: (jnp.minimum((i + 1) * per, last), 0)),
                  _vec_spec(), _tile_spec(), _tile_spec(),
                  pl.BlockSpec((8, D_MODEL), lambda i: (0, 0))],
        out_specs=tuple(_row_spec() for _ in range(6)),
        compiler_params=_cparams(("parallel",)),
        name="odd_pre",
    )(x, x, x, g.reshape(1, D_MODEL), sc, sh, jnp.pad(mu, ((0, 2), (0, 0))))


def _rwkv_prep_kernel(k_ref, a_ref, kk_w_ref, ka_w_ref, bo_ref, kk_ref, k2_ref):
    k = k_ref[...]
    kk = k * kk_w_ref[...]
    kk_ref[...] = kk * lax.rsqrt(_head_sum(kk * kk, bo_ref[...]) + 1e-12)
    k2_ref[...] = k * (1.0 + (a_ref[...] - 1.0) * ka_w_ref[...])


def _rwkv_prep(k, a, k_k, k_a):
    return pl.pallas_call(
        _rwkv_prep_kernel,
        out_shape=(jax.ShapeDtypeStruct((N_TOK, D_MODEL), F32), jax.ShapeDtypeStruct((N_TOK, D_MODEL), F32)),
        grid=(N_TILES,),
        in_specs=[_row_spec(), _row_spec(), _vec_spec(), _vec_spec(),
                  pl.BlockSpec((LANES, LANES), lambda i: (0, 0))],
        out_specs=(_row_spec(), _row_spec()),
        compiler_params=_cparams(("parallel",)),
        name="rwkv_prep",
    )(k, a, k_k.reshape(1, D_MODEL), k_a.reshape(1, D_MODEL), _block_ones())


def _rwkv_post_kernel(yfp_ref, ybp_ref, yfl_ref, ybl_ref, r_ref, k_ref, v_ref, g_ref, rk_ref, lnw_ref, lnb_ref,
                      bo_ref, o_ref):
    bo = bo_ref[...]
    y = jnp.where(pl.program_id(0) < BATCH, yfp_ref[...] + ybp_ref[...], yfl_ref[...] + ybl_ref[...])
    d = y - _head_sum(y, bo) * (1.0 / N_C)
    var = _head_sum(d * d, bo) * (1.0 / N_C)
    yn = d * lax.rsqrt(var + GN_EPS) * lnw_ref[...] + lnb_ref[...]
    bonus = _head_sum(r_ref[...] * k_ref[...] * rk_ref[...], bo) * v_ref[...]
    o_ref[...] = ((yn + bonus) * g_ref[...]).astype(o_ref.dtype)


def _rwkv_post(y_fp, y_bp, y_fl, y_bl, r, k2, v, g, r_k, ln_w, ln_b):
    return pl.pallas_call(
        _rwkv_post_kernel,
        out_shape=jax.ShapeDtypeStruct((N_TOK, D_MODEL), BF16),
        grid=(N_TILES,),
        in_specs=([_prompt_spec(D_MODEL)] * 2 + [_latent_spec(D_MODEL)] * 2 + [_row_spec()] * 4 + [_vec_spec()] * 3
                  + [pl.BlockSpec((LANES, LANES), lambda i: (0, 0))]),
        out_specs=_row_spec(),
        compiler_params=_cparams(("parallel",)),
        name="rwkv_post",
    )(y_fp, y_bp, y_fl, y_bl, r, k2, v, g, r_k.reshape(1, D_MODEL), ln_w.reshape(1, D_MODEL),
      ln_b.reshape(1, D_MODEL), _block_ones())


def _rot_cols(w):
    wb = w.reshape(w.shape[:-1] + (2, 2, D_ROPE // 4))
    return jnp.concatenate([-wb[..., 1:, :], wb[..., :1, :]], axis=-2).reshape(w.shape)


def _rope_tables():
    rows = DEC_SEQ // GRID_W
    row, col = np.meshgrid(np.arange(rows), np.arange(GRID_W), indexing='ij')
    row = jnp.asarray(row.reshape(-1), F32)
    col = jnp.asarray(col.reshape(-1), F32)
    n_freq = D_ROPE // 4
    inv = 1.0 / (ROPE_BASE ** (jnp.arange(n_freq, dtype=F32) / n_freq))
    ar = row[:, None] * inv
    ac = col[:, None] * inv
    ang = jnp.concatenate([ar, ar, ac, ac], axis=-1)
    cos, sin = jnp.cos(ang), jnp.sin(ang)
    cos = jnp.concatenate([jnp.ones((N_PROMPT, D_ROPE), F32), jnp.tile(cos, (DEC_BATCH, 1))], axis=0)
    sin = jnp.concatenate([jnp.zeros((N_PROMPT, D_ROPE), F32), jnp.tile(sin, (DEC_BATCH, 1))], axis=0)
    return cos, sin


def _even_mixer(j, h, lb_all, cos, sin, cache_ckv, cache_krope, state_hgrn,
                ev_w_in, hgrn_norm, mla_q_norm, mla_w_q_up, mla_kv_norm, mla_w_kv_up):
    w_in = ev_w_in[j]
    n_a = 5 * H_A * DK_A
    w_kr = w_in[:, n_a + Q_RANK + KV_RANK:]
    w_tail = jnp.concatenate([w_in[:, n_a:n_a + Q_RANK + KV_RANK], jnp.tile(w_kr, (1, 4)),
                              jnp.tile(_rot_cols(w_kr), (1, 4))], axis=1).astype(BF16)
    slabs = _mm_split(h, w_in[:, :n_a].astype(BF16), H_A * DK_A, name="hgrn_proj")
    tail = _mm(h, w_tail, F32, tm=512, tn=T_N, name="mla_proj")

    o_fp, o_bp, o_fl, o_bl, s_h = _hgrn_scans(slabs, lb_all[j], state_hgrn[:, j])
    o_a = _hgrn_post(o_fp, o_bp, o_fl, o_bl, slabs, hgrn_norm[j])

    wq = mla_w_q_up[j].reshape(Q_RANK, H_B, D_NOPE + D_ROPE)
    wq_n = wq[:, :, :D_NOPE].reshape(Q_RANK, H_B * D_NOPE)
    wq_r = wq[:, :, D_NOPE:]
    wq_aug = jnp.concatenate([wq_n, wq_r.reshape(Q_RANK, -1), _rot_cols(wq_r).reshape(Q_RANK, -1)], axis=1).astype(BF16)
    qn, qr = _q_up(tail, mla_q_norm[j], wq_aug, jnp.tile(cos, (1, H_B)), jnp.tile(sin, (1, H_B)))
    wkv = mla_w_kv_up[j].reshape(KV_RANK, H_B, D_NOPE + DV_B)
    wkv_aug = jnp.concatenate([wkv[:, :, :D_NOPE].reshape(KV_RANK, -1), wkv[:, :, D_NOPE:].reshape(KV_RANK, -1)],
                              axis=1).astype(BF16)
    ckv_n, kn, vv, kr = _kv_up(tail, mla_kv_norm[j], wkv_aug, jnp.tile(cos, (1, 4)), jnp.tile(sin, (1, 4)))
    kv_ctx = _mm(cache_ckv[:, j].reshape(DEC_BATCH * PAST_LEN, KV_RANK), wkv_aug, BF16, name="kv_ctx")
    kn_c = kv_ctx[:, :512].reshape(DEC_BATCH, PAST_LEN, 512)
    v_c = kv_ctx[:, 512:].reshape(DEC_BATCH, PAST_LEN, 512)
    kr_c = jnp.tile(cache_krope[:, j], (1, 1, 4)).astype(BF16)

    def prompt(x):
        return x[:N_PROMPT].reshape(1, SEQ, BATCH * x.shape[-1])

    def latent(x):
        return x[N_PROMPT:].reshape(DEC_BATCH, DEC_SEQ, x.shape[-1])

    o_p = _attention(prompt(qn), prompt(qr), prompt(kn), prompt(kr), prompt(vv), BATCH, True)
    o_s = _attention(latent(qn), latent(qr),
                     jnp.concatenate([kn_c, latent(kn)], axis=1),
                     jnp.concatenate([kr_c, latent(kr)], axis=1),
                     jnp.concatenate([v_c, latent(vv)], axis=1), DEC_BATCH, False)
    o_att = jnp.concatenate([o_p.reshape(N_PROMPT, 512), o_s.reshape(N_SAMPLE, 512)], axis=0)

    new_ckv = jnp.transpose(ckv_n[:N_PROMPT].reshape(SEQ, BATCH, KV_RANK), (1, 0, 2))
    new_krope = jnp.transpose(tail[:N_PROMPT, T_KR:T_KR + D_ROPE].reshape(SEQ, BATCH, D_ROPE), (1, 0, 2))
    return o_a, o_att, (s_h, new_ckv, new_krope)


def _odd_mixer(j, x, g0, sc, sh, state_rwkv, rw_mu, rw_w_r, rw_w_k, rw_w_v, rw_w0, rw_w1, rw_w2, rw_a0, rw_a1, rw_a2,
               rw_g1, rw_g2, rw_k_k, rw_k_a, rw_r_k, rw_ln_w, rw_ln_b):
    xr, xw, xk, xv, xa, xg = _odd_pre(x, g0, sc, sh, rw_mu[j])
    r = _mm(xr, rw_w_r[j].astype(BF16), name="rwkv_r")
    k = _mm(xk, rw_w_k[j].astype(BF16), name="rwkv_k")
    v = _mm(xv, rw_w_v[j].astype(BF16), name="rwkv_v")
    a = _lora(xa, rw_a1[j], rw_a2[j], rw_a0[j], _identity, jax.nn.sigmoid)
    g = _lora(xg, rw_g1[j], rw_g2[j], jnp.zeros((D_MODEL,), F32), jax.nn.sigmoid, _identity)
    w_f = _lora(xw, rw_w1[j, 0], rw_w2[j, 0], rw_w0[j, 0], jnp.tanh, _decay)
    w_b = _lora(xw, rw_w1[j, 1], rw_w2[j, 1], rw_w0[j, 1], jnp.tanh, _decay)
    kk, k2 = _rwkv_prep(k, a, rw_k_k[j], rw_k_a[j])
    y_fp, y_bp, y_fl, y_bl, s_r = _rwkv_scans(r, w_f, w_b, k2, v, kk, a, state_rwkv[:, j])
    mix = _rwkv_post(y_fp, y_bp, y_fl, y_bl, r, k2, v, g, rw_r_k[j], rw_ln_w[j], rw_ln_b[j])
    return mix, s_r


def kernel(x_prompt, x_sample, cache_ckv, cache_krope, state_hgrn, state_rwkv, c, c_ctx, ada_w, ada_b, norm_gains, ev_w_in, hgrn_lb_logits, hgrn_norm, mla_q_norm, mla_w_q_up, mla_kv_norm, mla_w_kv_up, ev_w_out, rw_mu, rw_w_r, rw_w_k, rw_w_v, rw_w_o, rw_w0, rw_w1, rw_w2, rw_a0, rw_a1, rw_a2, rw_g1, rw_g2, rw_k_k, rw_k_a, rw_r_k, rw_ln_w, rw_ln_b, ffn_w_gate, ffn_w_up, ffn_w_down, moe_router, moe_w_gate, moe_w_up, moe_w_down):
    lb_all = jnp.cumsum(jax.nn.softmax(hgrn_lb_logits.astype(F32), axis=0), axis=0)
    lb_all = lb_all - lb_all[:1]
    cos, sin = _rope_tables()

    cond8 = jnp.zeros((8, D_MODEL), F32).at[0].set(c_ctx).at[1:1 + DEC_BATCH].set(c)
    mod = _modulation(cond8, ada_w, ada_b)
    tile_row = np.array([0] * BATCH + [1 + t // 4 for t in range(N_TILES - BATCH)])
    modt = mod[:, tile_row, :].reshape(DEPTH, N_TILES, 6, 1, D_MODEL)

    x = jnp.concatenate([jnp.transpose(x_prompt, (1, 0, 2)).reshape(N_PROMPT, D_MODEL),
                         x_sample.reshape(N_SAMPLE, D_MODEL)], axis=0)
    ckv_l, krope_l, hgrn_l, rwkv_l = [], [], [], []
    for l in range(DEPTH):
        j = l // 2
        sh_m, sc_m, gt_m, sh_f, sc_f, gt_f = [modt[l, :, i] for i in range(6)]
        if l % 2 == 0:
            h = _norm_mod(x, norm_gains[l, 0], sc_m, sh_m)
            o_a, o_att, (s_h, n_ckv, n_kr) = _even_mixer(j, h, lb_all, cos, sin, cache_ckv, cache_krope, state_hgrn,
                                                        ev_w_in, hgrn_norm, mla_q_norm, mla_w_q_up, mla_kv_norm,
                                                        mla_w_kv_up)
            hgrn_l.append(s_h)
            ckv_l.append(n_ckv)
            krope_l.append(n_kr)
            x = _mm2_resid(o_a, o_att, ev_w_out[j].astype(BF16), x, norm_gains[l, 1], gt_m)
            h = _norm_mod(x, norm_gains[l, 2], sc_f, sh_f)
            hid = _swiglu(h, ffn_w_gate[j].astype(BF16), ffn_w_up[j].astype(BF16))
            x = _mm_resid(hid, ffn_w_down[j].astype(BF16), x, norm_gains[l, 3], gt_f)
        else:
            mix, s_r = _odd_mixer(j, x, norm_gains[l, 0], sc_m, sh_m, state_rwkv, rw_mu, rw_w_r, rw_w_k, rw_w_v,
                                  rw_w0, rw_w1, rw_w2, rw_a0, rw_a1, rw_a2, rw_g1, rw_g2, rw_k_k, rw_k_a, rw_r_k,
                                  rw_ln_w, rw_ln_b)
            rwkv_l.append(s_r)
            x = _mm_resid(mix, rw_w_o[j].astype(BF16), x, norm_gains[l, 1], gt_m)
            h, route = _norm_mod_router(x, norm_gains[l, 2], sc_f, sh_f, moe_router[j])
            dest, src_tok, tile_e, n_used = _moe_plan(route)
            h_sorted = _row_gather(src_tok, h, MOE_ROWS, MOE_TM)
            hid = _moe_swiglu(tile_e, n_used, h_sorted, moe_w_gate, moe_w_up, j)
            y_sorted = _moe_down(tile_e, n_used, hid, moe_w_down, j)
            x = _moe_combine(dest, y_sorted, route, x, norm_gains[l, 3], gt_f)

    y_prompt = jnp.transpose(x[:N_PROMPT].reshape(SEQ, BATCH, D_MODEL), (1, 0, 2))
    y_sample = x[N_PROMPT:].reshape(DEC_BATCH, DEC_SEQ, D_MODEL)
    return (y_prompt, y_sample, jnp.stack(ckv_l, axis=1), jnp.stack(krope_l, axis=1),
            jnp.stack(hgrn_l, axis=1), jnp.stack(rwkv_l, axis=1))
```

```python
import functools
from typing import NamedTuple

import numpy as np
import jax
import jax.numpy as jnp
from jax import lax
from jax.experimental import pallas as pl
from jax.experimental.pallas import tpu as pltpu

D_MODEL = 1024
BATCH = 16
SEQ = 256
DEPTH = 4
DEC_BATCH = 2
DEC_SEQ = 1024
PAST_LEN = 256
GRID_W = 64
H_A = 8
DK_A = 64
DV_A = 64
H_B = 8
Q_RANK = 256
KV_RANK = 128
D_NOPE = 64
D_ROPE = 32
DV_B = 64
ROPE_BASE = 10000.0
H_C = 16
N_C = 64
D_FF = 2816
N_EXPERTS = 8
EPS = 1e-6
GN_EPS = 64e-5

F32 = jnp.float32
BF16 = jnp.bfloat16
HIGHEST = lax.Precision.HIGHEST

N_PROMPT = BATCH * SEQ
N_SAMPLE = DEC_BATCH * DEC_SEQ
N_TOK = N_PROMPT + N_SAMPLE
ROW_TILE = 256
N_TILES = N_TOK // ROW_TILE
LANES = 128
HEAD = 64
VMEM_LIMIT = 48 * 1024 * 1024
HALO = 16
PAIRS = 4
FF_TN = 1408
MOE_TM = 256
MOE_ROWS = 2 * N_TOK + N_EXPERTS * MOE_TM
MOE_TILES = MOE_ROWS // MOE_TM

T_CQ, T_CKV, T_KR, T_KROT, T_N = 0, 256, 384, 512, 640


def _cparams(sem):
    return pltpu.CompilerParams(dimension_semantics=sem, vmem_limit_bytes=VMEM_LIMIT)


def _mod_kernel(c_ref, w_ref, b_ref, o_ref):
    c = c_ref[...]
    s = (c * jax.nn.sigmoid(c)).astype(BF16)
    o_ref[0] = jnp.dot(s, w_ref[0].astype(BF16), preferred_element_type=F32) + b_ref[0]


def _modulation(cond8, ada_w, ada_b):
    tn = 768
    return pl.pallas_call(
        _mod_kernel,
        out_shape=jax.ShapeDtypeStruct((DEPTH, 8, 6 * D_MODEL), F32),
        grid=(DEPTH, 6 * D_MODEL // tn),
        in_specs=[pl.BlockSpec((8, D_MODEL), lambda l, n: (0, 0)),
                  pl.BlockSpec((1, D_MODEL, tn), lambda l, n: (l, 0, n)),
                  pl.BlockSpec((1, 1, tn), lambda l, n: (l, 0, n))],
        out_specs=pl.BlockSpec((1, 8, tn), lambda l, n: (l, 0, n)),
        compiler_params=_cparams(("parallel", "parallel")),
        name="modulation",
    )(cond8, ada_w, ada_b.reshape(DEPTH, 1, 6 * D_MODEL))


def _rms(x, g):
    return x * lax.rsqrt(jnp.mean(x * x, axis=-1, keepdims=True) + EPS) * g


def _norm_mod_kernel(x_ref, g_ref, sc_ref, sh_ref, o_ref):
    h = _rms(x_ref[...], g_ref[...]) * (1.0 + sc_ref[0]) + sh_ref[0]
    o_ref[...] = h.astype(o_ref.dtype)


def _tile_spec():
    return pl.BlockSpec((1, 1, D_MODEL), lambda i: (i, 0, 0))


def _row_spec(width=D_MODEL, col=0):
    return pl.BlockSpec((ROW_TILE, width), lambda i: (i, col))


def _vec_spec(width=D_MODEL):
    return pl.BlockSpec((1, width), lambda i: (0, 0))


def _norm_mod(x, g, sc, sh):
    return pl.pallas_call(
        _norm_mod_kernel,
        out_shape=jax.ShapeDtypeStruct((N_TOK, D_MODEL), BF16),
        grid=(N_TILES,),
        in_specs=[_row_spec(), _vec_spec(), _tile_spec(), _tile_spec()],
        out_specs=_row_spec(),
        compiler_params=_cparams(("parallel",)),
        name="norm_mod",
    )(x, g.reshape(1, D_MODEL), sc, sh)


def _norm_mod_router_kernel(x_ref, g_ref, sc_ref, sh_ref, wr_ref, o_ref, route_ref):
    h = _rms(x_ref[...], g_ref[...]) * (1.0 + sc_ref[0]) + sh_ref[0]
    o_ref[...] = h
    logits = jnp.dot(h, wr_ref[...], precision=HIGHEST, preferred_element_type=F32)
    lane = lax.broadcasted_iota(jnp.int32, logits.shape, 1).astype(F32)
    neg = jnp.float32(-jnp.inf)
    lg = jnp.where(lane < N_EXPERTS, logits, neg)
    m1 = jnp.max(lg, axis=-1, keepdims=True)
    i1 = jnp.min(jnp.where(lg == m1, lane, float(LANES)), axis=-1, keepdims=True)
    lg2 = jnp.where(lane == i1, neg, lg)
    m2 = jnp.max(lg2, axis=-1, keepdims=True)
    i2 = jnp.min(jnp.where(lg2 == m2, lane, float(LANES)), axis=-1, keepdims=True)
    e = jnp.exp(m2 - m1)
    w1 = 1.0 / (1.0 + e)
    w2 = e * w1
    route_ref[...] = jnp.where(lane == 0.0, i1, jnp.where(lane == 1.0, i2, jnp.where(lane == 2.0, w1,
                                                                                    jnp.where(lane == 3.0, w2, 0.0))))


def _norm_mod_router(x, g, sc, sh, w_router):
    wr = jnp.pad(w_router, ((0, 0), (0, LANES - N_EXPERTS)))
    return pl.pallas_call(
        _norm_mod_router_kernel,
        out_shape=(jax.ShapeDtypeStruct((N_TOK, D_MODEL), F32),
                   jax.ShapeDtypeStruct((N_TOK, LANES), F32)),
        grid=(N_TILES,),
        in_specs=[_row_spec(), _vec_spec(), _tile_spec(), _tile_spec(),
                  pl.BlockSpec((D_MODEL, LANES), lambda i: (0, 0))],
        out_specs=(_row_spec(), _row_spec(LANES)),
        compiler_params=_cparams(("parallel",)),
        name="norm_mod_router",
    )(x, g.reshape(1, D_MODEL), sc, sh, wr)


def _mm_kernel(a_ref, w_ref, o_ref):
    o_ref[...] = jnp.dot(a_ref[...].astype(BF16), w_ref[...], preferred_element_type=F32).astype(o_ref.dtype)


def _mm(a, w, out_dtype=F32, tm=512, tn=512, name="mm"):
    m, k = a.shape
    n = w.shape[1]
    tm, tn = min(tm, m), min(tn, n)
    return pl.pallas_call(
        _mm_kernel,
        out_shape=jax.ShapeDtypeStruct((m, n), out_dtype),
        grid=(n // tn, m // tm),
        in_specs=[pl.BlockSpec((tm, k), lambda j, i: (i, 0)),
                  pl.BlockSpec((k, tn), lambda j, i: (0, j))],
        out_specs=pl.BlockSpec((tm, tn), lambda j, i: (i, j)),
        compiler_params=_cparams(("parallel", "parallel")),
        name=name,
    )(a, w)


def _mm_split(a, w, tn, tm=512, name="mm_split"):
    m, k = a.shape
    p = w.shape[1] // tn
    return pl.pallas_call(
        _mm_kernel,
        out_shape=jax.ShapeDtypeStruct((p, m, tn), F32),
        grid=(p, m // tm),
        in_specs=[pl.BlockSpec((tm, k), lambda j, i: (i, 0)),
                  pl.BlockSpec((k, tn), lambda j, i: (0, j))],
        out_specs=pl.BlockSpec((None, tm, tn), lambda j, i: (j, i, 0)),
        compiler_params=_cparams(("parallel", "parallel")),
        name=name,
    )(a, w)


def _mm_resid_kernel(a_ref, w_ref, x_ref, g_ref, gt_ref, o_ref):
    y = jnp.dot(a_ref[...], w_ref[...], preferred_element_type=F32)
    o_ref[...] = x_ref[...] + gt_ref[0] * _rms(y, g_ref[...])


def _mm_resid(a, w, x, g, gt):
    k = a.shape[1]
    return pl.pallas_call(
        _mm_resid_kernel,
        out_shape=jax.ShapeDtypeStruct((N_TOK, D_MODEL), F32),
        grid=(N_TILES,),
        in_specs=[_row_spec(k), pl.BlockSpec((k, D_MODEL), lambda i: (0, 0)),
                  _row_spec(), _vec_spec(), _tile_spec()],
        out_specs=_row_spec(),
        compiler_params=_cparams(("parallel",)),
        name="mm_resid",
    )(a, w, x, g.reshape(1, D_MODEL), gt)


def _mm2_resid_kernel(a1_ref, a2_ref, w_ref, x_ref, g_ref, gt_ref, o_ref):
    k1 = a1_ref.shape[1]
    y = (jnp.dot(a1_ref[...], w_ref[:k1], preferred_element_type=F32)
         + jnp.dot(a2_ref[...], w_ref[k1:], preferred_element_type=F32))
    o_ref[...] = x_ref[...] + gt_ref[0] * _rms(y, g_ref[...])


def _mm2_resid(a1, a2, w, x, g, gt):
    k1, k2 = a1.shape[1], a2.shape[1]
    return pl.pallas_call(
        _mm2_resid_kernel,
        out_shape=jax.ShapeDtypeStruct((N_TOK, D_MODEL), F32),
        grid=(N_TILES,),
        in_specs=[_row_spec(k1), _row_spec(k2), pl.BlockSpec((k1 + k2, D_MODEL), lambda i: (0, 0)),
                  _row_spec(), _vec_spec(), _tile_spec()],
        out_specs=_row_spec(),
        compiler_params=_cparams(("parallel",)),
        name="mm2_resid",
    )(a1, a2, w, x, g.reshape(1, D_MODEL), gt)


def _swiglu_kernel(a_ref, wg_ref, wu_ref, o_ref):
    a = a_ref[...]
    g = jnp.dot(a, wg_ref[...], preferred_element_type=F32)
    u = jnp.dot(a, wu_ref[...], preferred_element_type=F32)
    o_ref[...] = ((g * jax.nn.sigmoid(g)) * u).astype(o_ref.dtype)


def _swiglu(a, wg, wu, tm=512, tn=FF_TN):
    f = wg.shape[1]
    m = a.shape[0]
    return pl.pallas_call(
        _swiglu_kernel,
        out_shape=jax.ShapeDtypeStruct((m, f), BF16),
        grid=(f // tn, m // tm),
        in_specs=[pl.BlockSpec((tm, D_MODEL), lambda j, i: (i, 0)),
                  pl.BlockSpec((D_MODEL, tn), lambda j, i: (0, j)),
                  pl.BlockSpec((D_MODEL, tn), lambda j, i: (0, j))],
        out_specs=pl.BlockSpec((tm, tn), lambda j, i: (i, j)),
        compiler_params=_cparams(("parallel", "parallel")),
        name="swiglu",
    )(a, wg, wu)


def _moe_plan(route):
    e_flat = route[:, :2].astype(jnp.int32).reshape(-1)
    onehot = (e_flat[:, None] == jnp.arange(N_EXPERTS, dtype=jnp.int32)[None, :]).astype(jnp.int32)
    pos = jnp.cumsum(onehot, axis=0) - onehot
    padded = (jnp.sum(onehot, axis=0) + MOE_TM - 1) // MOE_TM * MOE_TM
    ends = jnp.cumsum(padded)
    dest = jnp.sum(onehot * ((ends - padded)[None, :] + pos), axis=1)
    token = jnp.arange(2 * N_TOK, dtype=jnp.int32) // 2
    src_tok = jnp.zeros((MOE_ROWS,), jnp.int32).at[dest].set(token)
    tile_start = jnp.arange(MOE_TILES, dtype=jnp.int32) * MOE_TM
    tile_e = jnp.sum((tile_start[:, None] >= ends[None, :]).astype(jnp.int32), axis=1)
    tile_e = jnp.minimum(tile_e, N_EXPERTS - 1)
    n_used = (ends[-1:] // MOE_TM).astype(jnp.int32)
    return dest, src_tok, tile_e, n_used


def _row_gather_kernel(idx_ref, src_ref, o_ref, rows_s):
    rows = o_ref.shape[0]
    base = pl.program_id(0) * rows

    def copy8(c, carry):
        for u in range(8):
            r = c * 8 + u
            rows_s[pl.ds(r, 1), :] = src_ref[pl.ds(idx_ref[base + r], 1), :]
        return carry

    lax.fori_loop(0, rows // 8, copy8, 0)
    o_ref[...] = rows_s[...].astype(o_ref.dtype)


def _row_gather(idx, src, n_rows, tm):
    n_src, w = src.shape
    return pl.pallas_call(
        _row_gather_kernel,
        out_shape=jax.ShapeDtypeStruct((n_rows, w), BF16),
        grid_spec=pltpu.PrefetchScalarGridSpec(
            num_scalar_prefetch=1, grid=(n_rows // tm,),
            in_specs=[pl.BlockSpec((n_src, w), lambda i, idx_ref: (0, 0), pipeline_mode=pl.Buffered(1))],
            out_specs=pl.BlockSpec((tm, w), lambda i, idx_ref: (i, 0)),
            scratch_shapes=[pltpu.VMEM((tm, w), F32)]),
        compiler_params=_cparams(("arbitrary",)),
        name="moe_gather",
    )(idx, src)


def _expert_changed(te_ref, i):
    return jnp.logical_or(i == 0, te_ref[i] != te_ref[jnp.maximum(i - 1, 0)])


def _moe_swiglu_kernel(te_ref, nu_ref, a_ref, wg_ref, wu_ref, o_ref, wg_s, wu_s):
    i = pl.program_id(1)

    @pl.when(_expert_changed(te_ref, i))
    def _():
        wg_s[...] = wg_ref[...].astype(BF16)
        wu_s[...] = wu_ref[...].astype(BF16)

    @pl.when(i < nu_ref[0])
    def _():
        a = a_ref[...]
        g = jnp.dot(a, wg_s[...], preferred_element_type=F32)
        u = jnp.dot(a, wu_s[...], preferred_element_type=F32)
        o_ref[...] = ((g * jax.nn.sigmoid(g)) * u).astype(o_ref.dtype)

    @pl.when(i >= nu_ref[0])
    def _():
        o_ref[...] = jnp.zeros_like(o_ref)


def _moe_swiglu(tile_e, n_used, a, wg, wu, layer, tn=FF_TN):
    f = wg.shape[3]
    w_spec = pl.BlockSpec((None, None, D_MODEL, tn), lambda j, i, te, nu: (layer, te[i], 0, j))
    return pl.pallas_call(
        _moe_swiglu_kernel,
        out_shape=jax.ShapeDtypeStruct((MOE_ROWS, f), BF16),
        grid_spec=pltpu.PrefetchScalarGridSpec(
            num_scalar_prefetch=2, grid=(f // tn, MOE_TILES),
            in_specs=[pl.BlockSpec((MOE_TM, D_MODEL), lambda j, i, te, nu: (i, 0)), w_spec, w_spec],
            out_specs=pl.BlockSpec((MOE_TM, tn), lambda j, i, te, nu: (i, j)),
            scratch_shapes=[pltpu.VMEM((D_MODEL, tn), BF16), pltpu.VMEM((D_MODEL, tn), BF16)]),
        compiler_params=_cparams(("parallel", "arbitrary")),
        name="moe_swiglu",
    )(tile_e, n_used, a, wg, wu)


def _moe_down_kernel(te_ref, nu_ref, h_ref, wd_ref, o_ref, wd_s):
    i = pl.program_id(0)

    @pl.when(_expert_changed(te_ref, i))
    def _():
        wd_s[...] = wd_ref[...].astype(BF16)

    @pl.when(i < nu_ref[0])
    def _():
        o_ref[...] = jnp.dot(h_ref[...], wd_s[...], preferred_element_type=F32)

    @pl.when(i >= nu_ref[0])
    def _():
        o_ref[...] = jnp.zeros_like(o_ref)


def _moe_down(tile_e, n_used, hid, wd, layer):
    f = wd.shape[2]
    return pl.pallas_call(
        _moe_down_kernel,
        out_shape=jax.ShapeDtypeStruct((MOE_ROWS, D_MODEL), F32),
        grid_spec=pltpu.PrefetchScalarGridSpec(
            num_scalar_prefetch=2, grid=(MOE_TILES,),
            in_specs=[pl.BlockSpec((MOE_TM, f), lambda i, te, nu: (i, 0)),
                      pl.BlockSpec((None, None, f, D_MODEL), lambda i, te, nu: (layer, te[i], 0, 0))],
            out_specs=pl.BlockSpec((MOE_TM, D_MODEL), lambda i, te, nu: (i, 0)),
            scratch_shapes=[pltpu.VMEM((f, D_MODEL), BF16)]),
        compiler_params=_cparams(("arbitrary",)),
        name="moe_down",
    )(tile_e, n_used, hid, wd)


def _moe_combine_kernel(dest_ref, y_ref, route_ref, x_ref, g_ref, gt_ref, o_ref, buf, sem):
    base = pl.program_id(0) * ROW_TILE

    def issue(r, carry):
        a = 2 * (base + r)
        pltpu.make_async_copy(y_ref.at[dest_ref[a]], buf.at[0, r], sem).start()
        pltpu.make_async_copy(y_ref.at[dest_ref[a + 1]], buf.at[1, r], sem).start()
        return carry

    lax.fori_loop(0, ROW_TILE, issue, 0)

    def wait(r, carry):
        pltpu.make_async_copy(y_ref.at[0], buf.at[0, 0], sem).wait()
        pltpu.make_async_copy(y_ref.at[0], buf.at[0, 0], sem).wait()
        return carry

    lax.fori_loop(0, ROW_TILE, wait, 0)
    route = route_ref[...]
    ff = route[:, 2:3] * buf[0] + route[:, 3:4] * buf[1]
    o_ref[...] = x_ref[...] + gt_ref[0] * _rms(ff, g_ref[...])


def _moe_combine(dest, y_sorted, route, x, g, gt):
    return pl.pallas_call(
        _moe_combine_kernel,
        out_shape=jax.ShapeDtypeStruct((N_TOK, D_MODEL), F32),
        grid_spec=pltpu.PrefetchScalarGridSpec(
            num_scalar_prefetch=1, grid=(N_TILES,),
            in_specs=[pl.BlockSpec(memory_space=pl.ANY),
                      pl.BlockSpec((ROW_TILE, LANES), lambda i, d: (i, 0)),
                      pl.BlockSpec((ROW_TILE, D_MODEL), lambda i, d: (i, 0)),
                      pl.BlockSpec((1, D_MODEL), lambda i, d: (0, 0)),
                      pl.BlockSpec((1, 1, D_MODEL), lambda i, d: (i, 0, 0))],
            out_specs=pl.BlockSpec((ROW_TILE, D_MODEL), lambda i, d: (i, 0)),
            scratch_shapes=[pltpu.VMEM((2, ROW_TILE, D_MODEL), F32), pltpu.SemaphoreType.DMA(())]),
        compiler_params=_cparams(("arbitrary",)),
        name="moe_combine",
    )(dest, y_sorted, route, x, g.reshape(1, D_MODEL), gt)


def _lora_kernel(x_ref, w1_ref, w2_ref, b_ref, o_ref, *, mid, out):
    t = mid(jnp.dot(x_ref[...], w1_ref[...], preferred_element_type=F32))
    o_ref[...] = out(jnp.dot(t.astype(BF16), w2_ref[...], preferred_element_type=F32) + b_ref[...])


def _lora(x, w1, w2, b, mid, out, tm=512):
    m = x.shape[0]
    r = w1.shape[1]
    return pl.pallas_call(
        functools.partial(_lora_kernel, mid=mid, out=out),
        out_shape=jax.ShapeDtypeStruct((m, D_MODEL), F32),
        grid=(m // tm,),
        in_specs=[pl.BlockSpec((tm, D_MODEL), lambda i: (i, 0)),
                  pl.BlockSpec((D_MODEL, r), lambda i: (0, 0)),
                  pl.BlockSpec((r, D_MODEL), lambda i: (0, 0)),
                  pl.BlockSpec((1, D_MODEL), lambda i: (0, 0))],
        out_specs=pl.BlockSpec((tm, D_MODEL), lambda i: (i, 0)),
        compiler_params=_cparams(("parallel",)),
        name="lora",
    )(x, w1.astype(BF16), w2.astype(BF16), b.reshape(1, D_MODEL))


def _identity(x):
    return x


def _decay(wr):
    return jnp.exp(-float(np.exp(-0.5)) * jax.nn.sigmoid(wr))


def _block_ones():
    i = np.arange(LANES)
    return jnp.asarray((i[:, None] // HEAD == i[None, :] // HEAD).astype(np.float32))


def _head_sum(x, bo):
    parts = [jnp.dot(x[:, c * LANES:(c + 1) * LANES], bo, precision=HIGHEST, preferred_element_type=F32)
             for c in range(x.shape[1] // LANES)]
    return jnp.concatenate(parts, axis=-1)


class _ScanMode(NamedTuple):
    prompt: bool
    groups: int
    rep: int
    r2: int
    tb: int
    pipelined: bool = True


def _scan_gather(mode, src_ref, fwd_ref, bwd_ref, s):
    sb = mode.tb - 1 - s
    if mode.prompt:
        for d, (ref, tt) in enumerate(((fwd_ref, s), (bwd_ref, sb))):
            z = ref[pl.ds(pl.multiple_of(tt * BATCH, BATCH), BATCH), :]
            for hp in range(PAIRS):
                r0 = d * HEAD + hp * BATCH
                src_ref[r0:r0 + BATCH, :] = z[:, hp * LANES:(hp + 1) * LANES]
        return src_ref[...].T
    unit = 4 * mode.r2
    for q in range(mode.rep):
        for d, (ref, tt) in enumerate(((fwd_ref, s), (bwd_ref, sb))):
            for b in range(DEC_BATCH):
                r0 = q * unit + (2 * d + b) * mode.r2
                src_ref[r0:r0 + mode.r2, 0:HEAD] = ref[b, tt]
    return src_ref[...].T[0:HEAD]


def _scan_value_slab(mode, vt):
    vsub = HEAD // mode.rep
    out = vt[0:vsub]
    if mode.rep > 1:
        lane = lax.broadcasted_iota(jnp.int32, (vsub, LANES), 1)
        for q in range(1, mode.rep):
            out = jnp.where(lane >= q * (LANES // mode.rep), vt[q * vsub:(q + 1) * vsub], out)
    return out


def _scan_scatter(mode, src_ref, ys_ref, yf_ref, yb_ref, s):
    sb = mode.tb - 1 - s
    if mode.prompt:
        tr = jnp.concatenate([ys_ref[s, 0], ys_ref[s, 1]], axis=0).T
        for d, (ref, tt) in enumerate(((yf_ref, s), (yb_ref, sb))):
            row0 = pl.multiple_of(tt * BATCH, BATCH)
            for hp in range(PAIRS):
                r0 = d * HEAD + hp * BATCH
                ref[pl.ds(row0, BATCH), hp * LANES:(hp + 1) * LANES] = tr[r0:r0 + BATCH]
        return
    vsub = HEAD // mode.rep
    y = ys_ref[s, 0]
    for q in range(mode.rep):
        src_ref[q * vsub:(q + 1) * vsub, :] = y
    tr = src_ref[...].T
    unit = 4 * mode.r2
    out = tr[0:unit]
    lane = lax.broadcasted_iota(jnp.int32, (unit, LANES), 1)
    for q in range(1, mode.rep):
        out = jnp.where(lane >= q * vsub, tr[q * unit:(q + 1) * unit], out)
    for d, (ref, tt) in enumerate(((yf_ref, s), (yb_ref, sb))):
        for b in range(DEC_BATCH):
            r0 = (2 * d + b) * mode.r2
            ref[b, tt] = out[r0:r0 + mode.r2, 0:HEAD]


def _n_acc(vsub):
    return 1 if vsub == HEAD else 2


def _scan_init(mode, t_axis, s_ref, s0_ref, src_ref):
    @pl.when(pl.program_id(t_axis) == 0)
    def _():
        if s0_ref is None:
            s_ref[...] = jnp.zeros_like(s_ref)
        else:
            s_ref[...] = s0_ref[...]

    if not mode.prompt:
        src_ref[...] = jnp.zeros_like(src_ref)


def _hgrn_scan_kernel(*refs, mode, has_s0, t_axis):
    qf, qb, xf, xb, vf, vb, lb_ref = refs[:7]
    refs = refs[7:]
    s0_ref = None
    if has_s0:
        s0_ref, refs = refs[0], refs[1:]
    of_ref, ob_ref, s_ref, src_ref, q_t, f_t, k_t, v_t, ys_ref = refs
    vsub = HEAD // mode.rep
    n_acc = _n_acc(vsub)
    _scan_init(mode, t_axis, s_ref, s0_ref, src_ref)
    lb = lb_ref[...]

    def gather(s):
        q_t[s] = _scan_gather(mode, src_ref, qf, qb, s)
        x = _scan_gather(mode, src_ref, xf, xb, s)
        f_t[s] = lb + (1.0 - lb) * jax.nn.sigmoid(x)
        k_t[s] = (1.0 - lb) * jax.nn.sigmoid(-x)
        v_t[s] = _scan_gather(mode, src_ref, vf, vb, s)

    def step(s):
        for g in range(mode.groups):
            r0 = g * HEAD
            vv = _scan_value_slab(mode, v_t[s, r0:r0 + HEAD, :])
            acc = [jnp.zeros((vsub, LANES), F32) for _ in range(n_acc)]
            for d in range(HEAD):
                r = r0 + d
                sn = s_ref[g, d] * f_t[s, r:r + 1, :] + vv * k_t[s, r:r + 1, :]
                s_ref[g, d] = sn
                acc[d % n_acc] = acc[d % n_acc] + sn * q_t[s, r:r + 1, :]
            ys_ref[s, g] = sum(acc[1:], acc[0])

    def scatter(s):
        _scan_scatter(mode, src_ref, ys_ref, of_ref, ob_ref, s)

    _scan_run(mode, ys_ref, gather, step, scatter)


def _scan_run(mode, ys_ref, gather, step, scatter):
    if not mode.pipelined:
        for phase in (gather, step, scatter):
            lax.fori_loop(0, mode.tb, lambda s, carry, phase=phase: (phase(s), carry)[1], 0)
        return
    last = mode.tb - 1
    gather(0)
    ys_ref[0] = jnp.zeros(ys_ref.shape[1:], F32)

    def body(s, carry):
        gather(jnp.minimum(s + 1, last))
        scatter(jnp.maximum(s - 1, 0))
        step(s)
        return carry

    lax.fori_loop(0, mode.tb, body, 0)
    scatter(last)


def _rwkv_scan_kernel(*refs, mode, has_s0, t_axis):
    ins = refs[:12]
    refs = refs[12:]
    s0_ref = None
    if has_s0:
        s0_ref, refs = refs[0], refs[1:]
    yf_ref, yb_ref, s_ref, src_ref, r_t, w_t, k_t, v_t, kk_t, a_t, ys_ref = refs
    vsub = HEAD // mode.rep
    n_acc = _n_acc(vsub)
    _scan_init(mode, t_axis, s_ref, s0_ref, src_ref)

    def gather(s):
        for i, x_t in enumerate((r_t, w_t, k_t, v_t, kk_t, a_t)):
            x_t[s] = _scan_gather(mode, src_ref, ins[2 * i], ins[2 * i + 1], s)

    def step(s):
        for g in range(mode.groups):
            r0 = g * HEAD
            acc = [jnp.zeros((vsub, LANES), F32) for _ in range(n_acc)]
            for d in range(HEAD):
                acc[d % n_acc] = acc[d % n_acc] + s_ref[g, d] * kk_t[s, r0 + d:r0 + d + 1, :]
            sa = -sum(acc[1:], acc[0])
            vv = _scan_value_slab(mode, v_t[s, r0:r0 + HEAD, :])
            acc = [jnp.zeros((vsub, LANES), F32) for _ in range(n_acc)]
            for d in range(HEAD):
                r = r0 + d
                b = kk_t[s, r:r + 1, :] * a_t[s, r:r + 1, :]
                sn = s_ref[g, d] * w_t[s, r:r + 1, :] + sa * b + vv * k_t[s, r:r + 1, :]
                s_ref[g, d] = sn
                acc[d % n_acc] = acc[d % n_acc] + sn * r_t[s, r:r + 1, :]
            ys_ref[s, g] = sum(acc[1:], acc[0])

    def scatter(s):
        _scan_scatter(mode, src_ref, ys_ref, yf_ref, yb_ref, s)

    _scan_run(mode, ys_ref, gather, step, scatter)


def _scan_scratch(mode, n_streams):
    rows = mode.groups * HEAD
    vsub = HEAD // mode.rep
    return ([pltpu.VMEM((LANES, LANES), F32)]
            + [pltpu.VMEM((mode.tb, rows, LANES), F32) for _ in range(n_streams)]
            + [pltpu.VMEM((mode.tb, mode.groups, vsub, LANES), F32)])


def _hgrn_scans(slabs, lb, state0):
    mode = _ScanMode(prompt=True, groups=2, rep=1, r2=0, tb=16, pipelined=False)
    n_t = SEQ // mode.tb
    rows = mode.tb * BATCH
    width = PAIRS * LANES
    assert width == H_A * HEAD

    def spec(p, rev):
        if rev:
            return pl.BlockSpec((None, rows, width), lambda t: (p, n_t - 1 - t, 0))
        return pl.BlockSpec((None, rows, width), lambda t: (p, t, 0))

    def ospec(rev):
        if rev:
            return pl.BlockSpec((rows, width), lambda t: (n_t - 1 - t, 0))
        return pl.BlockSpec((rows, width), lambda t: (t, 0))

    lb4 = lb.reshape(2, PAIRS, 2, HEAD)
    lb_p = jnp.broadcast_to(jnp.transpose(lb4, (2, 3, 0, 1))[..., None], (2, HEAD, 2, PAIRS, BATCH))
    lb_p = lb_p.reshape(2 * HEAD, LANES)
    st_spec = pl.BlockSpec((2, HEAD, HEAD, LANES), lambda t: (0, 0, 0, 0))
    o_shape = jax.ShapeDtypeStruct((N_PROMPT, width), F32)
    view = slabs
    o_fp, o_bp, s_fin = pl.pallas_call(
        functools.partial(_hgrn_scan_kernel, mode=mode, has_s0=False, t_axis=0),
        out_shape=(o_shape, o_shape, jax.ShapeDtypeStruct((2, HEAD, HEAD, LANES), F32)),
        grid=(n_t,),
        in_specs=[spec(0, False), spec(0, True), spec(1, False), spec(2, True), spec(3, False), spec(3, True),
                  pl.BlockSpec((2 * HEAD, LANES), lambda t: (0, 0))],
        out_specs=(ospec(False), ospec(True), st_spec),
        scratch_shapes=_scan_scratch(mode, 4),
        compiler_params=_cparams(("arbitrary",)),
        name="hgrn_scan_prompt",
    )(view, view, view, view, view, view, lb_p)
    s_fin = jnp.transpose(s_fin.reshape(2, HEAD, HEAD, 2, PAIRS, BATCH), (5, 3, 4, 0, 1, 2))
    s_fin = s_fin.reshape(BATCH, 2, H_A, DK_A, DV_A)

    mode = _ScanMode(prompt=False, groups=1, rep=4, r2=H_A, tb=32)
    n_t = DEC_SEQ // mode.tb
    view = slabs[:, N_PROMPT:].reshape(5, DEC_BATCH, DEC_SEQ, H_A, HEAD)

    def spec(p, rev):
        if rev:
            return pl.BlockSpec((None, DEC_BATCH, mode.tb, H_A, HEAD), lambda t: (p, 0, n_t - 1 - t, 0, 0))
        return pl.BlockSpec((None, DEC_BATCH, mode.tb, H_A, HEAD), lambda t: (p, 0, t, 0, 0))

    def ospec(rev):
        if rev:
            return pl.BlockSpec((DEC_BATCH, mode.tb, H_A, HEAD), lambda t: (0, n_t - 1 - t, 0, 0))
        return pl.BlockSpec((DEC_BATCH, mode.tb, H_A, HEAD), lambda t: (0, t, 0, 0))

    vsub = HEAD // mode.rep
    lb_s = jnp.transpose(lb.reshape(2, H_A, HEAD), (2, 0, 1))
    lb_s = jnp.broadcast_to(lb_s[:, None, :, None, :], (HEAD, mode.rep, 2, DEC_BATCH, H_A)).reshape(HEAD, LANES)
    s0 = state0.reshape(DEC_BATCH, 2, H_A, DK_A, mode.rep, vsub)
    s0 = jnp.transpose(s0, (3, 5, 4, 1, 0, 2)).reshape(1, HEAD, vsub, LANES)
    st_spec = pl.BlockSpec((1, HEAD, vsub, LANES), lambda t: (0, 0, 0, 0))
    o_shape = jax.ShapeDtypeStruct((DEC_BATCH, DEC_SEQ, H_A, HEAD), F32)
    o_fl, o_bl, _ = pl.pallas_call(
        functools.partial(_hgrn_scan_kernel, mode=mode, has_s0=True, t_axis=0),
        out_shape=(o_shape, o_shape, jax.ShapeDtypeStruct((1, HEAD, vsub, LANES), F32)),
        grid=(n_t,),
        in_specs=[spec(0, False), spec(0, True), spec(1, False), spec(2, True), spec(3, False), spec(3, True),
                  pl.BlockSpec((HEAD, LANES), lambda t: (0, 0)), st_spec],
        out_specs=(ospec(False), ospec(True), st_spec),
        scratch_shapes=_scan_scratch(mode, 4),
        compiler_params=_cparams(("arbitrary",)),
        name="hgrn_scan_latent",
    )(view, view, view, view, view, view, lb_s, s0)
    return o_fp, o_bp, o_fl.reshape(N_SAMPLE, width), o_bl.reshape(N_SAMPLE, width), s_fin


def _rwkv_scans(r, w_f, w_b, k, v, kk, a, state0):
    streams = ((r, r), (w_f, w_b), (k, k), (v, v), (kk, kk), (a, a))
    mode = _ScanMode(prompt=True, groups=2, rep=1, r2=0, tb=16)
    n_t = SEQ // mode.tb
    rows = mode.tb * BATCH
    width = PAIRS * LANES
    n_half = D_MODEL // width

    fwd = pl.BlockSpec((rows, width), lambda h, t: (t, h))
    bwd = pl.BlockSpec((rows, width), lambda h, t: (n_t - 1 - t, h))
    st_spec = pl.BlockSpec((None, 2, HEAD, HEAD, LANES), lambda h, t: (h, 0, 0, 0, 0))
    y_shape = jax.ShapeDtypeStruct((N_PROMPT, D_MODEL), F32)
    args = []
    for x_f, x_b in streams:
        args += [x_f, x_b]
    y_fp, y_bp, s_fin = pl.pallas_call(
        functools.partial(_rwkv_scan_kernel, mode=mode, has_s0=False, t_axis=1),
        out_shape=(y_shape, y_shape, jax.ShapeDtypeStruct((n_half, 2, HEAD, HEAD, LANES), F32)),
        grid=(n_half, n_t),
        in_specs=[fwd, bwd] * 6,
        out_specs=(fwd, bwd, st_spec),
        scratch_shapes=_scan_scratch(mode, 6),
        compiler_params=_cparams(("parallel", "arbitrary")),
        name="rwkv_scan_prompt",
    )(*args)
    s_fin = s_fin.reshape(n_half, 2, HEAD, HEAD, 2, PAIRS, BATCH)
    s_fin = jnp.transpose(s_fin, (6, 4, 0, 5, 1, 3, 2)).reshape(BATCH, 2, H_C, N_C, N_C)

    mode = _ScanMode(prompt=False, groups=1, rep=2, r2=H_C, tb=32)
    n_t = DEC_SEQ // mode.tb

    def view(x):
        return x[N_PROMPT:].reshape(DEC_BATCH, DEC_SEQ, H_C, HEAD)

    fwd = pl.BlockSpec((DEC_BATCH, mode.tb, H_C, HEAD), lambda t: (0, t, 0, 0))
    bwd = pl.BlockSpec((DEC_BATCH, mode.tb, H_C, HEAD), lambda t: (0, n_t - 1 - t, 0, 0))
    vsub = HEAD // mode.rep
    s0 = state0.reshape(DEC_BATCH, 2, H_C, mode.rep, vsub, N_C)
    s0 = jnp.transpose(s0, (5, 4, 3, 1, 0, 2)).reshape(1, HEAD, vsub, LANES)
    st_spec = pl.BlockSpec((1, HEAD, vsub, LANES), lambda t: (0, 0, 0, 0))
    y_shape = jax.ShapeDtypeStruct((DEC_BATCH, DEC_SEQ, H_C, HEAD), F32)
    args = []
    for x_f, x_b in streams:
        x_fv = view(x_f)
        args += [x_fv, x_fv if x_b is x_f else view(x_b)]
    y_fl, y_bl, _ = pl.pallas_call(
        functools.partial(_rwkv_scan_kernel, mode=mode, has_s0=True, t_axis=0),
        out_shape=(y_shape, y_shape, jax.ShapeDtypeStruct((1, HEAD, vsub, LANES), F32)),
        grid=(n_t,),
        in_specs=[fwd, bwd] * 6 + [st_spec],
        out_specs=(fwd, bwd, st_spec),
        scratch_shapes=_scan_scratch(mode, 6),
        compiler_params=_cparams(("arbitrary",)),
        name="rwkv_scan_latent",
    )(*args, s0)
    return y_fp, y_bp, y_fl.reshape(N_SAMPLE, D_MODEL), y_bl.reshape(N_SAMPLE, D_MODEL), s_fin


def _prompt_spec(width):
    return pl.BlockSpec((ROW_TILE, width), lambda i: (jnp.minimum(i, BATCH - 1), 0))


def _latent_spec(width):
    return pl.BlockSpec((ROW_TILE, width), lambda i: (jnp.maximum(i - BATCH, 0), 0))


def _hgrn_post_kernel(ofp_ref, obp_ref, ofl_ref, obl_ref, g_ref, gain_ref, bo_ref, o_ref):
    is_prompt = pl.program_id(0) < BATCH
    o = jnp.where(is_prompt, ofp_ref[...] + obp_ref[...], ofl_ref[...] + obl_ref[...])
    ms = _head_sum(o * o, bo_ref[...]) * (1.0 / DV_A)
    y = o * lax.rsqrt(ms + EPS) * gain_ref[...]
    g = g_ref[...]
    o_ref[...] = (y * (g * jax.nn.sigmoid(g))).astype(o_ref.dtype)


def _hgrn_post(o_fp, o_bp, o_fl, o_bl, slabs, gain):
    w = H_A * DV_A
    return pl.pallas_call(
        _hgrn_post_kernel,
        out_shape=jax.ShapeDtypeStruct((N_TOK, w), BF16),
        grid=(N_TILES,),
        in_specs=[_prompt_spec(w), _prompt_spec(w), _latent_spec(w), _latent_spec(w),
                  pl.BlockSpec((None, ROW_TILE, w), lambda i: (4, i, 0)),
                  _vec_spec(w), pl.BlockSpec((LANES, LANES), lambda i: (0, 0))],
        out_specs=_row_spec(w),
        compiler_params=_cparams(("parallel",)),
        name="hgrn_post",
    )(o_fp, o_bp, o_fl, o_bl, slabs, jnp.tile(gain, H_A).reshape(1, w), _block_ones())


def _q_up_kernel(cq_ref, gain_ref, w_ref, cos_ref, sin_ref, qn_ref, qr_ref):
    cq = _rms(cq_ref[...], gain_ref[...]).astype(BF16)
    q = jnp.dot(cq, w_ref[...], preferred_element_type=F32)
    qn_ref[...] = q[:, :512].astype(BF16)
    qr_ref[...] = (q[:, 512:768] * cos_ref[...] + q[:, 768:1024] * sin_ref[...]).astype(BF16)


def _q_up(tail, gain, w, cos, sin):
    return pl.pallas_call(
        _q_up_kernel,
        out_shape=(jax.ShapeDtypeStruct((N_TOK, 512), BF16), jax.ShapeDtypeStruct((N_TOK, 256), BF16)),
        grid=(N_TILES,),
        in_specs=[_row_spec(Q_RANK, T_CQ // Q_RANK),
                  _vec_spec(Q_RANK), pl.BlockSpec((Q_RANK, 1024), lambda i: (0, 0)),
                  _row_spec(256), _row_spec(256)],
        out_specs=(_row_spec(512), _row_spec(256)),
        compiler_params=_cparams(("parallel",)),
        name="q_up",
    )(tail, gain.reshape(1, Q_RANK), w, cos, sin)


def _kv_up_kernel(ckv_ref, kr_ref, krot_ref, gain_ref, w_ref, cos_ref, sin_ref, ckvn_ref, kn_ref, v_ref, krope_ref):
    ckv = _rms(ckv_ref[...], gain_ref[...])
    ckvn_ref[...] = ckv
    kv = jnp.dot(ckv.astype(BF16), w_ref[...], preferred_element_type=F32)
    kn_ref[...] = kv[:, :512].astype(BF16)
    v_ref[...] = kv[:, 512:].astype(BF16)
    krope_ref[...] = (kr_ref[...] * cos_ref[...] + krot_ref[...] * sin_ref[...]).astype(BF16)


def _kv_up(tail, gain, w, cos4, sin4):
    return pl.pallas_call(
        _kv_up_kernel,
        out_shape=(jax.ShapeDtypeStruct((N_TOK, KV_RANK), F32), jax.ShapeDtypeStruct((N_TOK, 512), BF16),
                   jax.ShapeDtypeStruct((N_TOK, 512), BF16), jax.ShapeDtypeStruct((N_TOK, LANES), BF16)),
        grid=(N_TILES,),
        in_specs=[_row_spec(LANES, T_CKV // LANES), _row_spec(LANES, T_KR // LANES), _row_spec(LANES, T_KROT // LANES),
                  _vec_spec(KV_RANK), pl.BlockSpec((KV_RANK, 1024), lambda i: (0, 0)),
                  _row_spec(LANES), _row_spec(LANES)],
        out_specs=(_row_spec(KV_RANK), _row_spec(512), _row_spec(512), _row_spec(LANES)),
        compiler_params=_cparams(("parallel",)),
        name="kv_up",
    )(tail, tail, tail, gain.reshape(1, KV_RANK), w, cos4, sin4)


def _attn_kernel(qn_ref, qr_ref, kn_ref, kr_ref, v_ref, o_ref, *, scale):
    hp = pl.program_id(1)
    qn = qn_ref[0]
    qr = qr_ref[0]
    kcat = jnp.concatenate([kn_ref[0], kr_ref[0]], axis=-1)
    v = v_ref[0]
    lane = lax.broadcasted_iota(jnp.int32, (1, LANES), 1)
    zero = jnp.zeros((), BF16)
    outs = []
    for j in range(2):
        mn = (lane // D_NOPE) == j
        mr = (lane // D_ROPE) == (2 * hp + j) % 4
        qcat = jnp.concatenate([jnp.where(mn, qn, zero), jnp.where(mr, qr, zero)], axis=-1)
        s = lax.dot_general(qcat, kcat, (((1,), (1,)), ((), ())), preferred_element_type=F32) * scale
        p = jnp.exp(s - jnp.max(s, axis=-1, keepdims=True))
        l = jnp.sum(p, axis=-1, keepdims=True)
        outs.append(jnp.dot(p.astype(BF16), v, preferred_element_type=F32) / l)
    o_ref[0] = jnp.where(lane < DV_B, outs[0], outs[1]).astype(o_ref.dtype)


def _attention(qn, qr, kn, kr, v, bsz, time_major, tq=256):
    t_len, tk = qn.shape[1], kn.shape[1]

    def spec(is_query, per_seq, col):
        rows = tq if is_query else tk
        if time_major:
            return pl.BlockSpec((1, rows, LANES), lambda b, h, q: (0, q if is_query else 0, b * per_seq + col(h)))
        return pl.BlockSpec((1, rows, LANES), lambda b, h, q: (b, q if is_query else 0, col(h)))

    return pl.pallas_call(
        functools.partial(_attn_kernel, scale=float((D_NOPE + D_ROPE) ** -0.5)),
        out_shape=jax.ShapeDtypeStruct(qn.shape, BF16),
        grid=(bsz, 4, t_len // tq),
        in_specs=[spec(True, 4, lambda h: h), spec(True, 2, lambda h: h // 2),
                  spec(False, 4, lambda h: h), spec(False, 1, lambda h: 0), spec(False, 4, lambda h: h)],
        out_specs=spec(True, 4, lambda h: h),
        compiler_params=_cparams(("parallel", "parallel", "parallel")),
        name="attention",
    )(qn, qr, kn, kr, v)


def _odd_pre_kernel(x_ref, xp_ref, xn_ref, g_ref, sc_ref, sh_ref, mu_ref, *o_refs):
    i = pl.program_id(0)
    g, sc, sh = g_ref[...], sc_ref[0], sh_ref[0]

    def nm(x):
        return _rms(x, g) * (1.0 + sc) + sh

    h = nm(x_ref[...])
    hp = nm(xp_ref[...])
    hn = nm(xn_ref[...])
    is_prompt = i < BATCH
    hp_p = jnp.where(i > 0, hp, 0.0)
    hn_p = jnp.where(i < BATCH - 1, hn, 0.0)
    prev_p = jnp.concatenate([hp_p, h[:ROW_TILE - BATCH]], axis=0)
    next_p = jnp.concatenate([h[BATCH:], hn_p], axis=0)
    part = (i - BATCH) % 4
    hp_s = jnp.where(part != 0, hp[HALO - 1:HALO], 0.0)
    hn_s = jnp.where(part != 3, hn[0:1], 0.0)
    row = lax.broadcasted_iota(jnp.int32, (ROW_TILE, 1), 0)
    prev_s = jnp.where(row == 0, hp_s, pltpu.roll(h, 1, axis=0))
    next_s = jnp.where(row == ROW_TILE - 1, hn_s, pltpu.roll(h, ROW_TILE - 1, axis=0))
    prev = jnp.where(is_prompt, prev_p, prev_s)
    nxt = jnp.where(is_prompt, next_p, next_s)
    xx = 0.5 * (prev + nxt) - h
    for j, o_ref in enumerate(o_refs):
        o_ref[...] = (h + xx * mu_ref[j:j + 1, :]).astype(o_ref.dtype)


def _odd_pre(x, g, sc, sh, mu):
    assert HALO == BATCH
    per = ROW_TILE // HALO
    last = N_TOK // HALO - 1
    return pl.pallas_call(
        _odd_pre_kernel,
        out_shape=tuple(jax.ShapeDtypeStruct((N_TOK, D_MODEL), BF16) for _ in range(6)),
        grid=(N_TILES,),
        in_specs=[_row_spec(),
                  pl.BlockSpec((HALO, D_MODEL), lambda i: (jnp.maximum(i * per - 1, 0), 0)),
                  pl.BlockSpec((HALO, D_MODEL), lambda i: (jnp.minimum((i + 1) * per, last), 0)),
                  _vec_spec(), _tile_spec(), _tile_spec(),
                  pl.BlockSpec((8, D_MODEL), lambda i: (0, 0))],
        out_specs=tuple(_row_spec() for _ in range(6)),
        compiler_params=_cparams(("parallel",)),
        name="odd_pre",
    )(x, x, x, g.reshape(1, D_MODEL), sc, sh, jnp.pad(mu, ((0, 2), (0, 0))))


def _rwkv_prep_kernel(k_ref, a_ref, kk_w_ref, ka_w_ref, bo_ref, kk_ref, k2_ref):
    k = k_ref[...]
    kk = k * kk_w_ref[...]
    kk_ref[...] = kk * lax.rsqrt(_head_sum(kk * kk, bo_ref[...]) + 1e-12)
    k2_ref[...] = k * (1.0 + (a_ref[...] - 1.0) * ka_w_ref[...])


def _rwkv_prep(k, a, k_k, k_a):
    return pl.pallas_call(
        _rwkv_prep_kernel,
        out_shape=(jax.ShapeDtypeStruct((N_TOK, D_MODEL), F32), jax.ShapeDtypeStruct((N_TOK, D_MODEL), F32)),
        grid=(N_TILES,),
        in_specs=[_row_spec(), _row_spec(), _vec_spec(), _vec_spec(),
                  pl.BlockSpec((LANES, LANES), lambda i: (0, 0))],
        out_specs=(_row_spec(), _row_spec()),
        compiler_params=_cparams(("parallel",)),
        name="rwkv_prep",
    )(k, a, k_k.reshape(1, D_MODEL), k_a.reshape(1, D_MODEL), _block_ones())


def _rwkv_post_kernel(yfp_ref, ybp_ref, yfl_ref, ybl_ref, r_ref, k_ref, v_ref, g_ref, rk_ref, lnw_ref, lnb_ref,
                      bo_ref, o_ref):
    bo = bo_ref[...]
    y = jnp.where(pl.program_id(0) < BATCH, yfp_ref[...] + ybp_ref[...], yfl_ref[...] + ybl_ref[...])
    d = y - _head_sum(y, bo) * (1.0 / N_C)
    var = _head_sum(d * d, bo) * (1.0 / N_C)
    yn = d * lax.rsqrt(var + GN_EPS) * lnw_ref[...] + lnb_ref[...]
    bonus = _head_sum(r_ref[...] * k_ref[...] * rk_ref[...], bo) * v_ref[...]
    o_ref[...] = ((yn + bonus) * g_ref[...]).astype(o_ref.dtype)


def _rwkv_post(y_fp, y_bp, y_fl, y_bl, r, k2, v, g, r_k, ln_w, ln_b):
    return pl.pallas_call(
        _rwkv_post_kernel,
        out_shape=jax.ShapeDtypeStruct((N_TOK, D_MODEL), BF16),
        grid=(N_TILES,),
        in_specs=([_prompt_spec(D_MODEL)] * 2 + [_latent_spec(D_MODEL)] * 2 + [_row_spec()] * 4 + [_vec_spec()] * 3
                  + [pl.BlockSpec((LANES, LANES), lambda i: (0, 0))]),
        out_specs=_row_spec(),
        compiler_params=_cparams(("parallel",)),
        name="rwkv_post",
    )(y_fp, y_bp, y_fl, y_bl, r, k2, v, g, r_k.reshape(1, D_MODEL), ln_w.reshape(1, D_MODEL),
      ln_b.reshape(1, D_MODEL), _block_ones())


def _rot_cols(w):
    wb = w.reshape(w.shape[:-1] + (2, 2, D_ROPE // 4))
    return jnp.concatenate([-wb[..., 1:, :], wb[..., :1, :]], axis=-2).reshape(w.shape)


def _rope_tables():
    rows = DEC_SEQ // GRID_W
    row, col = np.meshgrid(np.arange(rows), np.arange(GRID_W), indexing='ij')
    row = jnp.asarray(row.reshape(-1), F32)
    col = jnp.asarray(col.reshape(-1), F32)
    n_freq = D_ROPE // 4
    inv = 1.0 / (ROPE_BASE ** (jnp.arange(n_freq, dtype=F32) / n_freq))
    ar = row[:, None] * inv
    ac = col[:, None] * inv
    ang = jnp.concatenate([ar, ar, ac, ac], axis=-1)
    cos, sin = jnp.cos(ang), jnp.sin(ang)
    cos = jnp.concatenate([jnp.ones((N_PROMPT, D_ROPE), F32), jnp.tile(cos, (DEC_BATCH, 1))], axis=0)
    sin = jnp.concatenate([jnp.zeros((N_PROMPT, D_ROPE), F32), jnp.tile(sin, (DEC_BATCH, 1))], axis=0)
    return cos, sin


def _even_mixer(j, h, lb_all, cos, sin, cache_ckv, cache_krope, state_hgrn,
                ev_w_in, hgrn_norm, mla_q_norm, mla_w_q_up, mla_kv_norm, mla_w_kv_up):
    w_in = ev_w_in[j]
    n_a = 5 * H_A * DK_A
    w_kr = w_in[:, n_a + Q_RANK + KV_RANK:]
    w_tail = jnp.concatenate([w_in[:, n_a:n_a + Q_RANK + KV_RANK], jnp.tile(w_kr, (1, 4)),
                              jnp.tile(_rot_cols(w_kr), (1, 4))], axis=1).astype(BF16)
    slabs = _mm_split(h, w_in[:, :n_a].astype(BF16), H_A * DK_A, name="hgrn_proj")
    tail = _mm(h, w_tail, F32, tm=512, tn=T_N, name="mla_proj")

    o_fp, o_bp, o_fl, o_bl, s_h = _hgrn_scans(slabs, lb_all[j], state_hgrn[:, j])
    o_a = _hgrn_post(o_fp, o_bp, o_fl, o_bl, slabs, hgrn_norm[j])

    wq = mla_w_q_up[j].reshape(Q_RANK, H_B, D_NOPE + D_ROPE)
    wq_n = wq[:, :, :D_NOPE].reshape(Q_RANK, H_B * D_NOPE)
    wq_r = wq[:, :, D_NOPE:]
    wq_aug = jnp.concatenate([wq_n, wq_r.reshape(Q_RANK, -1), _rot_cols(wq_r).reshape(Q_RANK, -1)], axis=1).astype(BF16)
    qn, qr = _q_up(tail, mla_q_norm[j], wq_aug, jnp.tile(cos, (1, H_B)), jnp.tile(sin, (1, H_B)))
    wkv = mla_w_kv_up[j].reshape(KV_RANK, H_B, D_NOPE + DV_B)
    wkv_aug = jnp.concatenate([wkv[:, :, :D_NOPE].reshape(KV_RANK, -1), wkv[:, :, D_NOPE:].reshape(KV_RANK, -1)],
                              axis=1).astype(BF16)
    ckv_n, kn, vv, kr = _kv_up(tail, mla_kv_norm[j], wkv_aug, jnp.tile(cos, (1, 4)), jnp.tile(sin, (1, 4)))
    kv_ctx = _mm(cache_ckv[:, j].reshape(DEC_BATCH * PAST_LEN, KV_RANK), wkv_aug, BF16, name="kv_ctx")
    kn_c = kv_ctx[:, :512].reshape(DEC_BATCH, PAST_LEN, 512)
    v_c = kv_ctx[:, 512:].reshape(DEC_BATCH, PAST_LEN, 512)
    kr_c = jnp.tile(cache_krope[:, j], (1, 1, 4)).astype(BF16)

    def prompt(x):
        return x[:N_PROMPT].reshape(1, SEQ, BATCH * x.shape[-1])

    def latent(x):
        return x[N_PROMPT:].reshape(DEC_BATCH, DEC_SEQ, x.shape[-1])

    o_p = _attention(prompt(qn), prompt(qr), prompt(kn), prompt(kr), prompt(vv), BATCH, True)
    o_s = _attention(latent(qn), latent(qr),
                     jnp.concatenate([kn_c, latent(kn)], axis=1),
                     jnp.concatenate([kr_c, latent(kr)], axis=1),
                     jnp.concatenate([v_c, latent(vv)], axis=1), DEC_BATCH, False)
    o_att = jnp.concatenate([o_p.reshape(N_PROMPT, 512), o_s.reshape(N_SAMPLE, 512)], axis=0)

    new_ckv = jnp.transpose(ckv_n[:N_PROMPT].reshape(SEQ, BATCH, KV_RANK), (1, 0, 2))
    new_krope = jnp.transpose(tail[:N_PROMPT, T_KR:T_KR + D_ROPE].reshape(SEQ, BATCH, D_ROPE), (1, 0, 2))
    return o_a, o_att, (s_h, new_ckv, new_krope)


def _odd_mixer(j, x, g0, sc, sh, state_rwkv, rw_mu, rw_w_r, rw_w_k, rw_w_v, rw_w0, rw_w1, rw_w2, rw_a0, rw_a1, rw_a2,
               rw_g1, rw_g2, rw_k_k, rw_k_a, rw_r_k, rw_ln_w, rw_ln_b):
    xr, xw, xk, xv, xa, xg = _odd_pre(x, g0, sc, sh, rw_mu[j])
    r = _mm(xr, rw_w_r[j].astype(BF16), name="rwkv_r")
    k = _mm(xk, rw_w_k[j].astype(BF16), name="rwkv_k")
    v = _mm(xv, rw_w_v[j].astype(BF16), name="rwkv_v")
    a = _lora(xa, rw_a1[j], rw_a2[j], rw_a0[j], _identity, jax.nn.sigmoid)
    g = _lora(xg, rw_g1[j], rw_g2[j], jnp.zeros((D_MODEL,), F32), jax.nn.sigmoid, _identity)
    w_f = _lora(xw, rw_w1[j, 0], rw_w2[j, 0], rw_w0[j, 0], jnp.tanh, _decay)
    w_b = _lora(xw, rw_w1[j, 1], rw_w2[j, 1], rw_w0[j, 1], jnp.tanh, _decay)
    kk, k2 = _rwkv_prep(k, a, rw_k_k[j], rw_k_a[j])
    y_fp, y_bp, y_fl, y_bl, s_r = _rwkv_scans(r, w_f, w_b, k2, v, kk, a, state_rwkv[:, j])
    mix = _rwkv_post(y_fp, y_bp, y_fl, y_bl, r, k2, v, g, rw_r_k[j], rw_ln_w[j], rw_ln_b[j])
    return mix, s_r


def kernel(x_prompt, x_sample, cache_ckv, cache_krope, state_hgrn, state_rwkv, c, c_ctx, ada_w, ada_b, norm_gains, ev_w_in, hgrn_lb_logits, hgrn_norm, mla_q_norm, mla_w_q_up, mla_kv_norm, mla_w_kv_up, ev_w_out, rw_mu, rw_w_r, rw_w_k, rw_w_v, rw_w_o, rw_w0, rw_w1, rw_w2, rw_a0, rw_a1, rw_a2, rw_g1, rw_g2, rw_k_k, rw_k_a, rw_r_k, rw_ln_w, rw_ln_b, ffn_w_gate, ffn_w_up, ffn_w_down, moe_router, moe_w_gate, moe_w_up, moe_w_down):
    lb_all = jnp.cumsum(jax.nn.softmax(hgrn_lb_logits.astype(F32), axis=0), axis=0)
    lb_all = lb_all - lb_all[:1]
    cos, sin = _rope_tables()

    cond8 = jnp.zeros((8, D_MODEL), F32).at[0].set(c_ctx).at[1:1 + DEC_BATCH].set(c)
    mod = _modulation(cond8, ada_w, ada_b)
    tile_row = np.array([0] * BATCH + [1 + t // 4 for t in range(N_TILES - BATCH)])
    modt = mod[:, tile_row, :].reshape(DEPTH, N_TILES, 6, 1, D_MODEL)

    x = jnp.concatenate([jnp.transpose(x_prompt, (1, 0, 2)).reshape(N_PROMPT, D_MODEL),
                         x_sample.reshape(N_SAMPLE, D_MODEL)], axis=0)
    ckv_l, krope_l, hgrn_l, rwkv_l = [], [], [], []
    for l in range(DEPTH):
        j = l // 2
        sh_m, sc_m, gt_m, sh_f, sc_f, gt_f = [modt[l, :, i] for i in range(6)]
        if l % 2 == 0:
            h = _norm_mod(x, norm_gains[l, 0], sc_m, sh_m)
            o_a, o_att, (s_h, n_ckv, n_kr) = _even_mixer(j, h, lb_all, cos, sin, cache_ckv, cache_krope, state_hgrn,
                                                        ev_w_in, hgrn_norm, mla_q_norm, mla_w_q_up, mla_kv_norm,
                                                        mla_w_kv_up)
            hgrn_l.append(s_h)
            ckv_l.append(n_ckv)
            krope_l.append(n_kr)
            x = _mm2_resid(o_a, o_att, ev_w_out[j].astype(BF16), x, norm_gains[l, 1], gt_m)
            h = _norm_mod(x, norm_gains[l, 2], sc_f, sh_f)
            hid = _swiglu(h, ffn_w_gate[j].astype(BF16), ffn_w_up[j].astype(BF16))
            x = _mm_resid(hid, ffn_w_down[j].astype(BF16), x, norm_gains[l, 3], gt_f)
        else:
            mix, s_r = _odd_mixer(j, x, norm_gains[l, 0], sc_m, sh_m, state_rwkv, rw_mu, rw_w_r, rw_w_k, rw_w_v,
                                  rw_w0, rw_w1, rw_w2, rw_a0, rw_a1, rw_a2, rw_g1, rw_g2, rw_k_k, rw_k_a, rw_r_k,
                                  rw_ln_w, rw_ln_b)
            rwkv_l.append(s_r)
            x = _mm_resid(mix, rw_w_o[j].astype(BF16), x, norm_gains[l, 1], gt_m)
            h, route = _norm_mod_router(x, norm_gains[l, 2], sc_f, sh_f, moe_router[j])
            dest, src_tok, tile_e, n_used = _moe_plan(route)
            h_sorted = _row_gather(src_tok, h, MOE_ROWS, MOE_TM)
            hid = _moe_swiglu(tile_e, n_used, h_sorted, moe_w_gate, moe_w_up, j)
            y_sorted = _moe_down(tile_e, n_used, hid, moe_w_down, j)
            x = _moe_combine(dest, y_sorted, route, x, norm_gains[l, 3], gt_f)

    y_prompt = jnp.transpose(x[:N_PROMPT].reshape(SEQ, BATCH, D_MODEL), (1, 0, 2))
    y_sample = x[N_PROMPT:].reshape(DEC_BATCH, DEC_SEQ, D_MODEL)
    return (y_prompt, y_sample, jnp.stack(ckv_l, axis=1), jnp.stack(krope_l, axis=1),
            jnp.stack(hgrn_l, axis=1), jnp.stack(rwkv_l, axis=1))
```

```python
import functools
from typing import NamedTuple

import numpy as np
import jax
import jax.numpy as jnp
from jax import lax
from jax.experimental import pallas as pl
from jax.experimental.pallas import tpu as pltpu

D_MODEL = 1024
BATCH = 16
SEQ = 256
DEPTH = 4
DEC_BATCH = 2
DEC_SEQ = 1024
PAST_LEN = 256
GRID_W = 64
H_A = 8
DK_A = 64
DV_A = 64
H_B = 8
Q_RANK = 256
KV_RANK = 128
D_NOPE = 64
D_ROPE = 32
DV_B = 64
ROPE_BASE = 10000.0
H_C = 16
N_C = 64
D_FF = 2816
N_EXPERTS = 8
EPS = 1e-6
GN_EPS = 64e-5

F32 = jnp.float32
BF16 = jnp.bfloat16
HIGHEST = lax.Precision.HIGHEST

N_PROMPT = BATCH * SEQ
N_SAMPLE = DEC_BATCH * DEC_SEQ
N_TOK = N_PROMPT + N_SAMPLE
ROW_TILE = 256
N_TILES = N_TOK // ROW_TILE
LANES = 128
HEAD = 64
VMEM_LIMIT = 48 * 1024 * 1024
HALO = 16
PAIRS = 4
MM_TM = 1024
MM_TN = 1024
FF_TN = 1408
MOE_TM = 256
MOE_ROWS = 2 * N_TOK + N_EXPERTS * MOE_TM
MOE_TILES = MOE_ROWS // MOE_TM

T_CQ, T_CKV, T_KR, T_KROT, T_N = 0, 256, 384, 512, 640


def _cparams(sem):
    return pltpu.CompilerParams(dimension_semantics=sem, vmem_limit_bytes=VMEM_LIMIT)


def _mod_kernel(c_ref, w_ref, b_ref, o_ref):
    c = c_ref[...]
    s = (c * jax.nn.sigmoid(c)).astype(BF16)
    o_ref[0] = jnp.dot(s, w_ref[0].astype(BF16), preferred_element_type=F32) + b_ref[0]


def _modulation(cond8, ada_w, ada_b):
    tn = 768
    return pl.pallas_call(
        _mod_kernel,
        out_shape=jax.ShapeDtypeStruct((DEPTH, 8, 6 * D_MODEL), F32),
        grid=(DEPTH, 6 * D_MODEL // tn),
        in_specs=[pl.BlockSpec((8, D_MODEL), lambda l, n: (0, 0)),
                  pl.BlockSpec((1, D_MODEL, tn), lambda l, n: (l, 0, n)),
                  pl.BlockSpec((1, 1, tn), lambda l, n: (l, 0, n))],
        out_specs=pl.BlockSpec((1, 8, tn), lambda l, n: (l, 0, n)),
        compiler_params=_cparams(("parallel", "parallel")),
        name="modulation",
    )(cond8, ada_w, ada_b.reshape(DEPTH, 1, 6 * D_MODEL))


def _rms(x, g):
    return x * lax.rsqrt(jnp.mean(x * x, axis=-1, keepdims=True) + EPS) * g


def _norm_mod_kernel(x_ref, g_ref, sc_ref, sh_ref, o_ref):
    h = _rms(x_ref[...], g_ref[...]) * (1.0 + sc_ref[0]) + sh_ref[0]
    o_ref[...] = h.astype(o_ref.dtype)


def _tile_spec():
    return pl.BlockSpec((1, 1, D_MODEL), lambda i: (i, 0, 0))


def _row_spec(width=D_MODEL, col=0):
    return pl.BlockSpec((ROW_TILE, width), lambda i: (i, col))


def _vec_spec(width=D_MODEL):
    return pl.BlockSpec((1, width), lambda i: (0, 0))


def _norm_mod(x, g, sc, sh):
    return pl.pallas_call(
        _norm_mod_kernel,
        out_shape=jax.ShapeDtypeStruct((N_TOK, D_MODEL), BF16),
        grid=(N_TILES,),
        in_specs=[_row_spec(), _vec_spec(), _tile_spec(), _tile_spec()],
        out_specs=_row_spec(),
        compiler_params=_cparams(("parallel",)),
        name="norm_mod",
    )(x, g.reshape(1, D_MODEL), sc, sh)


def _norm_mod_router_kernel(x_ref, g_ref, sc_ref, sh_ref, wr_ref, o_ref, route_ref):
    h = _rms(x_ref[...], g_ref[...]) * (1.0 + sc_ref[0]) + sh_ref[0]
    o_ref[...] = h
    logits = jnp.dot(h, wr_ref[...], precision=HIGHEST, preferred_element_type=F32)
    lane = lax.broadcasted_iota(jnp.int32, logits.shape, 1).astype(F32)
    neg = jnp.float32(-jnp.inf)
    lg = jnp.where(lane < N_EXPERTS, logits, neg)
    m1 = jnp.max(lg, axis=-1, keepdims=True)
    i1 = jnp.min(jnp.where(lg == m1, lane, float(LANES)), axis=-1, keepdims=True)
    lg2 = jnp.where(lane == i1, neg, lg)
    m2 = jnp.max(lg2, axis=-1, keepdims=True)
    i2 = jnp.min(jnp.where(lg2 == m2, lane, float(LANES)), axis=-1, keepdims=True)
    e = jnp.exp(m2 - m1)
    w1 = 1.0 / (1.0 + e)
    w2 = e * w1
    route_ref[...] = jnp.where(lane == 0.0, i1, jnp.where(lane == 1.0, i2, jnp.where(lane == 2.0, w1,
                                                                                    jnp.where(lane == 3.0, w2, 0.0))))


def _norm_mod_router(x, g, sc, sh, w_router):
    wr = jnp.pad(w_router, ((0, 0), (0, LANES - N_EXPERTS)))
    return pl.pallas_call(
        _norm_mod_router_kernel,
        out_shape=(jax.ShapeDtypeStruct((N_TOK, D_MODEL), F32),
                   jax.ShapeDtypeStruct((N_TOK, LANES), F32)),
        grid=(N_TILES,),
        in_specs=[_row_spec(), _vec_spec(), _tile_spec(), _tile_spec(),
                  pl.BlockSpec((D_MODEL, LANES), lambda i: (0, 0))],
        out_specs=(_row_spec(), _row_spec(LANES)),
        compiler_params=_cparams(("parallel",)),
        name="norm_mod_router",
    )(x, g.reshape(1, D_MODEL), sc, sh, wr)


def _mm_kernel(a_ref, w_ref, o_ref):
    o_ref[...] = jnp.dot(a_ref[...].astype(BF16), w_ref[...], preferred_element_type=F32).astype(o_ref.dtype)


def _mm(a, w, out_dtype=F32, tm=MM_TM, tn=MM_TN, name="mm"):
    m, k = a.shape
    n = w.shape[1]
    tm, tn = min(tm, m), min(tn, n)
    return pl.pallas_call(
        _mm_kernel,
        out_shape=jax.ShapeDtypeStruct((m, n), out_dtype),
        grid=(n // tn, m // tm),
        in_specs=[pl.BlockSpec((tm, k), lambda j, i: (i, 0)),
                  pl.BlockSpec((k, tn), lambda j, i: (0, j))],
        out_specs=pl.BlockSpec((tm, tn), lambda j, i: (i, j)),
        compiler_params=_cparams(("parallel", "parallel")),
        name=name,
    )(a, w)


def _mm_split(a, w, tn, tm=MM_TM, name="mm_split"):
    m, k = a.shape
    p = w.shape[1] // tn
    return pl.pallas_call(
        _mm_kernel,
        out_shape=jax.ShapeDtypeStruct((p, m, tn), F32),
        grid=(p, m // tm),
        in_specs=[pl.BlockSpec((tm, k), lambda j, i: (i, 0)),
                  pl.BlockSpec((k, tn), lambda j, i: (0, j))],
        out_specs=pl.BlockSpec((None, tm, tn), lambda j, i: (j, i, 0)),
        compiler_params=_cparams(("parallel", "parallel")),
        name=name,
    )(a, w)


def _mm_resid_kernel(a_ref, w_ref, x_ref, g_ref, gt_ref, o_ref):
    y = jnp.dot(a_ref[...], w_ref[...], preferred_element_type=F32)
    o_ref[...] = x_ref[...] + gt_ref[0] * _rms(y, g_ref[...])


def _mm_resid(a, w, x, g, gt):
    k = a.shape[1]
    return pl.pallas_call(
        _mm_resid_kernel,
        out_shape=jax.ShapeDtypeStruct((N_TOK, D_MODEL), F32),
        grid=(N_TILES,),
        in_specs=[_row_spec(k), pl.BlockSpec((k, D_MODEL), lambda i: (0, 0)),
                  _row_spec(), _vec_spec(), _tile_spec()],
        out_specs=_row_spec(),
        compiler_params=_cparams(("parallel",)),
        name="mm_resid",
    )(a, w, x, g.reshape(1, D_MODEL), gt)


def _mm2_resid_kernel(a1_ref, a2_ref, w_ref, x_ref, g_ref, gt_ref, o_ref):
    k1 = a1_ref.shape[1]
    y = (jnp.dot(a1_ref[...], w_ref[:k1], preferred_element_type=F32)
         + jnp.dot(a2_ref[...], w_ref[k1:], preferred_element_type=F32))
    o_ref[...] = x_ref[...] + gt_ref[0] * _rms(y, g_ref[...])


def _mm2_resid(a1, a2, w, x, g, gt):
    k1, k2 = a1.shape[1], a2.shape[1]
    return pl.pallas_call(
        _mm2_resid_kernel,
        out_shape=jax.ShapeDtypeStruct((N_TOK, D_MODEL), F32),
        grid=(N_TILES,),
        in_specs=[_row_spec(k1), _row_spec(k2), pl.BlockSpec((k1 + k2, D_MODEL), lambda i: (0, 0)),
                  _row_spec(), _vec_spec(), _tile_spec()],
        out_specs=_row_spec(),
        compiler_params=_cparams(("parallel",)),
        name="mm2_resid",
    )(a1, a2, w, x, g.reshape(1, D_MODEL), gt)


def _swiglu_kernel(a_ref, wg_ref, wu_ref, o_ref):
    a = a_ref[...]
    g = jnp.dot(a, wg_ref[...], preferred_element_type=F32)
    u = jnp.dot(a, wu_ref[...], preferred_element_type=F32)
    o_ref[...] = ((g * jax.nn.sigmoid(g)) * u).astype(o_ref.dtype)


def _swiglu(a, wg, wu, tm=512, tn=FF_TN):
    f = wg.shape[1]
    m = a.shape[0]
    return pl.pallas_call(
        _swiglu_kernel,
        out_shape=jax.ShapeDtypeStruct((m, f), BF16),
        grid=(f // tn, m // tm),
        in_specs=[pl.BlockSpec((tm, D_MODEL), lambda j, i: (i, 0)),
                  pl.BlockSpec((D_MODEL, tn), lambda j, i: (0, j)),
                  pl.BlockSpec((D_MODEL, tn), lambda j, i: (0, j))],
        out_specs=pl.BlockSpec((tm, tn), lambda j, i: (i, j)),
        compiler_params=_cparams(("parallel", "parallel")),
        name="swiglu",
    )(a, wg, wu)


def _moe_plan(route):
    e_flat = route[:, :2].astype(jnp.int32).reshape(-1)
    onehot = (e_flat[:, None] == jnp.arange(N_EXPERTS, dtype=jnp.int32)[None, :]).astype(jnp.int32)
    pos = jnp.cumsum(onehot, axis=0) - onehot
    padded = (jnp.sum(onehot, axis=0) + MOE_TM - 1) // MOE_TM * MOE_TM
    ends = jnp.cumsum(padded)
    dest = jnp.sum(onehot * ((ends - padded)[None, :] + pos), axis=1)
    token = jnp.arange(2 * N_TOK, dtype=jnp.int32) // 2
    src_tok = jnp.zeros((MOE_ROWS,), jnp.int32).at[dest].set(token)
    tile_start = jnp.arange(MOE_TILES, dtype=jnp.int32) * MOE_TM
    tile_e = jnp.sum((tile_start[:, None] >= ends[None, :]).astype(jnp.int32), axis=1)
    tile_e = jnp.minimum(tile_e, N_EXPERTS - 1)
    n_used = (ends[-1:] // MOE_TM).astype(jnp.int32)
    return dest, src_tok, tile_e, n_used


def _row_gather_kernel(idx_ref, src_ref, o_ref, rows_s):
    rows = o_ref.shape[0]
    base = pl.program_id(0) * rows

    def copy8(c, carry):
        for u in range(8):
            r = c * 8 + u
            rows_s[pl.ds(r, 1), :] = src_ref[pl.ds(idx_ref[base + r], 1), :]
        return carry

    lax.fori_loop(0, rows // 8, copy8, 0)
    o_ref[...] = rows_s[...].astype(o_ref.dtype)


def _row_gather(idx, src, n_rows, tm):
    n_src, w = src.shape
    return pl.pallas_call(
        _row_gather_kernel,
        out_shape=jax.ShapeDtypeStruct((n_rows, w), BF16),
        grid_spec=pltpu.PrefetchScalarGridSpec(
            num_scalar_prefetch=1, grid=(n_rows // tm,),
            in_specs=[pl.BlockSpec((n_src, w), lambda i, idx_ref: (0, 0), pipeline_mode=pl.Buffered(1))],
            out_specs=pl.BlockSpec((tm, w), lambda i, idx_ref: (i, 0)),
            scratch_shapes=[pltpu.VMEM((tm, w), F32)]),
        compiler_params=_cparams(("arbitrary",)),
        name="moe_gather",
    )(idx, src)


def _expert_changed(te_ref, i):
    return jnp.logical_or(i == 0, te_ref[i] != te_ref[jnp.maximum(i - 1, 0)])


def _moe_swiglu_kernel(te_ref, nu_ref, a_ref, wg_ref, wu_ref, o_ref, wg_s, wu_s):
    i = pl.program_id(1)

    @pl.when(_expert_changed(te_ref, i))
    def _():
        wg_s[...] = wg_ref[...].astype(BF16)
        wu_s[...] = wu_ref[...].astype(BF16)

    @pl.when(i < nu_ref[0])
    def _():
        a = a_ref[...]
        g = jnp.dot(a, wg_s[...], preferred_element_type=F32)
        u = jnp.dot(a, wu_s[...], preferred_element_type=F32)
        o_ref[...] = ((g * jax.nn.sigmoid(g)) * u).astype(o_ref.dtype)

    @pl.when(i >= nu_ref[0])
    def _():
        o_ref[...] = jnp.zeros_like(o_ref)


def _moe_swiglu(tile_e, n_used, a, wg, wu, layer, tn=FF_TN):
    f = wg.shape[3]
    w_spec = pl.BlockSpec((None, None, D_MODEL, tn), lambda j, i, te, nu: (layer, te[i], 0, j))
    return pl.pallas_call(
        _moe_swiglu_kernel,
        out_shape=jax.ShapeDtypeStruct((MOE_ROWS, f), BF16),
        grid_spec=pltpu.PrefetchScalarGridSpec(
            num_scalar_prefetch=2, grid=(f // tn, MOE_TILES),
            in_specs=[pl.BlockSpec((MOE_TM, D_MODEL), lambda j, i, te, nu: (i, 0)), w_spec, w_spec],
            out_specs=pl.BlockSpec((MOE_TM, tn), lambda j, i, te, nu: (i, j)),
            scratch_shapes=[pltpu.VMEM((D_MODEL, tn), BF16), pltpu.VMEM((D_MODEL, tn), BF16)]),
        compiler_params=_cparams(("parallel", "arbitrary")),
        name="moe_swiglu",
    )(tile_e, n_used, a, wg, wu)


def _moe_down_kernel(te_ref, nu_ref, h_ref, wd_ref, o_ref, wd_s):
    i = pl.program_id(0)

    @pl.when(_expert_changed(te_ref, i))
    def _():
        wd_s[...] = wd_ref[...].astype(BF16)

    @pl.when(i < nu_ref[0])
    def _():
        o_ref[...] = jnp.dot(h_ref[...], wd_s[...], preferred_element_type=F32)

    @pl.when(i >= nu_ref[0])
    def _():
        o_ref[...] = jnp.zeros_like(o_ref)


def _moe_down(tile_e, n_used, hid, wd, layer):
    f = wd.shape[2]
    return pl.pallas_call(
        _moe_down_kernel,
        out_shape=jax.ShapeDtypeStruct((MOE_ROWS, D_MODEL), F32),
        grid_spec=pltpu.PrefetchScalarGridSpec(
            num_scalar_prefetch=2, grid=(MOE_TILES,),
            in_specs=[pl.BlockSpec((MOE_TM, f), lambda i, te, nu: (i, 0)),
                      pl.BlockSpec((None, None, f, D_MODEL), lambda i, te, nu: (layer, te[i], 0, 0))],
            out_specs=pl.BlockSpec((MOE_TM, D_MODEL), lambda i, te, nu: (i, 0)),
            scratch_shapes=[pltpu.VMEM((f, D_MODEL), BF16)]),
        compiler_params=_cparams(("arbitrary",)),
        name="moe_down",
    )(tile_e, n_used, hid, wd)


def _moe_combine_kernel(dest_ref, y_ref, route_ref, x_ref, g_ref, gt_ref, o_ref, buf, sem):
    base = pl.program_id(0) * ROW_TILE

    def issue(r, carry):
        a = 2 * (base + r)
        pltpu.make_async_copy(y_ref.at[dest_ref[a]], buf.at[0, r], sem).start()
        pltpu.make_async_copy(y_ref.at[dest_ref[a + 1]], buf.at[1, r], sem).start()
        return carry

    lax.fori_loop(0, ROW_TILE, issue, 0)

    def wait(r, carry):
        pltpu.make_async_copy(y_ref.at[0], buf.at[0, 0], sem).wait()
        pltpu.make_async_copy(y_ref.at[0], buf.at[0, 0], sem).wait()
        return carry

    lax.fori_loop(0, ROW_TILE, wait, 0)
    route = route_ref[...]
    ff = route[:, 2:3] * buf[0] + route[:, 3:4] * buf[1]
    o_ref[...] = x_ref[...] + gt_ref[0] * _rms(ff, g_ref[...])


def _moe_combine(dest, y_sorted, route, x, g, gt):
    return pl.pallas_call(
        _moe_combine_kernel,
        out_shape=jax.ShapeDtypeStruct((N_TOK, D_MODEL), F32),
        grid_spec=pltpu.PrefetchScalarGridSpec(
            num_scalar_prefetch=1, grid=(N_TILES,),
            in_specs=[pl.BlockSpec(memory_space=pl.ANY),
                      pl.BlockSpec((ROW_TILE, LANES), lambda i, d: (i, 0)),
                      pl.BlockSpec((ROW_TILE, D_MODEL), lambda i, d: (i, 0)),
                      pl.BlockSpec((1, D_MODEL), lambda i, d: (0, 0)),
                      pl.BlockSpec((1, 1, D_MODEL), lambda i, d: (i, 0, 0))],
            out_specs=pl.BlockSpec((ROW_TILE, D_MODEL), lambda i, d: (i, 0)),
            scratch_shapes=[pltpu.VMEM((2, ROW_TILE, D_MODEL), F32), pltpu.SemaphoreType.DMA(())]),
        compiler_params=_cparams(("arbitrary",)),
        name="moe_combine",
    )(dest, y_sorted, route, x, g.reshape(1, D_MODEL), gt)


def _lora_kernel(x_ref, w1_ref, w2_ref, b_ref, o_ref, *, mid, out):
    t = mid(jnp.dot(x_ref[...], w1_ref[...], preferred_element_type=F32))
    o_ref[...] = out(jnp.dot(t.astype(BF16), w2_ref[...], preferred_element_type=F32) + b_ref[...])


def _lora(x, w1, w2, b, mid, out, tm=MM_TM):
    m = x.shape[0]
    r = w1.shape[1]
    return pl.pallas_call(
        functools.partial(_lora_kernel, mid=mid, out=out),
        out_shape=jax.ShapeDtypeStruct((m, D_MODEL), F32),
        grid=(m // tm,),
        in_specs=[pl.BlockSpec((tm, D_MODEL), lambda i: (i, 0)),
                  pl.BlockSpec((D_MODEL, r), lambda i: (0, 0)),
                  pl.BlockSpec((r, D_MODEL), lambda i: (0, 0)),
                  pl.BlockSpec((1, D_MODEL), lambda i: (0, 0))],
        out_specs=pl.BlockSpec((tm, D_MODEL), lambda i: (i, 0)),
        compiler_params=_cparams(("parallel",)),
        name="lora",
    )(x, w1.astype(BF16), w2.astype(BF16), b.reshape(1, D_MODEL))


def _identity(x):
    return x


def _decay(wr):
    return jnp.exp(-float(np.exp(-0.5)) * jax.nn.sigmoid(wr))


def _block_ones():
    i = np.arange(LANES)
    return jnp.asarray((i[:, None] // HEAD == i[None, :] // HEAD).astype(np.float32))


def _head_sum(x, bo):
    parts = [jnp.dot(x[:, c * LANES:(c + 1) * LANES], bo, precision=HIGHEST, preferred_element_type=F32)
             for c in range(x.shape[1] // LANES)]
    return jnp.concatenate(parts, axis=-1)


class _ScanMode(NamedTuple):
    prompt: bool
    groups: int
    rep: int
    r2: int
    tb: int
    pipelined: bool = True


def _scan_gather(mode, src_ref, fwd_ref, bwd_ref, s):
    sb = mode.tb - 1 - s
    if mode.prompt:
        for d, (ref, tt) in enumerate(((fwd_ref, s), (bwd_ref, sb))):
            z = ref[pl.ds(pl.multiple_of(tt * BATCH, BATCH), BATCH), :]
            for hp in range(PAIRS):
                r0 = d * HEAD + hp * BATCH
                src_ref[r0:r0 + BATCH, :] = z[:, hp * LANES:(hp + 1) * LANES]
        return src_ref[...].T
    unit = 4 * mode.r2
    for q in range(mode.rep):
        for d, (ref, tt) in enumerate(((fwd_ref, s), (bwd_ref, sb))):
            for b in range(DEC_BATCH):
                r0 = q * unit + (2 * d + b) * mode.r2
                src_ref[r0:r0 + mode.r2, 0:HEAD] = ref[b, tt]
    return src_ref[...].T[0:HEAD]


def _scan_value_slab(mode, vt):
    vsub = HEAD // mode.rep
    out = vt[0:vsub]
    if mode.rep > 1:
        lane = lax.broadcasted_iota(jnp.int32, (vsub, LANES), 1)
        for q in range(1, mode.rep):
            out = jnp.where(lane >= q * (LANES // mode.rep), vt[q * vsub:(q + 1) * vsub], out)
    return out


def _scan_scatter(mode, src_ref, ys_ref, yf_ref, yb_ref, s):
    sb = mode.tb - 1 - s
    if mode.prompt:
        tr = jnp.concatenate([ys_ref[s, 0], ys_ref[s, 1]], axis=0).T
        for d, (ref, tt) in enumerate(((yf_ref, s), (yb_ref, sb))):
            row0 = pl.multiple_of(tt * BATCH, BATCH)
            for hp in range(PAIRS):
                r0 = d * HEAD + hp * BATCH
                ref[pl.ds(row0, BATCH), hp * LANES:(hp + 1) * LANES] = tr[r0:r0 + BATCH]
        return
    vsub = HEAD // mode.rep
    y = ys_ref[s, 0]
    for q in range(mode.rep):
        src_ref[q * vsub:(q + 1) * vsub, :] = y
    tr = src_ref[...].T
    unit = 4 * mode.r2
    out = tr[0:unit]
    lane = lax.broadcasted_iota(jnp.int32, (unit, LANES), 1)
    for q in range(1, mode.rep):
        out = jnp.where(lane >= q * vsub, tr[q * unit:(q + 1) * unit], out)
    for d, (ref, tt) in enumerate(((yf_ref, s), (yb_ref, sb))):
        for b in range(DEC_BATCH):
            r0 = (2 * d + b) * mode.r2
            ref[b, tt] = out[r0:r0 + mode.r2, 0:HEAD]


def _n_acc(vsub):
    return 1 if vsub == HEAD else 2


def _scan_init(mode, t_axis, s_ref, s0_ref, src_ref):
    @pl.when(pl.program_id(t_axis) == 0)
    def _():
        if s0_ref is None:
            s_ref[...] = jnp.zeros_like(s_ref)
        else:
            s_ref[...] = s0_ref[...]

    if not mode.prompt:
        src_ref[...] = jnp.zeros_like(src_ref)


def _hgrn_scan_kernel(*refs, mode, has_s0, t_axis):
    qf, qb, xf, xb, vf, vb, lb_ref = refs[:7]
    refs = refs[7:]
    s0_ref = None
    if has_s0:
        s0_ref, refs = refs[0], refs[1:]
    of_ref, ob_ref, s_ref, src_ref, q_t, f_t, k_t, v_t, ys_ref = refs
    vsub = HEAD // mode.rep
    n_acc = _n_acc(vsub)
    _scan_init(mode, t_axis, s_ref, s0_ref, src_ref)
    lb = lb_ref[...]

    def gather(s):
        q_t[s] = _scan_gather(mode, src_ref, qf, qb, s)
        x = _scan_gather(mode, src_ref, xf, xb, s)
        f_t[s] = lb + (1.0 - lb) * jax.nn.sigmoid(x)
        k_t[s] = (1.0 - lb) * jax.nn.sigmoid(-x)
        v_t[s] = _scan_gather(mode, src_ref, vf, vb, s)

    def step(s):
        for g in range(mode.groups):
            r0 = g * HEAD
            vv = _scan_value_slab(mode, v_t[s, r0:r0 + HEAD, :])
            acc = [jnp.zeros((vsub, LANES), F32) for _ in range(n_acc)]
            for d in range(HEAD):
                r = r0 + d
                sn = s_ref[g, d] * f_t[s, r:r + 1, :] + vv * k_t[s, r:r + 1, :]
                s_ref[g, d] = sn
                acc[d % n_acc] = acc[d % n_acc] + sn * q_t[s, r:r + 1, :]
            ys_ref[s, g] = sum(acc[1:], acc[0])

    def scatter(s):
        _scan_scatter(mode, src_ref, ys_ref, of_ref, ob_ref, s)

    _scan_run(mode, ys_ref, gather, step, scatter)


def _scan_run(mode, ys_ref, gather, step, scatter):
    if not mode.pipelined:
        for phase in (gather, step, scatter):
            lax.fori_loop(0, mode.tb, lambda s, carry, phase=phase: (phase(s), carry)[1], 0)
        return
    last = mode.tb - 1
    gather(0)
    ys_ref[0] = jnp.zeros(ys_ref.shape[1:], F32)

    def body(s, carry):
        gather(jnp.minimum(s + 1, last))
        scatter(jnp.maximum(s - 1, 0))
        step(s)
        return carry

    lax.fori_loop(0, mode.tb, body, 0)
    scatter(last)


def _rwkv_scan_kernel(*refs, mode, has_s0, t_axis):
    ins = refs[:12]
    refs = refs[12:]
    s0_ref = None
    if has_s0:
        s0_ref, refs = refs[0], refs[1:]
    yf_ref, yb_ref, s_ref, src_ref, r_t, w_t, k_t, v_t, kk_t, a_t, ys_ref = refs
    vsub = HEAD // mode.rep
    n_acc = _n_acc(vsub)
    _scan_init(mode, t_axis, s_ref, s0_ref, src_ref)

    def gather(s):
        for i, x_t in enumerate((r_t, w_t, k_t, v_t, kk_t, a_t)):
            x_t[s] = _scan_gather(mode, src_ref, ins[2 * i], ins[2 * i + 1], s)

    def step(s):
        for g in range(mode.groups):
            r0 = g * HEAD
            acc = [jnp.zeros((vsub, LANES), F32) for _ in range(n_acc)]
            for d in range(HEAD):
                acc[d % n_acc] = acc[d % n_acc] + s_ref[g, d] * kk_t[s, r0 + d:r0 + d + 1, :]
            sa = -sum(acc[1:], acc[0])
            vv = _scan_value_slab(mode, v_t[s, r0:r0 + HEAD, :])
            acc = [jnp.zeros((vsub, LANES), F32) for _ in range(n_acc)]
            for d in range(HEAD):
                r = r0 + d
                b = kk_t[s, r:r + 1, :] * a_t[s, r:r + 1, :]
                sn = s_ref[g, d] * w_t[s, r:r + 1, :] + sa * b + vv * k_t[s, r:r + 1, :]
                s_ref[g, d] = sn
                acc[d % n_acc] = acc[d % n_acc] + sn * r_t[s, r:r + 1, :]
            ys_ref[s, g] = sum(acc[1:], acc[0])

    def scatter(s):
        _scan_scatter(mode, src_ref, ys_ref, yf_ref, yb_ref, s)

    _scan_run(mode, ys_ref, gather, step, scatter)


def _scan_scratch(mode, n_streams):
    rows = mode.groups * HEAD
    vsub = HEAD // mode.rep
    return ([pltpu.VMEM((LANES, LANES), F32)]
            + [pltpu.VMEM((mode.tb, rows, LANES), F32) for _ in range(n_streams)]
            + [pltpu.VMEM((mode.tb, mode.groups, vsub, LANES), F32)])


def _hgrn_scans(slabs, lb, state0):
    mode = _ScanMode(prompt=True, groups=2, rep=1, r2=0, tb=16, pipelined=False)
    n_t = SEQ // mode.tb
    rows = mode.tb * BATCH
    width = PAIRS * LANES
    assert width == H_A * HEAD

    def spec(p, rev):
        if rev:
            return pl.BlockSpec((None, rows, width), lambda t: (p, n_t - 1 - t, 0))
        return pl.BlockSpec((None, rows, width), lambda t: (p, t, 0))

    def ospec(rev):
        if rev:
            return pl.BlockSpec((rows, width), lambda t: (n_t - 1 - t, 0))
        return pl.BlockSpec((rows, width), lambda t: (t, 0))

    lb4 = lb.reshape(2, PAIRS, 2, HEAD)
    lb_p = jnp.broadcast_to(jnp.transpose(lb4, (2, 3, 0, 1))[..., None], (2, HEAD, 2, PAIRS, BATCH))
    lb_p = lb_p.reshape(2 * HEAD, LANES)
    st_spec = pl.BlockSpec((2, HEAD, HEAD, LANES), lambda t: (0, 0, 0, 0))
    o_shape = jax.ShapeDtypeStruct((N_PROMPT, width), F32)
    view = slabs
    o_fp, o_bp, s_fin = pl.pallas_call(
        functools.partial(_hgrn_scan_kernel, mode=mode, has_s0=False, t_axis=0),
        out_shape=(o_shape, o_shape, jax.ShapeDtypeStruct((2, HEAD, HEAD, LANES), F32)),
        grid=(n_t,),
        in_specs=[spec(0, False), spec(0, True), spec(1, False), spec(2, True), spec(3, False), spec(3, True),
                  pl.BlockSpec((2 * HEAD, LANES), lambda t: (0, 0))],
        out_specs=(ospec(False), ospec(True), st_spec),
        scratch_shapes=_scan_scratch(mode, 4),
        compiler_params=_cparams(("arbitrary",)),
        name="hgrn_scan_prompt",
    )(view, view, view, view, view, view, lb_p)
    s_fin = jnp.transpose(s_fin.reshape(2, HEAD, HEAD, 2, PAIRS, BATCH), (5, 3, 4, 0, 1, 2))
    s_fin = s_fin.reshape(BATCH, 2, H_A, DK_A, DV_A)

    mode = _ScanMode(prompt=False, groups=1, rep=4, r2=H_A, tb=32)
    n_t = DEC_SEQ // mode.tb
    view = slabs[:, N_PROMPT:].reshape(5, DEC_BATCH, DEC_SEQ, H_A, HEAD)

    def spec(p, rev):
        if rev:
            return pl.BlockSpec((None, DEC_BATCH, mode.tb, H_A, HEAD), lambda t: (p, 0, n_t - 1 - t, 0, 0))
        return pl.BlockSpec((None, DEC_BATCH, mode.tb, H_A, HEAD), lambda t: (p, 0, t, 0, 0))

    def ospec(rev):
        if rev:
            return pl.BlockSpec((DEC_BATCH, mode.tb, H_A, HEAD), lambda t: (0, n_t - 1 - t, 0, 0))
        return pl.BlockSpec((DEC_BATCH, mode.tb, H_A, HEAD), lambda t: (0, t, 0, 0))

    vsub = HEAD // mode.rep
    lb_s = jnp.transpose(lb.reshape(2, H_A, HEAD), (2, 0, 1))
    lb_s = jnp.broadcast_to(lb_s[:, None, :, None, :], (HEAD, mode.rep, 2, DEC_BATCH, H_A)).reshape(HEAD, LANES)
    s0 = state0.reshape(DEC_BATCH, 2, H_A, DK_A, mode.rep, vsub)
    s0 = jnp.transpose(s0, (3, 5, 4, 1, 0, 2)).reshape(1, HEAD, vsub, LANES)
    st_spec = pl.BlockSpec((1, HEAD, vsub, LANES), lambda t: (0, 0, 0, 0))
    o_shape = jax.ShapeDtypeStruct((DEC_BATCH, DEC_SEQ, H_A, HEAD), F32)
    o_fl, o_bl, _ = pl.pallas_call(
        functools.partial(_hgrn_scan_kernel, mode=mode, has_s0=True, t_axis=0),
        out_shape=(o_shape, o_shape, jax.ShapeDtypeStruct((1, HEAD, vsub, LANES), F32)),
        grid=(n_t,),
        in_specs=[spec(0, False), spec(0, True), spec(1, False), spec(2, True), spec(3, False), spec(3, True),
                  pl.BlockSpec((HEAD, LANES), lambda t: (0, 0)), st_spec],
        out_specs=(ospec(False), ospec(True), st_spec),
        scratch_shapes=_scan_scratch(mode, 4),
        compiler_params=_cparams(("arbitrary",)),
        name="hgrn_scan_latent",
    )(view, view, view, view, view, view, lb_s, s0)
    return o_fp, o_bp, o_fl.reshape(N_SAMPLE, width), o_bl.reshape(N_SAMPLE, width), s_fin


def _rwkv_scans(r, w_f, w_b, k, v, kk, a, state0):
    streams = ((r, r), (w_f, w_b), (k, k), (v, v), (kk, kk), (a, a))
    mode = _ScanMode(prompt=True, groups=2, rep=1, r2=0, tb=16)
    n_t = SEQ // mode.tb
    rows = mode.tb * BATCH
    width = PAIRS * LANES
    n_half = D_MODEL // width

    fwd = pl.BlockSpec((rows, width), lambda h, t: (t, h))
    bwd = pl.BlockSpec((rows, width), lambda h, t: (n_t - 1 - t, h))
    st_spec = pl.BlockSpec((None, 2, HEAD, HEAD, LANES), lambda h, t: (h, 0, 0, 0, 0))
    y_shape = jax.ShapeDtypeStruct((N_PROMPT, D_MODEL), F32)
    args = []
    for x_f, x_b in streams:
        args += [x_f, x_b]
    y_fp, y_bp, s_fin = pl.pallas_call(
        functools.partial(_rwkv_scan_kernel, mode=mode, has_s0=False, t_axis=1),
        out_shape=(y_shape, y_shape, jax.ShapeDtypeStruct((n_half, 2, HEAD, HEAD, LANES), F32)),
        grid=(n_half, n_t),
        in_specs=[fwd, bwd] * 6,
        out_specs=(fwd, bwd, st_spec),
        scratch_shapes=_scan_scratch(mode, 6),
        compiler_params=_cparams(("parallel", "arbitrary")),
        name="rwkv_scan_prompt",
    )(*args)
    s_fin = s_fin.reshape(n_half, 2, HEAD, HEAD, 2, PAIRS, BATCH)
    s_fin = jnp.transpose(s_fin, (6, 4, 0, 5, 1, 3, 2)).reshape(BATCH, 2, H_C, N_C, N_C)

    mode = _ScanMode(prompt=False, groups=1, rep=2, r2=H_C, tb=32)
    n_t = DEC_SEQ // mode.tb

    def view(x):
        return x[N_PROMPT:].reshape(DEC_BATCH, DEC_SEQ, H_C, HEAD)

    fwd = pl.BlockSpec((DEC_BATCH, mode.tb, H_C, HEAD), lambda t: (0, t, 0, 0))
    bwd = pl.BlockSpec((DEC_BATCH, mode.tb, H_C, HEAD), lambda t: (0, n_t - 1 - t, 0, 0))
    vsub = HEAD // mode.rep
    s0 = state0.reshape(DEC_BATCH, 2, H_C, mode.rep, vsub, N_C)
    s0 = jnp.transpose(s0, (5, 4, 3, 1, 0, 2)).reshape(1, HEAD, vsub, LANES)
    st_spec = pl.BlockSpec((1, HEAD, vsub, LANES), lambda t: (0, 0, 0, 0))
    y_shape = jax.ShapeDtypeStruct((DEC_BATCH, DEC_SEQ, H_C, HEAD), F32)
    args = []
    for x_f, x_b in streams:
        x_fv = view(x_f)
        args += [x_fv, x_fv if x_b is x_f else view(x_b)]
    y_fl, y_bl, _ = pl.pallas_call(
        functools.partial(_rwkv_scan_kernel, mode=mode, has_s0=True, t_axis=0),
        out_shape=(y_shape, y_shape, jax.ShapeDtypeStruct((1, HEAD, vsub, LANES), F32)),
        grid=(n_t,),
        in_specs=[fwd, bwd] * 6 + [st_spec],
        out_specs=(fwd, bwd, st_spec),
        scratch_shapes=_scan_scratch(mode, 6),
        compiler_params=_cparams(("arbitrary",)),
        name="rwkv_scan_latent",
    )(*args, s0)
    return y_fp, y_bp, y_fl.reshape(N_SAMPLE, D_MODEL), y_bl.reshape(N_SAMPLE, D_MODEL), s_fin


def _prompt_spec(width):
    return pl.BlockSpec((ROW_TILE, width), lambda i: (jnp.minimum(i, BATCH - 1), 0))


def _latent_spec(width):
    return pl.BlockSpec((ROW_TILE, width), lambda i: (jnp.maximum(i - BATCH, 0), 0))


def _hgrn_post_kernel(ofp_ref, obp_ref, ofl_ref, obl_ref, g_ref, gain_ref, bo_ref, o_ref):
    is_prompt = pl.program_id(0) < BATCH
    o = jnp.where(is_prompt, ofp_ref[...] + obp_ref[...], ofl_ref[...] + obl_ref[...])
    ms = _head_sum(o * o, bo_ref[...]) * (1.0 / DV_A)
    y = o * lax.rsqrt(ms + EPS) * gain_ref[...]
    g = g_ref[...]
    o_ref[...] = (y * (g * jax.nn.sigmoid(g))).astype(o_ref.dtype)


def _hgrn_post(o_fp, o_bp, o_fl, o_bl, slabs, gain):
    w = H_A * DV_A
    return pl.pallas_call(
        _hgrn_post_kernel,
        out_shape=jax.ShapeDtypeStruct((N_TOK, w), BF16),
        grid=(N_TILES,),
        in_specs=[_prompt_spec(w), _prompt_spec(w), _latent_spec(w), _latent_spec(w),
                  pl.BlockSpec((None, ROW_TILE, w), lambda i: (4, i, 0)),
                  _vec_spec(w), pl.BlockSpec((LANES, LANES), lambda i: (0, 0))],
        out_specs=_row_spec(w),
        compiler_params=_cparams(("parallel",)),
        name="hgrn_post",
    )(o_fp, o_bp, o_fl, o_bl, slabs, jnp.tile(gain, H_A).reshape(1, w), _block_ones())


def _q_up_kernel(cq_ref, gain_ref, w_ref, cos_ref, sin_ref, qn_ref, qr_ref):
    cq = _rms(cq_ref[...], gain_ref[...]).astype(BF16)
    q = jnp.dot(cq, w_ref[...], preferred_element_type=F32)
    qn_ref[...] = q[:, :512].astype(BF16)
    qr_ref[...] = (q[:, 512:768] * cos_ref[...] + q[:, 768:1024] * sin_ref[...]).astype(BF16)


def _q_up(tail, gain, w, cos, sin):
    return pl.pallas_call(
        _q_up_kernel,
        out_shape=(jax.ShapeDtypeStruct((N_TOK, 512), BF16), jax.ShapeDtypeStruct((N_TOK, 256), BF16)),
        grid=(N_TILES,),
        in_specs=[_row_spec(Q_RANK, T_CQ // Q_RANK),
                  _vec_spec(Q_RANK), pl.BlockSpec((Q_RANK, 1024), lambda i: (0, 0)),
                  _row_spec(256), _row_spec(256)],
        out_specs=(_row_spec(512), _row_spec(256)),
        compiler_params=_cparams(("parallel",)),
        name="q_up",
    )(tail, gain.reshape(1, Q_RANK), w, cos, sin)


def _kv_up_kernel(ckv_ref, kr_ref, krot_ref, gain_ref, w_ref, cos_ref, sin_ref, ckvn_ref, kn_ref, v_ref, krope_ref):
    ckv = _rms(ckv_ref[...], gain_ref[...])
    ckvn_ref[...] = ckv
    kv = jnp.dot(ckv.astype(BF16), w_ref[...], preferred_element_type=F32)
    kn_ref[...] = kv[:, :512].astype(BF16)
    v_ref[...] = kv[:, 512:].astype(BF16)
    krope_ref[...] = (kr_ref[...] * cos_ref[...] + krot_ref[...] * sin_ref[...]).astype(BF16)


def _kv_up(tail, gain, w, cos4, sin4):
    return pl.pallas_call(
        _kv_up_kernel,
        out_shape=(jax.ShapeDtypeStruct((N_TOK, KV_RANK), F32), jax.ShapeDtypeStruct((N_TOK, 512), BF16),
                   jax.ShapeDtypeStruct((N_TOK, 512), BF16), jax.ShapeDtypeStruct((N_TOK, LANES), BF16)),
        grid=(N_TILES,),
        in_specs=[_row_spec(LANES, T_CKV // LANES), _row_spec(LANES, T_KR // LANES), _row_spec(LANES, T_KROT // LANES),
                  _vec_spec(KV_RANK), pl.BlockSpec((KV_RANK, 1024), lambda i: (0, 0)),
                  _row_spec(LANES), _row_spec(LANES)],
        out_specs=(_row_spec(KV_RANK), _row_spec(512), _row_spec(512), _row_spec(LANES)),
        compiler_params=_cparams(("parallel",)),
        name="kv_up",
    )(tail, tail, tail, gain.reshape(1, KV_RANK), w, cos4, sin4)


def _attn_kernel(qn_ref, qr_ref, kn_ref, kr_ref, v_ref, o_ref, *, scale):
    kr = kr_ref[0]
    lane = lax.broadcasted_iota(jnp.int32, (1, LANES), 1)
    zero = jnp.zeros((), BF16)
    for hp in range(H_B // 2):
        cols = slice(hp * LANES, (hp + 1) * LANES)
        qn = qn_ref[0, :, cols]
        qr = qr_ref[0, :, (hp // 2) * LANES:(hp // 2 + 1) * LANES]
        kcat = jnp.concatenate([kn_ref[0, :, cols], kr], axis=-1)
        v = v_ref[0, :, cols]
        outs = []
        for j in range(2):
            mn = (lane // D_NOPE) == j
            mr = (lane // D_ROPE) == (2 * hp + j) % 4
            qcat = jnp.concatenate([jnp.where(mn, qn, zero), jnp.where(mr, qr, zero)], axis=-1)
            s = lax.dot_general(qcat, kcat, (((1,), (1,)), ((), ())), preferred_element_type=F32) * scale
            p = jnp.exp(s - jnp.max(s, axis=-1, keepdims=True))
            l = jnp.sum(p, axis=-1, keepdims=True)
            outs.append(jnp.dot(p.astype(BF16), v, preferred_element_type=F32) / l)
        o_ref[0, :, cols] = jnp.where(lane < DV_B, outs[0], outs[1]).astype(o_ref.dtype)


def _attention(qn, qr, kn, kr, v, bsz, time_major, tq=256):
    t_len, tk = qn.shape[1], kn.shape[1]

    def spec(is_query, width):
        rows = tq if is_query else tk
        if time_major:
            return pl.BlockSpec((1, rows, width), lambda b, q: (0, q if is_query else 0, b))
        return pl.BlockSpec((1, rows, width), lambda b, q: (b, q if is_query else 0, 0))

    w_n, w_r = H_B * D_NOPE, H_B * D_ROPE
    return pl.pallas_call(
        functools.partial(_attn_kernel, scale=float((D_NOPE + D_ROPE) ** -0.5)),
        out_shape=jax.ShapeDtypeStruct(qn.shape, BF16),
        grid=(bsz, t_len // tq),
        in_specs=[spec(True, w_n), spec(True, w_r), spec(False, w_n), spec(False, LANES), spec(False, w_n)],
        out_specs=spec(True, w_n),
        compiler_params=_cparams(("parallel", "parallel")),
        name="attention",
    )(qn, qr, kn, kr, v)


def _odd_pre_kernel(x_ref, xp_ref, xn_ref, g_ref, sc_ref, sh_ref, mu_ref, *o_refs):
    i = pl.program_id(0)
    g, sc, sh = g_ref[...], sc_ref[0], sh_ref[0]

    def nm(x):
        return _rms(x, g) * (1.0 + sc) + sh

    h = nm(x_ref[...])
    hp = nm(xp_ref[...])
    hn = nm(xn_ref[...])
    is_prompt = i < BATCH
    hp_p = jnp.where(i > 0, hp, 0.0)
    hn_p = jnp.where(i < BATCH - 1, hn, 0.0)
    prev_p = jnp.concatenate([hp_p, h[:ROW_TILE - BATCH]], axis=0)
    next_p = jnp.concatenate([h[BATCH:], hn_p], axis=0)
    part = (i - BATCH) % 4
    hp_s = jnp.where(part != 0, hp[HALO - 1:HALO], 0.0)
    hn_s = jnp.where(part != 3, hn[0:1], 0.0)
    row = lax.broadcasted_iota(jnp.int32, (ROW_TILE, 1), 0)
    prev_s = jnp.where(row == 0, hp_s, pltpu.roll(h, 1, axis=0))
    next_s = jnp.where(row == ROW_TILE - 1, hn_s, pltpu.roll(h, ROW_TILE - 1, axis=0))
    prev = jnp.where(is_prompt, prev_p, prev_s)
    nxt = jnp.where(is_prompt, next_p, next_s)
    xx = 0.5 * (prev + nxt) - h
    for j, o_ref in enumerate(o_refs):
        o_ref[...] = (h + xx * mu_ref[j:j + 1, :]).astype(o_ref.dtype)


def _odd_pre(x, g, sc, sh, mu):
    assert HALO == BATCH
    per = ROW_TILE // HALO
    last = N_TOK // HALO - 1
    return pl.pallas_call(
        _odd_pre_kernel,
        out_shape=tuple(jax.ShapeDtypeStruct((N_TOK, D_MODEL), BF16) for _ in range(6)),
        grid=(N_TILES,),
        in_specs=[_row_spec(),
                  pl.BlockSpec((HALO, D_MODEL), lambda i: (jnp.maximum(i * per - 1, 0), 0)),
                  pl.BlockSpec((HALO, D_MODEL), lambda i: (jnp.minimum((i + 1) * per, last), 0)),
                  _vec_spec(), _tile_spec(), _tile_spec(),
                  pl.BlockSpec((8, D_MODEL), lambda i: (0, 0))],
        out_specs=tuple(_row_spec() for _ in range(6)),
        compiler_params=_cparams(("parallel",)),
        name="odd_pre",
    )(x, x, x, g.reshape(1, D_MODEL), sc, sh, jnp.pad(mu, ((0, 2), (0, 0))))


def _rwkv_prep_kernel(k_ref, a_ref, kk_w_ref, ka_w_ref, bo_ref, kk_ref, k2_ref):
    k = k_ref[...]
    kk = k * kk_w_ref[...]
    kk_ref[...] = kk * lax.rsqrt(_head_sum(kk * kk, bo_ref[...]) + 1e-12)
    k2_ref[...] = k * (1.0 + (a_ref[...] - 1.0) * ka_w_ref[...])


def _rwkv_prep(k, a, k_k, k_a):
    return pl.pallas_call(
        _rwkv_prep_kernel,
        out_shape=(jax.ShapeDtypeStruct((N_TOK, D_MODEL), F32), jax.ShapeDtypeStruct((N_TOK, D_MODEL), F32)),
        grid=(N_TILES,),
        in_specs=[_row_spec(), _row_spec(), _vec_spec(), _vec_spec(),
                  pl.BlockSpec((LANES, LANES), lambda i: (0, 0))],
        out_specs=(_row_spec(), _row_spec()),
        compiler_params=_cparams(("parallel",)),
        name="rwkv_prep",
    )(k, a, k_k.reshape(1, D_MODEL), k_a.reshape(1, D_MODEL), _block_ones())


def _rwkv_post_kernel(yfp_ref, ybp_ref, yfl_ref, ybl_ref, r_ref, k_ref, v_ref, g_ref, rk_ref, lnw_ref, lnb_ref,
                      bo_ref, o_ref):
    bo = bo_ref[...]
    y = jnp.where(pl.program_id(0) < BATCH, yfp_ref[...] + ybp_ref[...], yfl_ref[...] + ybl_ref[...])
    d = y - _head_sum(y, bo) * (1.0 / N_C)
    var = _head_sum(d * d, bo) * (1.0 / N_C)
    yn = d * lax.rsqrt(var + GN_EPS) * lnw_ref[...] + lnb_ref[...]
    bonus = _head_sum(r_ref[...] * k_ref[...] * rk_ref[...], bo) * v_ref[...]
    o_ref[...] = ((yn + bonus) * g_ref[...]).astype(o_ref.dtype)


def _rwkv_post(y_fp, y_bp, y_fl, y_bl, r, k2, v, g, r_k, ln_w, ln_b):
    return pl.pallas_call(
        _rwkv_post_kernel,
        out_shape=jax.ShapeDtypeStruct((N_TOK, D_MODEL), BF16),
        grid=(N_TILES,),
        in_specs=([_prompt_spec(D_MODEL)] * 2 + [_latent_spec(D_MODEL)] * 2 + [_row_spec()] * 4 + [_vec_spec()] * 3
                  + [pl.BlockSpec((LANES, LANES), lambda i: (0, 0))]),
        out_specs=_row_spec(),
        compiler_params=_cparams(("parallel",)),
        name="rwkv_post",
    )(y_fp, y_bp, y_fl, y_bl, r, k2, v, g, r_k.reshape(1, D_MODEL), ln_w.reshape(1, D_MODEL),
      ln_b.reshape(1, D_MODEL), _block_ones())


def _rot_cols(w):
    wb = w.reshape(w.shape[:-1] + (2, 2, D_ROPE // 4))
    return jnp.concatenate([-wb[..., 1:, :], wb[..., :1, :]], axis=-2).reshape(w.shape)


def _rope_tables():
    rows = DEC_SEQ // GRID_W
    row, col = np.meshgrid(np.arange(rows), np.arange(GRID_W), indexing='ij')
    row = jnp.asarray(row.reshape(-1), F32)
    col = jnp.asarray(col.reshape(-1), F32)
    n_freq = D_ROPE // 4
    inv = 1.0 / (ROPE_BASE ** (jnp.arange(n_freq, dtype=F32) / n_freq))
    ar = row[:, None] * inv
    ac = col[:, None] * inv
    ang = jnp.concatenate([ar, ar, ac, ac], axis=-1)
    cos, sin = jnp.cos(ang), jnp.sin(ang)
    cos = jnp.concatenate([jnp.ones((N_PROMPT, D_ROPE), F32), jnp.tile(cos, (DEC_BATCH, 1))], axis=0)
    sin = jnp.concatenate([jnp.zeros((N_PROMPT, D_ROPE), F32), jnp.tile(sin, (DEC_BATCH, 1))], axis=0)
    return cos, sin


def _even_mixer(j, h, lb_all, cos, sin, cache_ckv, cache_krope, state_hgrn,
                ev_w_in, hgrn_norm, mla_q_norm, mla_w_q_up, mla_kv_norm, mla_w_kv_up):
    w_in = ev_w_in[j]
    n_a = 5 * H_A * DK_A
    w_kr = w_in[:, n_a + Q_RANK + KV_RANK:]
    w_tail = jnp.concatenate([w_in[:, n_a:n_a + Q_RANK + KV_RANK], jnp.tile(w_kr, (1, 4)),
                              jnp.tile(_rot_cols(w_kr), (1, 4))], axis=1).astype(BF16)
    slabs = _mm_split(h, w_in[:, :n_a].astype(BF16), H_A * DK_A, name="hgrn_proj")
    tail = _mm(h, w_tail, F32, tn=T_N, name="mla_proj")

    o_fp, o_bp, o_fl, o_bl, s_h = _hgrn_scans(slabs, lb_all[j], state_hgrn[:, j])
    o_a = _hgrn_post(o_fp, o_bp, o_fl, o_bl, slabs, hgrn_norm[j])

    wq = mla_w_q_up[j].reshape(Q_RANK, H_B, D_NOPE + D_ROPE)
    wq_n = wq[:, :, :D_NOPE].reshape(Q_RANK, H_B * D_NOPE)
    wq_r = wq[:, :, D_NOPE:]
    wq_aug = jnp.concatenate([wq_n, wq_r.reshape(Q_RANK, -1), _rot_cols(wq_r).reshape(Q_RANK, -1)], axis=1).astype(BF16)
    qn, qr = _q_up(tail, mla_q_norm[j], wq_aug, jnp.tile(cos, (1, H_B)), jnp.tile(sin, (1, H_B)))
    wkv = mla_w_kv_up[j].reshape(KV_RANK, H_B, D_NOPE + DV_B)
    wkv_aug = jnp.concatenate([wkv[:, :, :D_NOPE].reshape(KV_RANK, -1), wkv[:, :, D_NOPE:].reshape(KV_RANK, -1)],
                              axis=1).astype(BF16)
    ckv_n, kn, vv, kr = _kv_up(tail, mla_kv_norm[j], wkv_aug, jnp.tile(cos, (1, 4)), jnp.tile(sin, (1, 4)))
    kv_ctx = _mm(cache_ckv[:, j].reshape(DEC_BATCH * PAST_LEN, KV_RANK), wkv_aug, BF16, name="kv_ctx")
    kn_c = kv_ctx[:, :512].reshape(DEC_BATCH, PAST_LEN, 512)
    v_c = kv_ctx[:, 512:].reshape(DEC_BATCH, PAST_LEN, 512)
    kr_c = jnp.tile(cache_krope[:, j], (1, 1, 4)).astype(BF16)

    def prompt(x):
        return x[:N_PROMPT].reshape(1, SEQ, BATCH * x.shape[-1])

    def latent(x):
        return x[N_PROMPT:].reshape(DEC_BATCH, DEC_SEQ, x.shape[-1])

    o_p = _attention(prompt(qn), prompt(qr), prompt(kn), prompt(kr), prompt(vv), BATCH, True)
    o_s = _attention(latent(qn), latent(qr),
                     jnp.concatenate([kn_c, latent(kn)], axis=1),
                     jnp.concatenate([kr_c, latent(kr)], axis=1),
                     jnp.concatenate([v_c, latent(vv)], axis=1), DEC_BATCH, False)
    o_att = jnp.concatenate([o_p.reshape(N_PROMPT, 512), o_s.reshape(N_SAMPLE, 512)], axis=0)

    new_ckv = jnp.transpose(ckv_n[:N_PROMPT].reshape(SEQ, BATCH, KV_RANK), (1, 0, 2))
    new_krope = jnp.transpose(tail[:N_PROMPT, T_KR:T_KR + D_ROPE].reshape(SEQ, BATCH, D_ROPE), (1, 0, 2))
    return o_a, o_att, (s_h, new_ckv, new_krope)


def _odd_mixer(j, x, g0, sc, sh, state_rwkv, rw_mu, rw_w_r, rw_w_k, rw_w_v, rw_w0, rw_w1, rw_w2, rw_a0, rw_a1, rw_a2,
               rw_g1, rw_g2, rw_k_k, rw_k_a, rw_r_k, rw_ln_w, rw_ln_b):
    xr, xw, xk, xv, xa, xg = _odd_pre(x, g0, sc, sh, rw_mu[j])
    r = _mm(xr, rw_w_r[j].astype(BF16), name="rwkv_r")
    k = _mm(xk, rw_w_k[j].astype(BF16), name="rwkv_k")
    v = _mm(xv, rw_w_v[j].astype(BF16), name="rwkv_v")
    a = _lora(xa, rw_a1[j], rw_a2[j], rw_a0[j], _identity, jax.nn.sigmoid)
    g = _lora(xg, rw_g1[j], rw_g2[j], jnp.zeros((D_MODEL,), F32), jax.nn.sigmoid, _identity)
    w_f = _lora(xw, rw_w1[j, 0], rw_w2[j, 0], rw_w0[j, 0], jnp.tanh, _decay)
    w_b = _lora(xw, rw_w1[j, 1], rw_w2[j, 1], rw_w0[j, 1], jnp.tanh, _decay)
    kk, k2 = _rwkv_prep(k, a, rw_k_k[j], rw_k_a[j])
    y_fp, y_bp, y_fl, y_bl, s_r = _rwkv_scans(r, w_f, w_b, k2, v, kk, a, state_rwkv[:, j])
    mix = _rwkv_post(y_fp, y_bp, y_fl, y_bl, r, k2, v, g, rw_r_k[j], rw_ln_w[j], rw_ln_b[j])
    return mix, s_r


def kernel(x_prompt, x_sample, cache_ckv, cache_krope, state_hgrn, state_rwkv, c, c_ctx, ada_w, ada_b, norm_gains, ev_w_in, hgrn_lb_logits, hgrn_norm, mla_q_norm, mla_w_q_up, mla_kv_norm, mla_w_kv_up, ev_w_out, rw_mu, rw_w_r, rw_w_k, rw_w_v, rw_w_o, rw_w0, rw_w1, rw_w2, rw_a0, rw_a1, rw_a2, rw_g1, rw_g2, rw_k_k, rw_k_a, rw_r_k, rw_ln_w, rw_ln_b, ffn_w_gate, ffn_w_up, ffn_w_down, moe_router, moe_w_gate, moe_w_up, moe_w_down):
    lb_all = jnp.cumsum(jax.nn.softmax(hgrn_lb_logits.astype(F32), axis=0), axis=0)
    lb_all = lb_all - lb_all[:1]
    cos, sin = _rope_tables()

    cond8 = jnp.zeros((8, D_MODEL), F32).at[0].set(c_ctx).at[1:1 + DEC_BATCH].set(c)
    mod = _modulation(cond8, ada_w, ada_b)
    tile_row = np.array([0] * BATCH + [1 + t // 4 for t in range(N_TILES - BATCH)])
    modt = mod[:, tile_row, :].reshape(DEPTH, N_TILES, 6, 1, D_MODEL)

    x = jnp.concatenate([jnp.transpose(x_prompt, (1, 0, 2)).reshape(N_PROMPT, D_MODEL),
                         x_sample.reshape(N_SAMPLE, D_MODEL)], axis=0)
    ckv_l, krope_l, hgrn_l, rwkv_l = [], [], [], []
    for l in range(DEPTH):
        j = l // 2
        sh_m, sc_m, gt_m, sh_f, sc_f, gt_f = [modt[l, :, i] for i in range(6)]
        if l % 2 == 0:
            h = _norm_mod(x, norm_gains[l, 0], sc_m, sh_m)
            o_a, o_att, (s_h, n_ckv, n_kr) = _even_mixer(j, h, lb_all, cos, sin, cache_ckv, cache_krope, state_hgrn,
                                                        ev_w_in, hgrn_norm, mla_q_norm, mla_w_q_up, mla_kv_norm,
                                                        mla_w_kv_up)
            hgrn_l.append(s_h)
            ckv_l.append(n_ckv)
            krope_l.append(n_kr)
            x = _mm2_resid(o_a, o_att, ev_w_out[j].astype(BF16), x, norm_gains[l, 1], gt_m)
            h = _norm_mod(x, norm_gains[l, 2], sc_f, sh_f)
            hid = _swiglu(h, ffn_w_gate[j].astype(BF16), ffn_w_up[j].astype(BF16))
            x = _mm_resid(hid, ffn_w_down[j].astype(BF16), x, norm_gains[l, 3], gt_f)
        else:
            mix, s_r = _odd_mixer(j, x, norm_gains[l, 0], sc_m, sh_m, state_rwkv, rw_mu, rw_w_r, rw_w_k, rw_w_v,
                                  rw_w0, rw_w1, rw_w2, rw_a0, rw_a1, rw_a2, rw_g1, rw_g2, rw_k_k, rw_k_a, rw_r_k,
                                  rw_ln_w, rw_ln_b)
            rwkv_l.append(s_r)
            x = _mm_resid(mix, rw_w_o[j].astype(BF16), x, norm_gains[l, 1], gt_m)
            h, route = _norm_mod_router(x, norm_gains[l, 2], sc_f, sh_f, moe_router[j])
            dest, src_tok, tile_e, n_used = _moe_plan(route)
            h_sorted = _row_gather(src_tok, h, MOE_ROWS, MOE_TM)
            hid = _moe_swiglu(tile_e, n_used, h_sorted, moe_w_gate, moe_w_up, j)
            y_sorted = _moe_down(tile_e, n_used, hid, moe_w_down, j)
            x = _moe_combine(dest, y_sorted, route, x, norm_gains[l, 3], gt_f)

    y_prompt = jnp.transpose(x[:N_PROMPT].reshape(SEQ, BATCH, D_MODEL), (1, 0, 2))
    y_sample = x[N_PROMPT:].reshape(DEC_BATCH, DEC_SEQ, D_MODEL)
    return (y_prompt, y_sample, jnp.stack(ckv_l, axis=1), jnp.stack(krope_l, axis=1),
            jnp.stack(hgrn_l, axis=1), jnp.stack(rwkv_l, axis=1))
```

```python
import functools
from typing import NamedTuple

import numpy as np
import jax
import jax.numpy as jnp
from jax import lax
from jax.experimental import pallas as pl
from jax.experimental.pallas import tpu as pltpu

D_MODEL = 1024
BATCH = 16
SEQ = 256
DEPTH = 4
DEC_BATCH = 2
DEC_SEQ = 1024
PAST_LEN = 256
GRID_W = 64
H_A = 8
DK_A = 64
DV_A = 64
H_B = 8
Q_RANK = 256
KV_RANK = 128
D_NOPE = 64
D_ROPE = 32
DV_B = 64
ROPE_BASE = 10000.0
H_C = 16
N_C = 64
D_FF = 2816
N_EXPERTS = 8
EPS = 1e-6
GN_EPS = 64e-5

F32 = jnp.float32
BF16 = jnp.bfloat16
HIGHEST = lax.Precision.HIGHEST

N_PROMPT = BATCH * SEQ
N_SAMPLE = DEC_BATCH * DEC_SEQ
N_TOK = N_PROMPT + N_SAMPLE
ROW_TILE = 256
N_TILES = N_TOK // ROW_TILE
LANES = 128
HEAD = 64
VMEM_LIMIT = 48 * 1024 * 1024
HALO = 16
PAIRS = 4
EW_TILE = 512
EW_TILES = N_TOK // EW_TILE
EW_PROMPT = N_PROMPT // EW_TILE
RESID_TILES = 4
MM_TM = 1024
MM_TN = 1024
FF_TN = 1408
MOE_TM = 512
MOE_ROWS = 2 * N_TOK + N_EXPERTS * MOE_TM
MOE_TILES = MOE_ROWS // MOE_TM

T_CQ, T_CKV, T_KR, T_KROT, T_N = 0, 256, 384, 512, 640


def _cparams(sem):
    return pltpu.CompilerParams(dimension_semantics=sem, vmem_limit_bytes=VMEM_LIMIT)


def _mod_kernel(c_ref, w_ref, b_ref, o_ref):
    c = c_ref[...]
    s = (c * jax.nn.sigmoid(c)).astype(BF16)
    o_ref[0] = jnp.dot(s, w_ref[0].astype(BF16), preferred_element_type=F32) + b_ref[0]


def _modulation(cond8, ada_w, ada_b):
    tn = 768
    return pl.pallas_call(
        _mod_kernel,
        out_shape=jax.ShapeDtypeStruct((DEPTH, 8, 6 * D_MODEL), F32),
        grid=(DEPTH, 6 * D_MODEL // tn),
        in_specs=[pl.BlockSpec((8, D_MODEL), lambda l, n: (0, 0)),
                  pl.BlockSpec((1, D_MODEL, tn), lambda l, n: (l, 0, n)),
                  pl.BlockSpec((1, 1, tn), lambda l, n: (l, 0, n))],
        out_specs=pl.BlockSpec((1, 8, tn), lambda l, n: (l, 0, n)),
        compiler_params=_cparams(("parallel", "parallel")),
        name="modulation",
    )(cond8, ada_w, ada_b.reshape(DEPTH, 1, 6 * D_MODEL))


def _rms(x, g):
    return x * lax.rsqrt(jnp.mean(x * x, axis=-1, keepdims=True) + EPS) * g


def _norm_mod_kernel(x_ref, g_ref, sc_ref, sh_ref, o_ref):
    h = _rms(x_ref[...], g_ref[...]) * (1.0 + sc_ref[0]) + sh_ref[0]
    o_ref[...] = h.astype(o_ref.dtype)


def _tile_spec():
    return pl.BlockSpec((1, 1, D_MODEL), lambda i: (i, 0, 0))


def _row_spec(width=D_MODEL, col=0):
    return pl.BlockSpec((ROW_TILE, width), lambda i: (i, col))


def _vec_spec(width=D_MODEL):
    return pl.BlockSpec((1, width), lambda i: (0, 0))


def _norm_mod(x, g, sc, sh):
    return pl.pallas_call(
        _norm_mod_kernel,
        out_shape=jax.ShapeDtypeStruct((N_TOK, D_MODEL), BF16),
        grid=(N_TILES,),
        in_specs=[_row_spec(), _vec_spec(), _tile_spec(), _tile_spec()],
        out_specs=_row_spec(),
        compiler_params=_cparams(("parallel",)),
        name="norm_mod",
    )(x, g.reshape(1, D_MODEL), sc, sh)


def _norm_mod_router_kernel(x_ref, g_ref, sc_ref, sh_ref, wr_ref, o_ref, route_ref):
    h = _rms(x_ref[...], g_ref[...]) * (1.0 + sc_ref[0]) + sh_ref[0]
    o_ref[...] = h
    logits = jnp.dot(h, wr_ref[...], precision=HIGHEST, preferred_element_type=F32)
    lane = lax.broadcasted_iota(jnp.int32, logits.shape, 1).astype(F32)
    neg = jnp.float32(-jnp.inf)
    lg = jnp.where(lane < N_EXPERTS, logits, neg)
    m1 = jnp.max(lg, axis=-1, keepdims=True)
    i1 = jnp.min(jnp.where(lg == m1, lane, float(LANES)), axis=-1, keepdims=True)
    lg2 = jnp.where(lane == i1, neg, lg)
    m2 = jnp.max(lg2, axis=-1, keepdims=True)
    i2 = jnp.min(jnp.where(lg2 == m2, lane, float(LANES)), axis=-1, keepdims=True)
    e = jnp.exp(m2 - m1)
    w1 = 1.0 / (1.0 + e)
    w2 = e * w1
    route_ref[...] = jnp.where(lane == 0.0, i1, jnp.where(lane == 1.0, i2, jnp.where(lane == 2.0, w1,
                                                                                    jnp.where(lane == 3.0, w2, 0.0))))


def _norm_mod_router(x, g, sc, sh, w_router):
    wr = jnp.pad(w_router, ((0, 0), (0, LANES - N_EXPERTS)))
    return pl.pallas_call(
        _norm_mod_router_kernel,
        out_shape=(jax.ShapeDtypeStruct((N_TOK, D_MODEL), F32),
                   jax.ShapeDtypeStruct((N_TOK, LANES), F32)),
        grid=(N_TILES,),
        in_specs=[_row_spec(), _vec_spec(), _tile_spec(), _tile_spec(),
                  pl.BlockSpec((D_MODEL, LANES), lambda i: (0, 0))],
        out_specs=(_row_spec(), _row_spec(LANES)),
        compiler_params=_cparams(("parallel",)),
        name="norm_mod_router",
    )(x, g.reshape(1, D_MODEL), sc, sh, wr)


def _mm_kernel(a_ref, w_ref, o_ref):
    o_ref[...] = jnp.dot(a_ref[...].astype(BF16), w_ref[...], preferred_element_type=F32).astype(o_ref.dtype)


def _mm(a, w, out_dtype=F32, tm=MM_TM, tn=MM_TN, name="mm"):
    m, k = a.shape
    n = w.shape[1]
    tm, tn = min(tm, m), min(tn, n)
    return pl.pallas_call(
        _mm_kernel,
        out_shape=jax.ShapeDtypeStruct((m, n), out_dtype),
        grid=(n // tn, m // tm),
        in_specs=[pl.BlockSpec((tm, k), lambda j, i: (i, 0)),
                  pl.BlockSpec((k, tn), lambda j, i: (0, j))],
        out_specs=pl.BlockSpec((tm, tn), lambda j, i: (i, j)),
        compiler_params=_cparams(("parallel", "parallel")),
        name=name,
    )(a, w)


def _mm_split(a, w, tn, tm=MM_TM, name="mm_split"):
    m, k = a.shape
    p = w.shape[1] // tn
    return pl.pallas_call(
        _mm_kernel,
        out_shape=jax.ShapeDtypeStruct((p, m, tn), F32),
        grid=(p, m // tm),
        in_specs=[pl.BlockSpec((tm, k), lambda j, i: (i, 0)),
                  pl.BlockSpec((k, tn), lambda j, i: (0, j))],
        out_specs=pl.BlockSpec((None, tm, tn), lambda j, i: (j, i, 0)),
        compiler_params=_cparams(("parallel", "parallel")),
        name=name,
    )(a, w)


def _resid_store(y, x_ref, g_ref, gt_ref, o_ref):
    g = g_ref[...]
    for t in range(RESID_TILES):
        rows = slice(t * ROW_TILE, (t + 1) * ROW_TILE)
        o_ref[rows, :] = x_ref[rows, :] + gt_ref[t] * _rms(y[rows], g)


def _resid_rows(width=D_MODEL):
    return pl.BlockSpec((RESID_TILES * ROW_TILE, width), lambda i: (i, 0))


def _resid_gate_spec():
    return pl.BlockSpec((RESID_TILES, 1, D_MODEL), lambda i: (i, 0, 0))


def _mm_resid_kernel(a_ref, w_ref, x_ref, g_ref, gt_ref, o_ref):
    y = jnp.dot(a_ref[...], w_ref[...], preferred_element_type=F32)
    _resid_store(y, x_ref, g_ref, gt_ref, o_ref)


def _mm_resid(a, w, x, g, gt):
    k = a.shape[1]
    return pl.pallas_call(
        _mm_resid_kernel,
        out_shape=jax.ShapeDtypeStruct((N_TOK, D_MODEL), F32),
        grid=(N_TILES // RESID_TILES,),
        in_specs=[_resid_rows(k), pl.BlockSpec((k, D_MODEL), lambda i: (0, 0)),
                  _resid_rows(), _vec_spec(), _resid_gate_spec()],
        out_specs=_resid_rows(),
        compiler_params=_cparams(("parallel",)),
        name="mm_resid",
    )(a, w, x, g.reshape(1, D_MODEL), gt)


def _mm2_resid_kernel(a1_ref, a2_ref, w_ref, x_ref, g_ref, gt_ref, o_ref):
    k1 = a1_ref.shape[1]
    y = (jnp.dot(a1_ref[...], w_ref[:k1], preferred_element_type=F32)
         + jnp.dot(a2_ref[...], w_ref[k1:], preferred_element_type=F32))
    _resid_store(y, x_ref, g_ref, gt_ref, o_ref)


def _mm2_resid(a1, a2, w, x, g, gt):
    k1, k2 = a1.shape[1], a2.shape[1]
    return pl.pallas_call(
        _mm2_resid_kernel,
        out_shape=jax.ShapeDtypeStruct((N_TOK, D_MODEL), F32),
        grid=(N_TILES // RESID_TILES,),
        in_specs=[_resid_rows(k1), _resid_rows(k2), pl.BlockSpec((k1 + k2, D_MODEL), lambda i: (0, 0)),
                  _resid_rows(), _vec_spec(), _resid_gate_spec()],
        out_specs=_resid_rows(),
        compiler_params=_cparams(("parallel",)),
        name="mm2_resid",
    )(a1, a2, w, x, g.reshape(1, D_MODEL), gt)


def _swiglu_kernel(a_ref, wg_ref, wu_ref, o_ref):
    a = a_ref[...]
    g = jnp.dot(a, wg_ref[...], preferred_element_type=F32)
    u = jnp.dot(a, wu_ref[...], preferred_element_type=F32)
    o_ref[...] = ((g * jax.nn.sigmoid(g)) * u).astype(o_ref.dtype)


def _swiglu(a, wg, wu, tm=MM_TM, tn=FF_TN):
    f = wg.shape[1]
    m = a.shape[0]
    return pl.pallas_call(
        _swiglu_kernel,
        out_shape=jax.ShapeDtypeStruct((m, f), BF16),
        grid=(f // tn, m // tm),
        in_specs=[pl.BlockSpec((tm, D_MODEL), lambda j, i: (i, 0)),
                  pl.BlockSpec((D_MODEL, tn), lambda j, i: (0, j)),
                  pl.BlockSpec((D_MODEL, tn), lambda j, i: (0, j))],
        out_specs=pl.BlockSpec((tm, tn), lambda j, i: (i, j)),
        compiler_params=_cparams(("parallel", "parallel")),
        name="swiglu",
    )(a, wg, wu)


def _moe_plan(route):
    e_flat = route[:, :2].astype(jnp.int32).reshape(-1)
    onehot = (e_flat[:, None] == jnp.arange(N_EXPERTS, dtype=jnp.int32)[None, :]).astype(jnp.int32)
    pos = jnp.cumsum(onehot, axis=0) - onehot
    padded = (jnp.sum(onehot, axis=0) + MOE_TM - 1) // MOE_TM * MOE_TM
    ends = jnp.cumsum(padded)
    dest = jnp.sum(onehot * ((ends - padded)[None, :] + pos), axis=1)
    token = jnp.arange(2 * N_TOK, dtype=jnp.int32) // 2
    src_tok = jnp.zeros((MOE_ROWS,), jnp.int32).at[dest].set(token)
    tile_start = jnp.arange(MOE_TILES, dtype=jnp.int32) * MOE_TM
    tile_e = jnp.sum((tile_start[:, None] >= ends[None, :]).astype(jnp.int32), axis=1)
    tile_e = jnp.minimum(tile_e, N_EXPERTS - 1)
    n_used = (ends[-1:] // MOE_TM).astype(jnp.int32)
    return dest, src_tok, tile_e, n_used


def _row_gather_kernel(idx_ref, src_ref, o_ref, rows_s):
    rows = o_ref.shape[0]
    base = pl.program_id(0) * rows

    def copy8(c, carry):
        for u in range(8):
            r = c * 8 + u
            rows_s[pl.ds(r, 1), :] = src_ref[pl.ds(idx_ref[base + r], 1), :]
        return carry

    lax.fori_loop(0, rows // 8, copy8, 0)
    o_ref[...] = rows_s[...].astype(o_ref.dtype)


def _row_gather(idx, src, n_rows, tm):
    n_src, w = src.shape
    return pl.pallas_call(
        _row_gather_kernel,
        out_shape=jax.ShapeDtypeStruct((n_rows, w), BF16),
        grid_spec=pltpu.PrefetchScalarGridSpec(
            num_scalar_prefetch=1, grid=(n_rows // tm,),
            in_specs=[pl.BlockSpec((n_src, w), lambda i, idx_ref: (0, 0), pipeline_mode=pl.Buffered(1))],
            out_specs=pl.BlockSpec((tm, w), lambda i, idx_ref: (i, 0)),
            scratch_shapes=[pltpu.VMEM((tm, w), F32)]),
        compiler_params=_cparams(("arbitrary",)),
        name="moe_gather",
    )(idx, src)


def _expert_changed(te_ref, i):
    return jnp.logical_or(i == 0, te_ref[i] != te_ref[jnp.maximum(i - 1, 0)])


def _moe_swiglu_kernel(te_ref, nu_ref, a_ref, wg_ref, wu_ref, o_ref, wg_s, wu_s):
    i = pl.program_id(1)

    @pl.when(_expert_changed(te_ref, i))
    def _():
        wg_s[...] = wg_ref[...].astype(BF16)
        wu_s[...] = wu_ref[...].astype(BF16)

    @pl.when(i < nu_ref[0])
    def _():
        a = a_ref[...]
        g = jnp.dot(a, wg_s[...], preferred_element_type=F32)
        u = jnp.dot(a, wu_s[...], preferred_element_type=F32)
        o_ref[...] = ((g * jax.nn.sigmoid(g)) * u).astype(o_ref.dtype)

    @pl.when(i >= nu_ref[0])
    def _():
        o_ref[...] = jnp.zeros_like(o_ref)


def _moe_swiglu(tile_e, n_used, a, wg, wu, layer, tn=FF_TN):
    f = wg.shape[3]
    w_spec = pl.BlockSpec((None, None, D_MODEL, tn), lambda j, i, te, nu: (layer, te[i], 0, j))
    return pl.pallas_call(
        _moe_swiglu_kernel,
        out_shape=jax.ShapeDtypeStruct((MOE_ROWS, f), BF16),
        grid_spec=pltpu.PrefetchScalarGridSpec(
            num_scalar_prefetch=2, grid=(f // tn, MOE_TILES),
            in_specs=[pl.BlockSpec((MOE_TM, D_MODEL), lambda j, i, te, nu: (i, 0)), w_spec, w_spec],
            out_specs=pl.BlockSpec((MOE_TM, tn), lambda j, i, te, nu: (i, j)),
            scratch_shapes=[pltpu.VMEM((D_MODEL, tn), BF16), pltpu.VMEM((D_MODEL, tn), BF16)]),
        compiler_params=_cparams(("parallel", "arbitrary")),
        name="moe_swiglu",
    )(tile_e, n_used, a, wg, wu)


def _moe_down_kernel(te_ref, nu_ref, h_ref, wd_ref, o_ref, wd_s):
    i = pl.program_id(0)

    @pl.when(_expert_changed(te_ref, i))
    def _():
        wd_s[...] = wd_ref[...].astype(BF16)

    @pl.when(i < nu_ref[0])
    def _():
        o_ref[...] = jnp.dot(h_ref[...], wd_s[...], preferred_element_type=F32)

    @pl.when(i >= nu_ref[0])
    def _():
        o_ref[...] = jnp.zeros_like(o_ref)


def _moe_down(tile_e, n_used, hid, wd, layer):
    f = wd.shape[2]
    return pl.pallas_call(
        _moe_down_kernel,
        out_shape=jax.ShapeDtypeStruct((MOE_ROWS, D_MODEL), F32),
        grid_spec=pltpu.PrefetchScalarGridSpec(
            num_scalar_prefetch=2, grid=(MOE_TILES,),
            in_specs=[pl.BlockSpec((MOE_TM, f), lambda i, te, nu: (i, 0)),
                      pl.BlockSpec((None, None, f, D_MODEL), lambda i, te, nu: (layer, te[i], 0, 0))],
            out_specs=pl.BlockSpec((MOE_TM, D_MODEL), lambda i, te, nu: (i, 0)),
            scratch_shapes=[pltpu.VMEM((f, D_MODEL), BF16)]),
        compiler_params=_cparams(("arbitrary",)),
        name="moe_down",
    )(tile_e, n_used, hid, wd)


def _moe_combine_kernel(dest_ref, y_ref, route_ref, x_ref, g_ref, gt_ref, o_ref, buf, sem):
    base = pl.program_id(0) * ROW_TILE

    def issue(r, carry):
        a = 2 * (base + r)
        pltpu.make_async_copy(y_ref.at[dest_ref[a]], buf.at[0, r], sem).start()
        pltpu.make_async_copy(y_ref.at[dest_ref[a + 1]], buf.at[1, r], sem).start()
        return carry

    lax.fori_loop(0, ROW_TILE, issue, 0)

    def wait(r, carry):
        pltpu.make_async_copy(y_ref.at[0], buf.at[0, 0], sem).wait()
        pltpu.make_async_copy(y_ref.at[0], buf.at[0, 0], sem).wait()
        return carry

    lax.fori_loop(0, ROW_TILE, wait, 0)
    route = route_ref[...]
    ff = route[:, 2:3] * buf[0] + route[:, 3:4] * buf[1]
    o_ref[...] = x_ref[...] + gt_ref[0] * _rms(ff, g_ref[...])


def _moe_combine(dest, y_sorted, route, x, g, gt):
    return pl.pallas_call(
        _moe_combine_kernel,
        out_shape=jax.ShapeDtypeStruct((N_TOK, D_MODEL), F32),
        grid_spec=pltpu.PrefetchScalarGridSpec(
            num_scalar_prefetch=1, grid=(N_TILES,),
            in_specs=[pl.BlockSpec(memory_space=pl.ANY),
                      pl.BlockSpec((ROW_TILE, LANES), lambda i, d: (i, 0)),
                      pl.BlockSpec((ROW_TILE, D_MODEL), lambda i, d: (i, 0)),
                      pl.BlockSpec((1, D_MODEL), lambda i, d: (0, 0)),
                      pl.BlockSpec((1, 1, D_MODEL), lambda i, d: (i, 0, 0))],
            out_specs=pl.BlockSpec((ROW_TILE, D_MODEL), lambda i, d: (i, 0)),
            scratch_shapes=[pltpu.VMEM((2, ROW_TILE, D_MODEL), F32), pltpu.SemaphoreType.DMA(())]),
        compiler_params=_cparams(("arbitrary",)),
        name="moe_combine",
    )(dest, y_sorted, route, x, g.reshape(1, D_MODEL), gt)


def _lora_kernel(x_ref, w1_ref, w2_ref, b_ref, o_ref, *, mid, out):
    t = mid(jnp.dot(x_ref[...], w1_ref[...], preferred_element_type=F32))
    o_ref[...] = out(jnp.dot(t.astype(BF16), w2_ref[...], preferred_element_type=F32) + b_ref[...])


def _lora(x, w1, w2, b, mid, out, tm=MM_TM):
    m = x.shape[0]
    r = w1.shape[1]
    return pl.pallas_call(
        functools.partial(_lora_kernel, mid=mid, out=out),
        out_shape=jax.ShapeDtypeStruct((m, D_MODEL), F32),
        grid=(m // tm,),
        in_specs=[pl.BlockSpec((tm, D_MODEL), lambda i: (i, 0)),
                  pl.BlockSpec((D_MODEL, r), lambda i: (0, 0)),
                  pl.BlockSpec((r, D_MODEL), lambda i: (0, 0)),
                  pl.BlockSpec((1, D_MODEL), lambda i: (0, 0))],
        out_specs=pl.BlockSpec((tm, D_MODEL), lambda i: (i, 0)),
        compiler_params=_cparams(("parallel",)),
        name="lora",
    )(x, w1.astype(BF16), w2.astype(BF16), b.reshape(1, D_MODEL))


def _identity(x):
    return x


def _decay(wr):
    return jnp.exp(-float(np.exp(-0.5)) * jax.nn.sigmoid(wr))


def _block_ones():
    i = np.arange(LANES)
    return jnp.asarray((i[:, None] // HEAD == i[None, :] // HEAD).astype(np.float32))


def _head_sum(x, bo):
    parts = [jnp.dot(x[:, c * LANES:(c + 1) * LANES], bo, precision=HIGHEST, preferred_element_type=F32)
             for c in range(x.shape[1] // LANES)]
    return jnp.concatenate(parts, axis=-1)


class _ScanMode(NamedTuple):
    prompt: bool
    groups: int
    rep: int
    r2: int
    tb: int
    pipelined: bool = True


def _scan_gather(mode, src_ref, fwd_ref, bwd_ref, s):
    sb = mode.tb - 1 - s
    if mode.prompt:
        for d, (ref, tt) in enumerate(((fwd_ref, s), (bwd_ref, sb))):
            z = ref[pl.ds(pl.multiple_of(tt * BATCH, BATCH), BATCH), :]
            for hp in range(PAIRS):
                r0 = d * HEAD + hp * BATCH
                src_ref[r0:r0 + BATCH, :] = z[:, hp * LANES:(hp + 1) * LANES]
        return src_ref[...].T
    unit = 4 * mode.r2
    for q in range(mode.rep):
        for d, (ref, tt) in enumerate(((fwd_ref, s), (bwd_ref, sb))):
            for b in range(DEC_BATCH):
                r0 = q * unit + (2 * d + b) * mode.r2
                src_ref[r0:r0 + mode.r2, 0:HEAD] = ref[b, tt]
    return src_ref[...].T[0:HEAD]


def _scan_value_slab(mode, vt):
    vsub = HEAD // mode.rep
    out = vt[0:vsub]
    if mode.rep > 1:
        lane = lax.broadcasted_iota(jnp.int32, (vsub, LANES), 1)
        for q in range(1, mode.rep):
            out = jnp.where(lane >= q * (LANES // mode.rep), vt[q * vsub:(q + 1) * vsub], out)
    return out


def _scan_scatter(mode, src_ref, ys_ref, yf_ref, yb_ref, s):
    sb = mode.tb - 1 - s
    if mode.prompt:
        tr = jnp.concatenate([ys_ref[s, 0], ys_ref[s, 1]], axis=0).T
        for d, (ref, tt) in enumerate(((yf_ref, s), (yb_ref, sb))):
            row0 = pl.multiple_of(tt * BATCH, BATCH)
            for hp in range(PAIRS):
                r0 = d * HEAD + hp * BATCH
                ref[pl.ds(row0, BATCH), hp * LANES:(hp + 1) * LANES] = tr[r0:r0 + BATCH]
        return
    vsub = HEAD // mode.rep
    y = ys_ref[s, 0]
    for q in range(mode.rep):
        src_ref[q * vsub:(q + 1) * vsub, :] = y
    tr = src_ref[...].T
    unit = 4 * mode.r2
    out = tr[0:unit]
    lane = lax.broadcasted_iota(jnp.int32, (unit, LANES), 1)
    for q in range(1, mode.rep):
        out = jnp.where(lane >= q * vsub, tr[q * unit:(q + 1) * unit], out)
    for d, (ref, tt) in enumerate(((yf_ref, s), (yb_ref, sb))):
        for b in range(DEC_BATCH):
            r0 = (2 * d + b) * mode.r2
            ref[b, tt] = out[r0:r0 + mode.r2, 0:HEAD]


def _n_acc(vsub):
    return 1 if vsub == HEAD else 2


def _scan_init(mode, t_axis, s_ref, s0_ref, src_ref):
    @pl.when(pl.program_id(t_axis) == 0)
    def _():
        if s0_ref is None:
            s_ref[...] = jnp.zeros_like(s_ref)
        else:
            s_ref[...] = s0_ref[...]

    if not mode.prompt:
        src_ref[...] = jnp.zeros_like(src_ref)


def _hgrn_scan_kernel(*refs, mode, has_s0, t_axis):
    qf, qb, xf, xb, vf, vb, lb_ref = refs[:7]
    refs = refs[7:]
    s0_ref = None
    if has_s0:
        s0_ref, refs = refs[0], refs[1:]
    of_ref, ob_ref, s_ref, src_ref, q_t, f_t, k_t, v_t, ys_ref = refs
    vsub = HEAD // mode.rep
    n_acc = _n_acc(vsub)
    _scan_init(mode, t_axis, s_ref, s0_ref, src_ref)
    lb = lb_ref[...]

    def gather(s):
        q_t[s] = _scan_gather(mode, src_ref, qf, qb, s)
        x = _scan_gather(mode, src_ref, xf, xb, s)
        f_t[s] = lb + (1.0 - lb) * jax.nn.sigmoid(x)
        k_t[s] = (1.0 - lb) * jax.nn.sigmoid(-x)
        v_t[s] = _scan_gather(mode, src_ref, vf, vb, s)

    def step(s):
        for g in range(mode.groups):
            r0 = g * HEAD
            vv = _scan_value_slab(mode, v_t[s, r0:r0 + HEAD, :])
            acc = [jnp.zeros((vsub, LANES), F32) for _ in range(n_acc)]
            for d in range(HEAD):
                r = r0 + d
                sn = s_ref[g, d] * f_t[s, r:r + 1, :] + vv * k_t[s, r:r + 1, :]
                s_ref[g, d] = sn
                acc[d % n_acc] = acc[d % n_acc] + sn * q_t[s, r:r + 1, :]
            ys_ref[s, g] = sum(acc[1:], acc[0])

    def scatter(s):
        _scan_scatter(mode, src_ref, ys_ref, of_ref, ob_ref, s)

    _scan_run(mode, ys_ref, gather, step, scatter)


def _scan_run(mode, ys_ref, gather, step, scatter):
    if not mode.pipelined:
        for phase in (gather, step, scatter):
            lax.fori_loop(0, mode.tb, lambda s, carry, phase=phase: (phase(s), carry)[1], 0)
        return
    last = mode.tb - 1
    gather(0)
    ys_ref[0] = jnp.zeros(ys_ref.shape[1:], F32)

    def body(s, carry):
        gather(jnp.minimum(s + 1, last))
        scatter(jnp.maximum(s - 1, 0))
        step(s)
        return carry

    lax.fori_loop(0, mode.tb, body, 0)
    scatter(last)


def _rwkv_scan_kernel(*refs, mode, has_s0, t_axis):
    ins = refs[:12]
    refs = refs[12:]
    s0_ref = None
    if has_s0:
        s0_ref, refs = refs[0], refs[1:]
    yf_ref, yb_ref, s_ref, src_ref, r_t, w_t, k_t, v_t, kk_t, a_t, ys_ref = refs
    vsub = HEAD // mode.rep
    n_acc = _n_acc(vsub)
    _scan_init(mode, t_axis, s_ref, s0_ref, src_ref)

    def gather(s):
        for i, x_t in enumerate((r_t, w_t, k_t, v_t, kk_t, a_t)):
            x_t[s] = _scan_gather(mode, src_ref, ins[2 * i], ins[2 * i + 1], s)

    def step(s):
        for g in range(mode.groups):
            r0 = g * HEAD
            acc = [jnp.zeros((vsub, LANES), F32) for _ in range(n_acc)]
            for d in range(HEAD):
                acc[d % n_acc] = acc[d % n_acc] + s_ref[g, d] * kk_t[s, r0 + d:r0 + d + 1, :]
            sa = -sum(acc[1:], acc[0])
            vv = _scan_value_slab(mode, v_t[s, r0:r0 + HEAD, :])
            acc = [jnp.zeros((vsub, LANES), F32) for _ in range(n_acc)]
            for d in range(HEAD):
                r = r0 + d
                b = kk_t[s, r:r + 1, :] * a_t[s, r:r + 1, :]
                sn = s_ref[g, d] * w_t[s, r:r + 1, :] + sa * b + vv * k_t[s, r:r + 1, :]
                s_ref[g, d] = sn
                acc[d % n_acc] = acc[d % n_acc] + sn * r_t[s, r:r + 1, :]
            ys_ref[s, g] = sum(acc[1:], acc[0])

    def scatter(s):
        _scan_scatter(mode, src_ref, ys_ref, yf_ref, yb_ref, s)

    _scan_run(mode, ys_ref, gather, step, scatter)


def _scan_scratch(mode, n_streams):
    rows = mode.groups * HEAD
    vsub = HEAD // mode.rep
    return ([pltpu.VMEM((LANES, LANES), F32)]
            + [pltpu.VMEM((mode.tb, rows, LANES), F32) for _ in range(n_streams)]
            + [pltpu.VMEM((mode.tb, mode.groups, vsub, LANES), F32)])


def _hgrn_scans(slabs, lb, state0):
    mode = _ScanMode(prompt=True, groups=2, rep=1, r2=0, tb=16, pipelined=False)
    n_t = SEQ // mode.tb
    rows = mode.tb * BATCH
    width = PAIRS * LANES
    assert width == H_A * HEAD

    def spec(p, rev):
        if rev:
            return pl.BlockSpec((None, rows, width), lambda t: (p, n_t - 1 - t, 0))
        return pl.BlockSpec((None, rows, width), lambda t: (p, t, 0))

    def ospec(rev):
        if rev:
            return pl.BlockSpec((rows, width), lambda t: (n_t - 1 - t, 0))
        return pl.BlockSpec((rows, width), lambda t: (t, 0))

    lb4 = lb.reshape(2, PAIRS, 2, HEAD)
    lb_p = jnp.broadcast_to(jnp.transpose(lb4, (2, 3, 0, 1))[..., None], (2, HEAD, 2, PAIRS, BATCH))
    lb_p = lb_p.reshape(2 * HEAD, LANES)
    st_spec = pl.BlockSpec((2, HEAD, HEAD, LANES), lambda t: (0, 0, 0, 0))
    o_shape = jax.ShapeDtypeStruct((N_PROMPT, width), F32)
    view = slabs
    o_fp, o_bp, s_fin = pl.pallas_call(
        functools.partial(_hgrn_scan_kernel, mode=mode, has_s0=False, t_axis=0),
        out_shape=(o_shape, o_shape, jax.ShapeDtypeStruct((2, HEAD, HEAD, LANES), F32)),
        grid=(n_t,),
        in_specs=[spec(0, False), spec(0, True), spec(1, False), spec(2, True), spec(3, False), spec(3, True),
                  pl.BlockSpec((2 * HEAD, LANES), lambda t: (0, 0))],
        out_specs=(ospec(False), ospec(True), st_spec),
        scratch_shapes=_scan_scratch(mode, 4),
        compiler_params=_cparams(("arbitrary",)),
        name="hgrn_scan_prompt",
    )(view, view, view, view, view, view, lb_p)
    s_fin = jnp.transpose(s_fin.reshape(2, HEAD, HEAD, 2, PAIRS, BATCH), (5, 3, 4, 0, 1, 2))
    s_fin = s_fin.reshape(BATCH, 2, H_A, DK_A, DV_A)

    mode = _ScanMode(prompt=False, groups=1, rep=4, r2=H_A, tb=32)
    n_t = DEC_SEQ // mode.tb
    view = slabs[:, N_PROMPT:].reshape(5, DEC_BATCH, DEC_SEQ, H_A, HEAD)

    def spec(p, rev):
        if rev:
            return pl.BlockSpec((None, DEC_BATCH, mode.tb, H_A, HEAD), lambda t: (p, 0, n_t - 1 - t, 0, 0))
        return pl.BlockSpec((None, DEC_BATCH, mode.tb, H_A, HEAD), lambda t: (p, 0, t, 0, 0))

    def ospec(rev):
        if rev:
            return pl.BlockSpec((DEC_BATCH, mode.tb, H_A, HEAD), lambda t: (0, n_t - 1 - t, 0, 0))
        return pl.BlockSpec((DEC_BATCH, mode.tb, H_A, HEAD), lambda t: (0, t, 0, 0))

    vsub = HEAD // mode.rep
    lb_s = jnp.transpose(lb.reshape(2, H_A, HEAD), (2, 0, 1))
    lb_s = jnp.broadcast_to(lb_s[:, None, :, None, :], (HEAD, mode.rep, 2, DEC_BATCH, H_A)).reshape(HEAD, LANES)
    s0 = state0.reshape(DEC_BATCH, 2, H_A, DK_A, mode.rep, vsub)
    s0 = jnp.transpose(s0, (3, 5, 4, 1, 0, 2)).reshape(1, HEAD, vsub, LANES)
    st_spec = pl.BlockSpec((1, HEAD, vsub, LANES), lambda t: (0, 0, 0, 0))
    o_shape = jax.ShapeDtypeStruct((DEC_BATCH, DEC_SEQ, H_A, HEAD), F32)
    o_fl, o_bl, _ = pl.pallas_call(
        functools.partial(_hgrn_scan_kernel, mode=mode, has_s0=True, t_axis=0),
        out_shape=(o_shape, o_shape, jax.ShapeDtypeStruct((1, HEAD, vsub, LANES), F32)),
        grid=(n_t,),
        in_specs=[spec(0, False), spec(0, True), spec(1, False), spec(2, True), spec(3, False), spec(3, True),
                  pl.BlockSpec((HEAD, LANES), lambda t: (0, 0)), st_spec],
        out_specs=(ospec(False), ospec(True), st_spec),
        scratch_shapes=_scan_scratch(mode, 4),
        compiler_params=_cparams(("arbitrary",)),
        name="hgrn_scan_latent",
    )(view, view, view, view, view, view, lb_s, s0)
    return o_fp, o_bp, o_fl.reshape(N_SAMPLE, width), o_bl.reshape(N_SAMPLE, width), s_fin


def _rwkv_scans(r, w_f, w_b, k, v, kk, a, state0):
    streams = ((r, r), (w_f, w_b), (k, k), (v, v), (kk, kk), (a, a))
    mode = _ScanMode(prompt=True, groups=2, rep=1, r2=0, tb=16)
    n_t = SEQ // mode.tb
    rows = mode.tb * BATCH
    width = PAIRS * LANES
    n_half = D_MODEL // width

    fwd = pl.BlockSpec((rows, width), lambda h, t: (t, h))
    bwd = pl.BlockSpec((rows, width), lambda h, t: (n_t - 1 - t, h))
    st_spec = pl.BlockSpec((None, 2, HEAD, HEAD, LANES), lambda h, t: (h, 0, 0, 0, 0))
    y_shape = jax.ShapeDtypeStruct((N_PROMPT, D_MODEL), F32)
    args = []
    for x_f, x_b in streams:
        args += [x_f, x_b]
    y_fp, y_bp, s_fin = pl.pallas_call(
        functools.partial(_rwkv_scan_kernel, mode=mode, has_s0=False, t_axis=1),
        out_shape=(y_shape, y_shape, jax.ShapeDtypeStruct((n_half, 2, HEAD, HEAD, LANES), F32)),
        grid=(n_half, n_t),
        in_specs=[fwd, bwd] * 6,
        out_specs=(fwd, bwd, st_spec),
        scratch_shapes=_scan_scratch(mode, 6),
        compiler_params=_cparams(("parallel", "arbitrary")),
        name="rwkv_scan_prompt",
    )(*args)
    s_fin = s_fin.reshape(n_half, 2, HEAD, HEAD, 2, PAIRS, BATCH)
    s_fin = jnp.transpose(s_fin, (6, 4, 0, 5, 1, 3, 2)).reshape(BATCH, 2, H_C, N_C, N_C)

    mode = _ScanMode(prompt=False, groups=1, rep=2, r2=H_C, tb=32)
    n_t = DEC_SEQ // mode.tb

    def view(x):
        return x[N_PROMPT:].reshape(DEC_BATCH, DEC_SEQ, H_C, HEAD)

    fwd = pl.BlockSpec((DEC_BATCH, mode.tb, H_C, HEAD), lambda t: (0, t, 0, 0))
    bwd = pl.BlockSpec((DEC_BATCH, mode.tb, H_C, HEAD), lambda t: (0, n_t - 1 - t, 0, 0))
    vsub = HEAD // mode.rep
    s0 = state0.reshape(DEC_BATCH, 2, H_C, mode.rep, vsub, N_C)
    s0 = jnp.transpose(s0, (5, 4, 3, 1, 0, 2)).reshape(1, HEAD, vsub, LANES)
    st_spec = pl.BlockSpec((1, HEAD, vsub, LANES), lambda t: (0, 0, 0, 0))
    y_shape = jax.ShapeDtypeStruct((DEC_BATCH, DEC_SEQ, H_C, HEAD), F32)
    args = []
    for x_f, x_b in streams:
        x_fv = view(x_f)
        args += [x_fv, x_fv if x_b is x_f else view(x_b)]
    y_fl, y_bl, _ = pl.pallas_call(
        functools.partial(_rwkv_scan_kernel, mode=mode, has_s0=True, t_axis=0),
        out_shape=(y_shape, y_shape, jax.ShapeDtypeStruct((1, HEAD, vsub, LANES), F32)),
        grid=(n_t,),
        in_specs=[fwd, bwd] * 6 + [st_spec],
        out_specs=(fwd, bwd, st_spec),
        scratch_shapes=_scan_scratch(mode, 6),
        compiler_params=_cparams(("arbitrary",)),
        name="rwkv_scan_latent",
    )(*args, s0)
    return y_fp, y_bp, y_fl.reshape(N_SAMPLE, D_MODEL), y_bl.reshape(N_SAMPLE, D_MODEL), s_fin


def _ew_spec(width=D_MODEL, col=0):
    return pl.BlockSpec((EW_TILE, width), lambda i: (i, col))


def _prompt_spec(width):
    return pl.BlockSpec((EW_TILE, width), lambda i: (jnp.minimum(i, EW_PROMPT - 1), 0))


def _latent_spec(width):
    return pl.BlockSpec((EW_TILE, width), lambda i: (jnp.maximum(i - EW_PROMPT, 0), 0))


def _hgrn_post_kernel(ofp_ref, obp_ref, ofl_ref, obl_ref, g_ref, gain_ref, bo_ref, o_ref):
    is_prompt = pl.program_id(0) < EW_PROMPT
    o = jnp.where(is_prompt, ofp_ref[...] + obp_ref[...], ofl_ref[...] + obl_ref[...])
    ms = _head_sum(o * o, bo_ref[...]) * (1.0 / DV_A)
    y = o * lax.rsqrt(ms + EPS) * gain_ref[...]
    g = g_ref[...]
    o_ref[...] = (y * (g * jax.nn.sigmoid(g))).astype(o_ref.dtype)


def _hgrn_post(o_fp, o_bp, o_fl, o_bl, slabs, gain):
    w = H_A * DV_A
    return pl.pallas_call(
        _hgrn_post_kernel,
        out_shape=jax.ShapeDtypeStruct((N_TOK, w), BF16),
        grid=(EW_TILES,),
        in_specs=[_prompt_spec(w), _prompt_spec(w), _latent_spec(w), _latent_spec(w),
                  pl.BlockSpec((None, EW_TILE, w), lambda i: (4, i, 0)),
                  _vec_spec(w), pl.BlockSpec((LANES, LANES), lambda i: (0, 0))],
        out_specs=_ew_spec(w),
        compiler_params=_cparams(("parallel",)),
        name="hgrn_post",
    )(o_fp, o_bp, o_fl, o_bl, slabs, jnp.tile(gain, H_A).reshape(1, w), _block_ones())


def _q_up_kernel(cq_ref, gain_ref, w_ref, cos_ref, sin_ref, qn_ref, qr_ref):
    cq = _rms(cq_ref[...], gain_ref[...]).astype(BF16)
    q = jnp.dot(cq, w_ref[...], preferred_element_type=F32)
    qn_ref[...] = q[:, :512].astype(BF16)
    qr_ref[...] = (q[:, 512:768] * cos_ref[...] + q[:, 768:1024] * sin_ref[...]).astype(BF16)


def _q_up(tail, gain, w, cos, sin):
    return pl.pallas_call(
        _q_up_kernel,
        out_shape=(jax.ShapeDtypeStruct((N_TOK, 512), BF16), jax.ShapeDtypeStruct((N_TOK, 256), BF16)),
        grid=(EW_TILES,),
        in_specs=[_ew_spec(Q_RANK, T_CQ // Q_RANK),
                  _vec_spec(Q_RANK), pl.BlockSpec((Q_RANK, 1024), lambda i: (0, 0)),
                  _ew_spec(256), _ew_spec(256)],
        out_specs=(_ew_spec(512), _ew_spec(256)),
        compiler_params=_cparams(("parallel",)),
        name="q_up",
    )(tail, gain.reshape(1, Q_RANK), w, cos, sin)


def _kv_up_kernel(ckv_ref, kr_ref, krot_ref, gain_ref, w_ref, cos_ref, sin_ref, ckvn_ref, kn_ref, v_ref, krope_ref):
    ckv = _rms(ckv_ref[...], gain_ref[...])
    ckvn_ref[...] = ckv
    kv = jnp.dot(ckv.astype(BF16), w_ref[...], preferred_element_type=F32)
    kn_ref[...] = kv[:, :512].astype(BF16)
    v_ref[...] = kv[:, 512:].astype(BF16)
    krope_ref[...] = (kr_ref[...] * cos_ref[...] + krot_ref[...] * sin_ref[...]).astype(BF16)


def _kv_up(tail, gain, w, cos4, sin4):
    return pl.pallas_call(
        _kv_up_kernel,
        out_shape=(jax.ShapeDtypeStruct((N_TOK, KV_RANK), F32), jax.ShapeDtypeStruct((N_TOK, 512), BF16),
                   jax.ShapeDtypeStruct((N_TOK, 512), BF16), jax.ShapeDtypeStruct((N_TOK, LANES), BF16)),
        grid=(EW_TILES,),
        in_specs=[_ew_spec(LANES, T_CKV // LANES), _ew_spec(LANES, T_KR // LANES), _ew_spec(LANES, T_KROT // LANES),
                  _vec_spec(KV_RANK), pl.BlockSpec((KV_RANK, 1024), lambda i: (0, 0)),
                  _ew_spec(LANES), _ew_spec(LANES)],
        out_specs=(_ew_spec(KV_RANK), _ew_spec(512), _ew_spec(512), _ew_spec(LANES)),
        compiler_params=_cparams(("parallel",)),
        name="kv_up",
    )(tail, tail, tail, gain.reshape(1, KV_RANK), w, cos4, sin4)


def _attn_kernel(qn_ref, qr_ref, kn_ref, kr_ref, v_ref, o_ref, *, scale):
    kr = kr_ref[0]
    lane = lax.broadcasted_iota(jnp.int32, (1, LANES), 1)
    zero = jnp.zeros((), BF16)
    for hp in range(H_B // 2):
        cols = slice(hp * LANES, (hp + 1) * LANES)
        qn = qn_ref[0, :, cols]
        qr = qr_ref[0, :, (hp // 2) * LANES:(hp // 2 + 1) * LANES]
        kcat = jnp.concatenate([kn_ref[0, :, cols], kr], axis=-1)
        v = v_ref[0, :, cols]
        outs = []
        for j in range(2):
            mn = (lane // D_NOPE) == j
            mr = (lane // D_ROPE) == (2 * hp + j) % 4
            qcat = jnp.concatenate([jnp.where(mn, qn, zero), jnp.where(mr, qr, zero)], axis=-1)
            s = lax.dot_general(qcat, kcat, (((1,), (1,)), ((), ())), preferred_element_type=F32) * scale
            p = jnp.exp(s - jnp.max(s, axis=-1, keepdims=True))
            l = jnp.sum(p, axis=-1, keepdims=True)
            outs.append(jnp.dot(p.astype(BF16), v, preferred_element_type=F32) / l)
        o_ref[0, :, cols] = jnp.where(lane < DV_B, outs[0], outs[1]).astype(o_ref.dtype)


def _attention(qn, qr, kn, kr, v, bsz, time_major, tq=256):
    t_len, tk = qn.shape[1], kn.shape[1]

    def spec(is_query, width):
        rows = tq if is_query else tk
        if time_major:
            return pl.BlockSpec((1, rows, width), lambda b, q: (0, q if is_query else 0, b))
        return pl.BlockSpec((1, rows, width), lambda b, q: (b, q if is_query else 0, 0))

    w_n, w_r = H_B * D_NOPE, H_B * D_ROPE
    return pl.pallas_call(
        functools.partial(_attn_kernel, scale=float((D_NOPE + D_ROPE) ** -0.5)),
        out_shape=jax.ShapeDtypeStruct(qn.shape, BF16),
        grid=(bsz, t_len // tq),
        in_specs=[spec(True, w_n), spec(True, w_r), spec(False, w_n), spec(False, LANES), spec(False, w_n)],
        out_specs=spec(True, w_n),
        compiler_params=_cparams(("parallel", "parallel")),
        name="attention",
    )(qn, qr, kn, kr, v)


def _odd_pre_kernel(x_ref, xp_ref, xn_ref, g_ref, sc_ref, sh_ref, mu_ref, *o_refs):
    i = pl.program_id(0)
    g, sc, sh = g_ref[...], sc_ref[0], sh_ref[0]

    def nm(x):
        return _rms(x, g) * (1.0 + sc) + sh

    h = nm(x_ref[...])
    hp = nm(xp_ref[...])
    hn = nm(xn_ref[...])
    is_prompt = i < BATCH
    hp_p = jnp.where(i > 0, hp, 0.0)
    hn_p = jnp.where(i < BATCH - 1, hn, 0.0)
    prev_p = jnp.concatenate([hp_p, h[:ROW_TILE - BATCH]], axis=0)
    next_p = jnp.concatenate([h[BATCH:], hn_p], axis=0)
    part = (i - BATCH) % 4
    hp_s = jnp.where(part != 0, hp[HALO - 1:HALO], 0.0)
    hn_s = jnp.where(part != 3, hn[0:1], 0.0)
    row = lax.broadcasted_iota(jnp.int32, (ROW_TILE, 1), 0)
    prev_s = jnp.where(row == 0, hp_s, pltpu.roll(h, 1, axis=0))
    next_s = jnp.where(row == ROW_TILE - 1, hn_s, pltpu.roll(h, ROW_TILE - 1, axis=0))
    prev = jnp.where(is_prompt, prev_p, prev_s)
    nxt = jnp.where(is_prompt, next_p, next_s)
    xx = 0.5 * (prev + nxt) - h
    for j, o_ref in enumerate(o_refs):
        o_ref[...] = (h + xx * mu_ref[j:j + 1, :]).astype(o_ref.dtype)


def _odd_pre(x, g, sc, sh, mu):
    assert HALO == BATCH
    per = ROW_TILE // HALO
    last = N_TOK // HALO - 1
    return pl.pallas_call(
        _odd_pre_kernel,
        out_shape=tuple(jax.ShapeDtypeStruct((N_TOK, D_MODEL), BF16) for _ in range(6)),
        grid=(N_TILES,),
        in_specs=[_row_spec(),
                  pl.BlockSpec((HALO, D_MODEL), lambda i: (jnp.maximum(i * per - 1, 0), 0)),
                  pl.BlockSpec((HALO, D_MODEL), lambda i: (jnp.minimum((i + 1) * per, last), 0)),
                  _vec_spec(), _tile_spec(), _tile_spec(),
                  pl.BlockSpec((8, D_MODEL), lambda i: (0, 0))],
        out_specs=tuple(_row_spec() for _ in range(6)),
        compiler_params=_cparams(("parallel",)),
        name="odd_pre",
    )(x, x, x, g.reshape(1, D_MODEL), sc, sh, jnp.pad(mu, ((0, 2), (0, 0))))


def _rwkv_prep_kernel(k_ref, a_ref, kk_w_ref, ka_w_ref, bo_ref, kk_ref, k2_ref):
    k = k_ref[...]
    kk = k * kk_w_ref[...]
    kk_ref[...] = kk * lax.rsqrt(_head_sum(kk * kk, bo_ref[...]) + 1e-12)
    k2_ref[...] = k * (1.0 + (a_ref[...] - 1.0) * ka_w_ref[...])


def _rwkv_prep(k, a, k_k, k_a):
    return pl.pallas_call(
        _rwkv_prep_kernel,
        out_shape=(jax.ShapeDtypeStruct((N_TOK, D_MODEL), F32), jax.ShapeDtypeStruct((N_TOK, D_MODEL), F32)),
        grid=(EW_TILES,),
        in_specs=[_ew_spec(), _ew_spec(), _vec_spec(), _vec_spec(),
                  pl.BlockSpec((LANES, LANES), lambda i: (0, 0))],
        out_specs=(_ew_spec(), _ew_spec()),
        compiler_params=_cparams(("parallel",)),
        name="rwkv_prep",
    )(k, a, k_k.reshape(1, D_MODEL), k_a.reshape(1, D_MODEL), _block_ones())


def _rwkv_post_kernel(yfp_ref, ybp_ref, yfl_ref, ybl_ref, r_ref, k_ref, v_ref, g_ref, rk_ref, lnw_ref, lnb_ref,
                      bo_ref, o_ref):
    bo = bo_ref[...]
    y = jnp.where(pl.program_id(0) < EW_PROMPT, yfp_ref[...] + ybp_ref[...], yfl_ref[...] + ybl_ref[...])
    d = y - _head_sum(y, bo) * (1.0 / N_C)
    var = _head_sum(d * d, bo) * (1.0 / N_C)
    yn = d * lax.rsqrt(var + GN_EPS) * lnw_ref[...] + lnb_ref[...]
    bonus = _head_sum(r_ref[...] * k_ref[...] * rk_ref[...], bo) * v_ref[...]
    o_ref[...] = ((yn + bonus) * g_ref[...]).astype(o_ref.dtype)


def _rwkv_post(y_fp, y_bp, y_fl, y_bl, r, k2, v, g, r_k, ln_w, ln_b):
    return pl.pallas_call(
        _rwkv_post_kernel,
        out_shape=jax.ShapeDtypeStruct((N_TOK, D_MODEL), BF16),
        grid=(EW_TILES,),
        in_specs=([_prompt_spec(D_MODEL)] * 2 + [_latent_spec(D_MODEL)] * 2 + [_ew_spec()] * 4 + [_vec_spec()] * 3
                  + [pl.BlockSpec((LANES, LANES), lambda i: (0, 0))]),
        out_specs=_ew_spec(),
        compiler_params=_cparams(("parallel",)),
        name="rwkv_post",
    )(y_fp, y_bp, y_fl, y_bl, r, k2, v, g, r_k.reshape(1, D_MODEL), ln_w.reshape(1, D_MODEL),
      ln_b.reshape(1, D_MODEL), _block_ones())


def _rot_cols(w):
    wb = w.reshape(w.shape[:-1] + (2, 2, D_ROPE // 4))
    return jnp.concatenate([-wb[..., 1:, :], wb[..., :1, :]], axis=-2).reshape(w.shape)


def _rope_tables():
    rows = DEC_SEQ // GRID_W
    row, col = np.meshgrid(np.arange(rows), np.arange(GRID_W), indexing='ij')
    row = jnp.asarray(row.reshape(-1), F32)
    col = jnp.asarray(col.reshape(-1), F32)
    n_freq = D_ROPE // 4
    inv = 1.0 / (ROPE_BASE ** (jnp.arange(n_freq, dtype=F32) / n_freq))
    ar = row[:, None] * inv
    ac = col[:, None] * inv
    ang = jnp.concatenate([ar, ar, ac, ac], axis=-1)
    cos, sin = jnp.cos(ang), jnp.sin(ang)
    cos = jnp.concatenate([jnp.ones((N_PROMPT, D_ROPE), F32), jnp.tile(cos, (DEC_BATCH, 1))], axis=0)
    sin = jnp.concatenate([jnp.zeros((N_PROMPT, D_ROPE), F32), jnp.tile(sin, (DEC_BATCH, 1))], axis=0)
    return cos, sin


def _even_mixer(j, h, lb_all, cos, sin, cache_ckv, cache_krope, state_hgrn,
                ev_w_in, hgrn_norm, mla_q_norm, mla_w_q_up, mla_kv_norm, mla_w_kv_up):
    w_in = ev_w_in[j]
    n_a = 5 * H_A * DK_A
    w_kr = w_in[:, n_a + Q_RANK + KV_RANK:]
    w_tail = jnp.concatenate([w_in[:, n_a:n_a + Q_RANK + KV_RANK], jnp.tile(w_kr, (1, 4)),
                              jnp.tile(_rot_cols(w_kr), (1, 4))], axis=1).astype(BF16)
    slabs = _mm_split(h, w_in[:, :n_a].astype(BF16), H_A * DK_A, name="hgrn_proj")
    tail = _mm(h, w_tail, F32, tn=T_N, name="mla_proj")

    o_fp, o_bp, o_fl, o_bl, s_h = _hgrn_scans(slabs, lb_all[j], state_hgrn[:, j])
    o_a = _hgrn_post(o_fp, o_bp, o_fl, o_bl, slabs, hgrn_norm[j])

    wq = mla_w_q_up[j].reshape(Q_RANK, H_B, D_NOPE + D_ROPE)
    wq_n = wq[:, :, :D_NOPE].reshape(Q_RANK, H_B * D_NOPE)
    wq_r = wq[:, :, D_NOPE:]
    wq_aug = jnp.concatenate([wq_n, wq_r.reshape(Q_RANK, -1), _rot_cols(wq_r).reshape(Q_RANK, -1)], axis=1).astype(BF16)
    qn, qr = _q_up(tail, mla_q_norm[j], wq_aug, jnp.tile(cos, (1, H_B)), jnp.tile(sin, (1, H_B)))
    wkv = mla_w_kv_up[j].reshape(KV_RANK, H_B, D_NOPE + DV_B)
    wkv_aug = jnp.concatenate([wkv[:, :, :D_NOPE].reshape(KV_RANK, -1), wkv[:, :, D_NOPE:].reshape(KV_RANK, -1)],
                              axis=1).astype(BF16)
    ckv_n, kn, vv, kr = _kv_up(tail, mla_kv_norm[j], wkv_aug, jnp.tile(cos, (1, 4)), jnp.tile(sin, (1, 4)))
    kv_ctx = _mm(cache_ckv[:, j].reshape(DEC_BATCH * PAST_LEN, KV_RANK), wkv_aug, BF16, name="kv_ctx")
    kn_c = kv_ctx[:, :512].reshape(DEC_BATCH, PAST_LEN, 512)
    v_c = kv_ctx[:, 512:].reshape(DEC_BATCH, PAST_LEN, 512)
    kr_c = jnp.tile(cache_krope[:, j], (1, 1, 4)).astype(BF16)

    def prompt(x):
        return x[:N_PROMPT].reshape(1, SEQ, BATCH * x.shape[-1])

    def latent(x):
        return x[N_PROMPT:].reshape(DEC_BATCH, DEC_SEQ, x.shape[-1])

    o_p = _attention(prompt(qn), prompt(qr), prompt(kn), prompt(kr), prompt(vv), BATCH, True)
    o_s = _attention(latent(qn), latent(qr),
                     jnp.concatenate([kn_c, latent(kn)], axis=1),
                     jnp.concatenate([kr_c, latent(kr)], axis=1),
                     jnp.concatenate([v_c, latent(vv)], axis=1), DEC_BATCH, False)
    o_att = jnp.concatenate([o_p.reshape(N_PROMPT, 512), o_s.reshape(N_SAMPLE, 512)], axis=0)

    new_ckv = jnp.transpose(ckv_n[:N_PROMPT].reshape(SEQ, BATCH, KV_RANK), (1, 0, 2))
    new_krope = jnp.transpose(tail[:N_PROMPT, T_KR:T_KR + D_ROPE].reshape(SEQ, BATCH, D_ROPE), (1, 0, 2))
    return o_a, o_att, (s_h, new_ckv, new_krope)


def _odd_mixer(j, x, g0, sc, sh, state_rwkv, rw_mu, rw_w_r, rw_w_k, rw_w_v, rw_w0, rw_w1, rw_w2, rw_a0, rw_a1, rw_a2,
               rw_g1, rw_g2, rw_k_k, rw_k_a, rw_r_k, rw_ln_w, rw_ln_b):
    xr, xw, xk, xv, xa, xg = _odd_pre(x, g0, sc, sh, rw_mu[j])
    r = _mm(xr, rw_w_r[j].astype(BF16), name="rwkv_r")
    k = _mm(xk, rw_w_k[j].astype(BF16), name="rwkv_k")
    v = _mm(xv, rw_w_v[j].astype(BF16), name="rwkv_v")
    a = _lora(xa, rw_a1[j], rw_a2[j], rw_a0[j], _identity, jax.nn.sigmoid)
    g = _lora(xg, rw_g1[j], rw_g2[j], jnp.zeros((D_MODEL,), F32), jax.nn.sigmoid, _identity)
    w_f = _lora(xw, rw_w1[j, 0], rw_w2[j, 0], rw_w0[j, 0], jnp.tanh, _decay)
    w_b = _lora(xw, rw_w1[j, 1], rw_w2[j, 1], rw_w0[j, 1], jnp.tanh, _decay)
    kk, k2 = _rwkv_prep(k, a, rw_k_k[j], rw_k_a[j])
    y_fp, y_bp, y_fl, y_bl, s_r = _rwkv_scans(r, w_f, w_b, k2, v, kk, a, state_rwkv[:, j])
    mix = _rwkv_post(y_fp, y_bp, y_fl, y_bl, r, k2, v, g, rw_r_k[j], rw_ln_w[j], rw_ln_b[j])
    return mix, s_r


def kernel(x_prompt, x_sample, cache_ckv, cache_krope, state_hgrn, state_rwkv, c, c_ctx, ada_w, ada_b, norm_gains, ev_w_in, hgrn_lb_logits, hgrn_norm, mla_q_norm, mla_w_q_up, mla_kv_norm, mla_w_kv_up, ev_w_out, rw_mu, rw_w_r, rw_w_k, rw_w_v, rw_w_o, rw_w0, rw_w1, rw_w2, rw_a0, rw_a1, rw_a2, rw_g1, rw_g2, rw_k_k, rw_k_a, rw_r_k, rw_ln_w, rw_ln_b, ffn_w_gate, ffn_w_up, ffn_w_down, moe_router, moe_w_gate, moe_w_up, moe_w_down):
    lb_all = jnp.cumsum(jax.nn.softmax(hgrn_lb_logits.astype(F32), axis=0), axis=0)
    lb_all = lb_all - lb_all[:1]
    cos, sin = _rope_tables()

    cond8 = jnp.zeros((8, D_MODEL), F32).at[0].set(c_ctx).at[1:1 + DEC_BATCH].set(c)
    mod = _modulation(cond8, ada_w, ada_b)
    tile_row = np.array([0] * BATCH + [1 + t // 4 for t in range(N_TILES - BATCH)])
    modt = mod[:, tile_row, :].reshape(DEPTH, N_TILES, 6, 1, D_MODEL)

    x = jnp.concatenate([jnp.transpose(x_prompt, (1, 0, 2)).reshape(N_PROMPT, D_MODEL),
                         x_sample.reshape(N_SAMPLE, D_MODEL)], axis=0)
    ckv_l, krope_l, hgrn_l, rwkv_l = [], [], [], []
    for l in range(DEPTH):
        j = l // 2
        sh_m, sc_m, gt_m, sh_f, sc_f, gt_f = [modt[l, :, i] for i in range(6)]
        if l % 2 == 0:
            h = _norm_mod(x, norm_gains[l, 0], sc_m, sh_m)
            o_a, o_att, (s_h, n_ckv, n_kr) = _even_mixer(j, h, lb_all, cos, sin, cache_ckv, cache_krope, state_hgrn,
                                                        ev_w_in, hgrn_norm, mla_q_norm, mla_w_q_up, mla_kv_norm,
                                                        mla_w_kv_up)
            hgrn_l.append(s_h)
            ckv_l.append(n_ckv)
            krope_l.append(n_kr)
            x = _mm2_resid(o_a, o_att, ev_w_out[j].astype(BF16), x, norm_gains[l, 1], gt_m)
            h = _norm_mod(x, norm_gains[l, 2], sc_f, sh_f)
            hid = _swiglu(h, ffn_w_gate[j].astype(BF16), ffn_w_up[j].astype(BF16))
            x = _mm_resid(hid, ffn_w_down[j].astype(BF16), x, norm_gains[l, 3], gt_f)
        else:
            mix, s_r = _odd_mixer(j, x, norm_gains[l, 0], sc_m, sh_m, state_rwkv, rw_mu, rw_w_r, rw_w_k, rw_w_v,
                                  rw_w0, rw_w1, rw_w2, rw_a0, rw_a1, rw_a2, rw_g1, rw_g2, rw_k_k, rw_k_a, rw_r_k,
                                  rw_ln_w, rw_ln_b)
            rwkv_l.append(s_r)
            x = _mm_resid(mix, rw_w_o[j].astype(BF16), x, norm_gains[l, 1], gt_m)
            h, route = _norm_mod_router(x, norm_gains[l, 2], sc_f, sh_f, moe_router[j])
            dest, src_tok, tile_e, n_used = _moe_plan(route)
            h_sorted = _row_gather(src_tok, h, MOE_ROWS, MOE_TM)
            hid = _moe_swiglu(tile_e, n_used, h_sorted, moe_w_gate, moe_w_up, j)
            y_sorted = _moe_down(tile_e, n_used, hid, moe_w_down, j)
            x = _moe_combine(dest, y_sorted, route, x, norm_gains[l, 3], gt_f)

    y_prompt = jnp.transpose(x[:N_PROMPT].reshape(SEQ, BATCH, D_MODEL), (1, 0, 2))
    y_sample = x[N_PROMPT:].reshape(DEC_BATCH, DEC_SEQ, D_MODEL)
    return (y_prompt, y_sample, jnp.stack(ckv_l, axis=1), jnp.stack(krope_l, axis=1),
            jnp.stack(hgrn_l, axis=1), jnp.stack(rwkv_l, axis=1))
```

```python
import functools
from typing import NamedTuple

import numpy as np
import jax
import jax.numpy as jnp
from jax import lax
from jax.experimental import pallas as pl
from jax.experimental.pallas import tpu as pltpu

D_MODEL = 1024
BATCH = 16
SEQ = 256
DEPTH = 4
DEC_BATCH = 2
DEC_SEQ = 1024
PAST_LEN = 256
GRID_W = 64
H_A = 8
DK_A = 64
DV_A = 64
H_B = 8
Q_RANK = 256
KV_RANK = 128
D_NOPE = 64
D_ROPE = 32
DV_B = 64
ROPE_BASE = 10000.0
H_C = 16
N_C = 64
D_FF = 2816
N_EXPERTS = 8
EPS = 1e-6
GN_EPS = 64e-5

F32 = jnp.float32
BF16 = jnp.bfloat16
HIGHEST = lax.Precision.HIGHEST

N_PROMPT = BATCH * SEQ
N_SAMPLE = DEC_BATCH * DEC_SEQ
N_TOK = N_PROMPT + N_SAMPLE
ROW_TILE = 256
N_TILES = N_TOK // ROW_TILE
LANES = 128
HEAD = 64
VMEM_LIMIT = 48 * 1024 * 1024
HALO = 16
PAIRS = 4
EW_TILE = 512
EW_TILES = N_TOK // EW_TILE
EW_PROMPT = N_PROMPT // EW_TILE
RESID_TILES = 4
MM_TM = 1024
MM_TN = 1024
FF_TN = 1408
MOE_TM = 512
MOE_ROWS = 2 * N_TOK + N_EXPERTS * MOE_TM
MOE_TILES = MOE_ROWS // MOE_TM

T_CQ, T_CKV, T_KR, T_KROT, T_N = 0, 256, 384, 512, 640


def _cparams(sem):
    return pltpu.CompilerParams(dimension_semantics=sem, vmem_limit_bytes=VMEM_LIMIT)


def _mod_kernel(c_ref, w_ref, b_ref, o_ref):
    c = c_ref[...]
    s = (c * jax.nn.sigmoid(c)).astype(BF16)
    o_ref[0] = jnp.dot(s, w_ref[0].astype(BF16), preferred_element_type=F32) + b_ref[0]


def _modulation(cond8, ada_w, ada_b):
    tn = 768
    return pl.pallas_call(
        _mod_kernel,
        out_shape=jax.ShapeDtypeStruct((DEPTH, 8, 6 * D_MODEL), F32),
        grid=(DEPTH, 6 * D_MODEL // tn),
        in_specs=[pl.BlockSpec((8, D_MODEL), lambda l, n: (0, 0)),
                  pl.BlockSpec((1, D_MODEL, tn), lambda l, n: (l, 0, n)),
                  pl.BlockSpec((1, 1, tn), lambda l, n: (l, 0, n))],
        out_specs=pl.BlockSpec((1, 8, tn), lambda l, n: (l, 0, n)),
        compiler_params=_cparams(("parallel", "parallel")),
        name="modulation",
    )(cond8, ada_w, ada_b.reshape(DEPTH, 1, 6 * D_MODEL))


def _rms(x, g):
    return x * lax.rsqrt(jnp.mean(x * x, axis=-1, keepdims=True) + EPS) * g


def _norm_mod_kernel(x_ref, g_ref, sc_ref, sh_ref, o_ref):
    h = _rms(x_ref[...], g_ref[...]) * (1.0 + sc_ref[0]) + sh_ref[0]
    o_ref[...] = h.astype(o_ref.dtype)


def _tile_spec():
    return pl.BlockSpec((1, 1, D_MODEL), lambda i: (i, 0, 0))


def _row_spec(width=D_MODEL, col=0):
    return pl.BlockSpec((ROW_TILE, width), lambda i: (i, col))


def _vec_spec(width=D_MODEL):
    return pl.BlockSpec((1, width), lambda i: (0, 0))


def _norm_mod(x, g, sc, sh):
    return pl.pallas_call(
        _norm_mod_kernel,
        out_shape=jax.ShapeDtypeStruct((N_TOK, D_MODEL), BF16),
        grid=(N_TILES,),
        in_specs=[_row_spec(), _vec_spec(), _tile_spec(), _tile_spec()],
        out_specs=_row_spec(),
        compiler_params=_cparams(("parallel",)),
        name="norm_mod",
    )(x, g.reshape(1, D_MODEL), sc, sh)


def _norm_mod_router_kernel(x_ref, g_ref, sc_ref, sh_ref, wr_ref, o_ref, route_ref):
    h = _rms(x_ref[...], g_ref[...]) * (1.0 + sc_ref[0]) + sh_ref[0]
    o_ref[...] = h
    logits = jnp.dot(h, wr_ref[...], precision=HIGHEST, preferred_element_type=F32)
    lane = lax.broadcasted_iota(jnp.int32, logits.shape, 1).astype(F32)
    neg = jnp.float32(-jnp.inf)
    lg = jnp.where(lane < N_EXPERTS, logits, neg)
    m1 = jnp.max(lg, axis=-1, keepdims=True)
    i1 = jnp.min(jnp.where(lg == m1, lane, float(LANES)), axis=-1, keepdims=True)
    lg2 = jnp.where(lane == i1, neg, lg)
    m2 = jnp.max(lg2, axis=-1, keepdims=True)
    i2 = jnp.min(jnp.where(lg2 == m2, lane, float(LANES)), axis=-1, keepdims=True)
    e = jnp.exp(m2 - m1)
    w1 = 1.0 / (1.0 + e)
    w2 = e * w1
    route_ref[...] = jnp.where(lane == 0.0, i1, jnp.where(lane == 1.0, i2, jnp.where(lane == 2.0, w1,
                                                                                    jnp.where(lane == 3.0, w2, 0.0))))


def _norm_mod_router(x, g, sc, sh, w_router):
    wr = jnp.pad(w_router, ((0, 0), (0, LANES - N_EXPERTS)))
    return pl.pallas_call(
        _norm_mod_router_kernel,
        out_shape=(jax.ShapeDtypeStruct((N_TOK, D_MODEL), F32),
                   jax.ShapeDtypeStruct((N_TOK, LANES), F32)),
        grid=(N_TILES,),
        in_specs=[_row_spec(), _vec_spec(), _tile_spec(), _tile_spec(),
                  pl.BlockSpec((D_MODEL, LANES), lambda i: (0, 0))],
        out_specs=(_row_spec(), _row_spec(LANES)),
        compiler_params=_cparams(("parallel",)),
        name="norm_mod_router",
    )(x, g.reshape(1, D_MODEL), sc, sh, wr)


def _mm_kernel(a_ref, w_ref, o_ref):
    o_ref[...] = jnp.dot(a_ref[...].astype(BF16), w_ref[...], preferred_element_type=F32).astype(o_ref.dtype)


def _mm(a, w, out_dtype=F32, tm=MM_TM, tn=MM_TN, name="mm"):
    m, k = a.shape
    n = w.shape[1]
    tm, tn = min(tm, m), min(tn, n)
    return pl.pallas_call(
        _mm_kernel,
        out_shape=jax.ShapeDtypeStruct((m, n), out_dtype),
        grid=(n // tn, m // tm),
        in_specs=[pl.BlockSpec((tm, k), lambda j, i: (i, 0)),
                  pl.BlockSpec((k, tn), lambda j, i: (0, j))],
        out_specs=pl.BlockSpec((tm, tn), lambda j, i: (i, j)),
        compiler_params=_cparams(("parallel", "parallel")),
        name=name,
    )(a, w)


def _mm_split(a, w, tn, tm=MM_TM, name="mm_split"):
    m, k = a.shape
    p = w.shape[1] // tn
    return pl.pallas_call(
        _mm_kernel,
        out_shape=jax.ShapeDtypeStruct((p, m, tn), F32),
        grid=(p, m // tm),
        in_specs=[pl.BlockSpec((tm, k), lambda j, i: (i, 0)),
                  pl.BlockSpec((k, tn), lambda j, i: (0, j))],
        out_specs=pl.BlockSpec((None, tm, tn), lambda j, i: (j, i, 0)),
        compiler_params=_cparams(("parallel", "parallel")),
        name=name,
    )(a, w)


def _resid_store(y, x_ref, g_ref, gt_ref, o_ref):
    g = g_ref[...]
    for t in range(RESID_TILES):
        rows = slice(t * ROW_TILE, (t + 1) * ROW_TILE)
        o_ref[rows, :] = x_ref[rows, :] + gt_ref[t] * _rms(y[rows], g)


def _resid_rows(width=D_MODEL):
    return pl.BlockSpec((RESID_TILES * ROW_TILE, width), lambda i: (i, 0))


def _resid_gate_spec():
    return pl.BlockSpec((RESID_TILES, 1, D_MODEL), lambda i: (i, 0, 0))


def _mm_resid_kernel(a_ref, w_ref, x_ref, g_ref, gt_ref, o_ref):
    y = jnp.dot(a_ref[...], w_ref[...], preferred_element_type=F32)
    _resid_store(y, x_ref, g_ref, gt_ref, o_ref)


def _mm_resid(a, w, x, g, gt):
    k = a.shape[1]
    return pl.pallas_call(
        _mm_resid_kernel,
        out_shape=jax.ShapeDtypeStruct((N_TOK, D_MODEL), F32),
        grid=(N_TILES // RESID_TILES,),
        in_specs=[_resid_rows(k), pl.BlockSpec((k, D_MODEL), lambda i: (0, 0)),
                  _resid_rows(), _vec_spec(), _resid_gate_spec()],
        out_specs=_resid_rows(),
        compiler_params=_cparams(("parallel",)),
        name="mm_resid",
    )(a, w, x, g.reshape(1, D_MODEL), gt)


def _mm2_resid_kernel(a1_ref, a2_ref, w_ref, x_ref, g_ref, gt_ref, o_ref):
    k1 = a1_ref.shape[1]
    y = (jnp.dot(a1_ref[...], w_ref[:k1], preferred_element_type=F32)
         + jnp.dot(a2_ref[...], w_ref[k1:], preferred_element_type=F32))
    _resid_store(y, x_ref, g_ref, gt_ref, o_ref)


def _mm2_resid(a1, a2, w, x, g, gt):
    k1, k2 = a1.shape[1], a2.shape[1]
    return pl.pallas_call(
        _mm2_resid_kernel,
        out_shape=jax.ShapeDtypeStruct((N_TOK, D_MODEL), F32),
        grid=(N_TILES // RESID_TILES,),
        in_specs=[_resid_rows(k1), _resid_rows(k2), pl.BlockSpec((k1 + k2, D_MODEL), lambda i: (0, 0)),
                  _resid_rows(), _vec_spec(), _resid_gate_spec()],
        out_specs=_resid_rows(),
        compiler_params=_cparams(("parallel",)),
        name="mm2_resid",
    )(a1, a2, w, x, g.reshape(1, D_MODEL), gt)


def _swiglu_kernel(a_ref, wg_ref, wu_ref, o_ref):
    a = a_ref[...]
    g = jnp.dot(a, wg_ref[...], preferred_element_type=F32)
    u = jnp.dot(a, wu_ref[...], preferred_element_type=F32)
    o_ref[...] = ((g * jax.nn.sigmoid(g)) * u).astype(o_ref.dtype)


def _swiglu(a, wg, wu, tm=MM_TM, tn=FF_TN):
    f = wg.shape[1]
    m = a.shape[0]
    return pl.pallas_call(
        _swiglu_kernel,
        out_shape=jax.ShapeDtypeStruct((m, f), BF16),
        grid=(f // tn, m // tm),
        in_specs=[pl.BlockSpec((tm, D_MODEL), lambda j, i: (i, 0)),
                  pl.BlockSpec((D_MODEL, tn), lambda j, i: (0, j)),
                  pl.BlockSpec((D_MODEL, tn), lambda j, i: (0, j))],
        out_specs=pl.BlockSpec((tm, tn), lambda j, i: (i, j)),
        compiler_params=_cparams(("parallel", "parallel")),
        name="swiglu",
    )(a, wg, wu)


def _moe_plan(route):
    e_flat = route[:, :2].astype(jnp.int32).reshape(-1)
    onehot = (e_flat[:, None] == jnp.arange(N_EXPERTS, dtype=jnp.int32)[None, :]).astype(jnp.int32)
    pos = jnp.cumsum(onehot, axis=0) - onehot
    padded = (jnp.sum(onehot, axis=0) + MOE_TM - 1) // MOE_TM * MOE_TM
    ends = jnp.cumsum(padded)
    dest = jnp.sum(onehot * ((ends - padded)[None, :] + pos), axis=1)
    token = jnp.arange(2 * N_TOK, dtype=jnp.int32) // 2
    src_tok = jnp.zeros((MOE_ROWS,), jnp.int32).at[dest].set(token)
    tile_start = jnp.arange(MOE_TILES, dtype=jnp.int32) * MOE_TM
    tile_e = jnp.sum((tile_start[:, None] >= ends[None, :]).astype(jnp.int32), axis=1)
    tile_e = jnp.minimum(tile_e, N_EXPERTS - 1)
    n_used = (ends[-1:] // MOE_TM).astype(jnp.int32)
    return dest, src_tok, tile_e, n_used


def _row_gather_kernel(idx_ref, src_ref, o_ref, rows_s):
    rows = o_ref.shape[0]
    base = pl.program_id(0) * rows

    def copy8(c, carry):
        for u in range(8):
            r = c * 8 + u
            rows_s[pl.ds(r, 1), :] = src_ref[pl.ds(idx_ref[base + r], 1), :]
        return carry

    lax.fori_loop(0, rows // 8, copy8, 0)
    o_ref[...] = rows_s[...].astype(o_ref.dtype)


def _row_gather(idx, src, n_rows, tm):
    n_src, w = src.shape
    return pl.pallas_call(
        _row_gather_kernel,
        out_shape=jax.ShapeDtypeStruct((n_rows, w), BF16),
        grid_spec=pltpu.PrefetchScalarGridSpec(
            num_scalar_prefetch=1, grid=(n_rows // tm,),
            in_specs=[pl.BlockSpec((n_src, w), lambda i, idx_ref: (0, 0), pipeline_mode=pl.Buffered(1))],
            out_specs=pl.BlockSpec((tm, w), lambda i, idx_ref: (i, 0)),
            scratch_shapes=[pltpu.VMEM((tm, w), F32)]),
        compiler_params=_cparams(("arbitrary",)),
        name="moe_gather",
    )(idx, src)


def _expert_changed(te_ref, i):
    return jnp.logical_or(i == 0, te_ref[i] != te_ref[jnp.maximum(i - 1, 0)])


def _moe_swiglu_kernel(te_ref, nu_ref, a_ref, wg_ref, wu_ref, o_ref, wg_s, wu_s):
    i = pl.program_id(1)

    @pl.when(_expert_changed(te_ref, i))
    def _():
        wg_s[...] = wg_ref[...].astype(BF16)
        wu_s[...] = wu_ref[...].astype(BF16)

    @pl.when(i < nu_ref[0])
    def _():
        a = a_ref[...]
        g = jnp.dot(a, wg_s[...], preferred_element_type=F32)
        u = jnp.dot(a, wu_s[...], preferred_element_type=F32)
        o_ref[...] = ((g * jax.nn.sigmoid(g)) * u).astype(o_ref.dtype)

    @pl.when(i >= nu_ref[0])
    def _():
        o_ref[...] = jnp.zeros_like(o_ref)


def _moe_swiglu(tile_e, n_used, a, wg, wu, layer, tn=FF_TN):
    f = wg.shape[3]
    w_spec = pl.BlockSpec((None, None, D_MODEL, tn), lambda j, i, te, nu: (layer, te[i], 0, j))
    return pl.pallas_call(
        _moe_swiglu_kernel,
        out_shape=jax.ShapeDtypeStruct((MOE_ROWS, f), BF16),
        grid_spec=pltpu.PrefetchScalarGridSpec(
            num_scalar_prefetch=2, grid=(f // tn, MOE_TILES),
            in_specs=[pl.BlockSpec((MOE_TM, D_MODEL), lambda j, i, te, nu: (i, 0)), w_spec, w_spec],
            out_specs=pl.BlockSpec((MOE_TM, tn), lambda j, i, te, nu: (i, j)),
            scratch_shapes=[pltpu.VMEM((D_MODEL, tn), BF16), pltpu.VMEM((D_MODEL, tn), BF16)]),
        compiler_params=_cparams(("parallel", "arbitrary")),
        name="moe_swiglu",
    )(tile_e, n_used, a, wg, wu)


def _moe_down_kernel(te_ref, nu_ref, h_ref, wd_ref, o_ref, wd_s):
    i = pl.program_id(0)

    @pl.when(_expert_changed(te_ref, i))
    def _():
        wd_s[...] = wd_ref[...].astype(BF16)

    @pl.when(i < nu_ref[0])
    def _():
        o_ref[...] = jnp.dot(h_ref[...], wd_s[...], preferred_element_type=F32)

    @pl.when(i >= nu_ref[0])
    def _():
        o_ref[...] = jnp.zeros_like(o_ref)


def _moe_down(tile_e, n_used, hid, wd, layer):
    f = wd.shape[2]
    return pl.pallas_call(
        _moe_down_kernel,
        out_shape=jax.ShapeDtypeStruct((MOE_ROWS, D_MODEL), F32),
        grid_spec=pltpu.PrefetchScalarGridSpec(
            num_scalar_prefetch=2, grid=(MOE_TILES,),
            in_specs=[pl.BlockSpec((MOE_TM, f), lambda i, te, nu: (i, 0)),
                      pl.BlockSpec((None, None, f, D_MODEL), lambda i, te, nu: (layer, te[i], 0, 0))],
            out_specs=pl.BlockSpec((MOE_TM, D_MODEL), lambda i, te, nu: (i, 0)),
            scratch_shapes=[pltpu.VMEM((f, D_MODEL), BF16)]),
        compiler_params=_cparams(("arbitrary",)),
        name="moe_down",
    )(tile_e, n_used, hid, wd)


def _moe_combine_kernel(dest_ref, y_ref, route_ref, x_ref, g_ref, gt_ref, o_ref, buf, sem):
    base = pl.program_id(0) * ROW_TILE

    def issue(r, carry):
        a = 2 * (base + r)
        pltpu.make_async_copy(y_ref.at[dest_ref[a]], buf.at[0, r], sem).start()
        pltpu.make_async_copy(y_ref.at[dest_ref[a + 1]], buf.at[1, r], sem).start()
        return carry

    lax.fori_loop(0, ROW_TILE, issue, 0)

    def wait(r, carry):
        pltpu.make_async_copy(y_ref.at[0], buf.at[0, 0], sem).wait()
        pltpu.make_async_copy(y_ref.at[0], buf.at[0, 0], sem).wait()
        return carry

    lax.fori_loop(0, ROW_TILE, wait, 0)
    route = route_ref[...]
    ff = route[:, 2:3] * buf[0] + route[:, 3:4] * buf[1]
    o_ref[...] = x_ref[...] + gt_ref[0] * _rms(ff, g_ref[...])


def _moe_combine(dest, y_sorted, route, x, g, gt):
    return pl.pallas_call(
        _moe_combine_kernel,
        out_shape=jax.ShapeDtypeStruct((N_TOK, D_MODEL), F32),
        grid_spec=pltpu.PrefetchScalarGridSpec(
            num_scalar_prefetch=1, grid=(N_TILES,),
            in_specs=[pl.BlockSpec(memory_space=pl.ANY),
                      pl.BlockSpec((ROW_TILE, LANES), lambda i, d: (i, 0)),
                      pl.BlockSpec((ROW_TILE, D_MODEL), lambda i, d: (i, 0)),
                      pl.BlockSpec((1, D_MODEL), lambda i, d: (0, 0)),
                      pl.BlockSpec((1, 1, D_MODEL), lambda i, d: (i, 0, 0))],
            out_specs=pl.BlockSpec((ROW_TILE, D_MODEL), lambda i, d: (i, 0)),
            scratch_shapes=[pltpu.VMEM((2, ROW_TILE, D_MODEL), F32), pltpu.SemaphoreType.DMA(())]),
        compiler_params=_cparams(("arbitrary",)),
        name="moe_combine",
    )(dest, y_sorted, route, x, g.reshape(1, D_MODEL), gt)


def _lora_kernel(x_ref, w1_ref, w2_ref, b_ref, o_ref, *, mid, out):
    t = mid(jnp.dot(x_ref[...], w1_ref[...], preferred_element_type=F32))
    o_ref[...] = out(jnp.dot(t.astype(BF16), w2_ref[...], preferred_element_type=F32) + b_ref[...])


def _lora(x, w1, w2, b, mid, out, tm=MM_TM):
    m = x.shape[0]
    r = w1.shape[1]
    return pl.pallas_call(
        functools.partial(_lora_kernel, mid=mid, out=out),
        out_shape=jax.ShapeDtypeStruct((m, D_MODEL), F32),
        grid=(m // tm,),
        in_specs=[pl.BlockSpec((tm, D_MODEL), lambda i: (i, 0)),
                  pl.BlockSpec((D_MODEL, r), lambda i: (0, 0)),
                  pl.BlockSpec((r, D_MODEL), lambda i: (0, 0)),
                  pl.BlockSpec((1, D_MODEL), lambda i: (0, 0))],
        out_specs=pl.BlockSpec((tm, D_MODEL), lambda i: (i, 0)),
        compiler_params=_cparams(("parallel",)),
        name="lora",
    )(x, w1.astype(BF16), w2.astype(BF16), b.reshape(1, D_MODEL))


def _identity(x):
    return x


def _decay(wr):
    return jnp.exp(-float(np.exp(-0.5)) * jax.nn.sigmoid(wr))


def _block_ones():
    i = np.arange(LANES)
    return jnp.asarray((i[:, None] // HEAD == i[None, :] // HEAD).astype(np.float32))


def _head_sum(x, bo):
    hi = x.astype(BF16)
    r1 = x - hi.astype(F32)
    mid = r1.astype(BF16)
    lo = (r1 - mid.astype(F32)).astype(BF16)
    bo16 = bo.astype(BF16)
    parts = [sum(jnp.dot(t[:, c * LANES:(c + 1) * LANES], bo16, preferred_element_type=F32) for t in (lo, mid, hi))
             for c in range(x.shape[1] // LANES)]
    return jnp.concatenate(parts, axis=-1)


class _ScanMode(NamedTuple):
    prompt: bool
    groups: int
    rep: int
    r2: int
    tb: int
    pipelined: bool = True


def _scan_gather(mode, src_ref, fwd_ref, bwd_ref, s):
    sb = mode.tb - 1 - s
    if mode.prompt:
        for d, (ref, tt) in enumerate(((fwd_ref, s), (bwd_ref, sb))):
            z = ref[pl.ds(pl.multiple_of(tt * BATCH, BATCH), BATCH), :]
            for hp in range(PAIRS):
                r0 = d * HEAD + hp * BATCH
                src_ref[r0:r0 + BATCH, :] = z[:, hp * LANES:(hp + 1) * LANES]
        return src_ref[...].T
    unit = 4 * mode.r2
    for q in range(mode.rep):
        for d, (ref, tt) in enumerate(((fwd_ref, s), (bwd_ref, sb))):
            for b in range(DEC_BATCH):
                r0 = q * unit + (2 * d + b) * mode.r2
                src_ref[r0:r0 + mode.r2, 0:HEAD] = ref[b, tt]
    return src_ref[...].T[0:HEAD]


def _scan_value_slab(mode, vt):
    vsub = HEAD // mode.rep
    out = vt[0:vsub]
    if mode.rep > 1:
        lane = lax.broadcasted_iota(jnp.int32, (vsub, LANES), 1)
        for q in range(1, mode.rep):
            out = jnp.where(lane >= q * (LANES // mode.rep), vt[q * vsub:(q + 1) * vsub], out)
    return out


def _scan_scatter(mode, src_ref, ys_ref, yf_ref, yb_ref, s):
    sb = mode.tb - 1 - s
    if mode.prompt:
        tr = jnp.concatenate([ys_ref[s, 0], ys_ref[s, 1]], axis=0).T
        for d, (ref, tt) in enumerate(((yf_ref, s), (yb_ref, sb))):
            row0 = pl.multiple_of(tt * BATCH, BATCH)
            for hp in range(PAIRS):
                r0 = d * HEAD + hp * BATCH
                ref[pl.ds(row0, BATCH), hp * LANES:(hp + 1) * LANES] = tr[r0:r0 + BATCH]
        return
    vsub = HEAD // mode.rep
    y = ys_ref[s, 0]
    for q in range(mode.rep):
        src_ref[q * vsub:(q + 1) * vsub, :] = y
    tr = src_ref[...].T
    unit = 4 * mode.r2
    out = tr[0:unit]
    lane = lax.broadcasted_iota(jnp.int32, (unit, LANES), 1)
    for q in range(1, mode.rep):
        out = jnp.where(lane >= q * vsub, tr[q * unit:(q + 1) * unit], out)
    for d, (ref, tt) in enumerate(((yf_ref, s), (yb_ref, sb))):
        for b in range(DEC_BATCH):
            r0 = (2 * d + b) * mode.r2
            ref[b, tt] = out[r0:r0 + mode.r2, 0:HEAD]


def _n_acc(vsub):
    return 1 if vsub == HEAD else 2


def _scan_init(mode, t_axis, s_ref, s0_ref, src_ref):
    @pl.when(pl.program_id(t_axis) == 0)
    def _():
        if s0_ref is None:
            s_ref[...] = jnp.zeros_like(s_ref)
        else:
            s_ref[...] = s0_ref[...]

    if not mode.prompt:
        src_ref[...] = jnp.zeros_like(src_ref)


def _hgrn_scan_kernel(*refs, mode, has_s0, t_axis):
    qf, qb, xf, xb, vf, vb, lb_ref = refs[:7]
    refs = refs[7:]
    s0_ref = None
    if has_s0:
        s0_ref, refs = refs[0], refs[1:]
    of_ref, ob_ref, s_ref, src_ref, q_t, f_t, k_t, v_t, ys_ref = refs
    vsub = HEAD // mode.rep
    n_acc = _n_acc(vsub)
    _scan_init(mode, t_axis, s_ref, s0_ref, src_ref)
    lb = lb_ref[...]

    def gather(s):
        q_t[s] = _scan_gather(mode, src_ref, qf, qb, s)
        x = _scan_gather(mode, src_ref, xf, xb, s)
        f_t[s] = lb + (1.0 - lb) * jax.nn.sigmoid(x)
        k_t[s] = (1.0 - lb) * jax.nn.sigmoid(-x)
        v_t[s] = _scan_gather(mode, src_ref, vf, vb, s)

    def step(s):
        for g in range(mode.groups):
            r0 = g * HEAD
            vv = _scan_value_slab(mode, v_t[s, r0:r0 + HEAD, :])
            acc = [jnp.zeros((vsub, LANES), F32) for _ in range(n_acc)]
            for d in range(HEAD):
                r = r0 + d
                sn = s_ref[g, d] * f_t[s, r:r + 1, :] + vv * k_t[s, r:r + 1, :]
                s_ref[g, d] = sn
                acc[d % n_acc] = acc[d % n_acc] + sn * q_t[s, r:r + 1, :]
            ys_ref[s, g] = sum(acc[1:], acc[0])

    def scatter(s):
        _scan_scatter(mode, src_ref, ys_ref, of_ref, ob_ref, s)

    _scan_run(mode, ys_ref, gather, step, scatter)


def _scan_run(mode, ys_ref, gather, step, scatter):
    if not mode.pipelined:
        for phase in (gather, step, scatter):
            lax.fori_loop(0, mode.tb, lambda s, carry, phase=phase: (phase(s), carry)[1], 0)
        return
    last = mode.tb - 1
    gather(0)
    ys_ref[0] = jnp.zeros(ys_ref.shape[1:], F32)

    def body(s, carry):
        gather(jnp.minimum(s + 1, last))
        scatter(jnp.maximum(s - 1, 0))
        step(s)
        return carry

    lax.fori_loop(0, mode.tb, body, 0)
    scatter(last)


def _rwkv_scan_kernel(*refs, mode, has_s0, t_axis):
    ins = refs[:12]
    refs = refs[12:]
    s0_ref = None
    if has_s0:
        s0_ref, refs = refs[0], refs[1:]
    yf_ref, yb_ref, s_ref, src_ref, r_t, w_t, k_t, v_t, kk_t, a_t, ys_ref = refs
    vsub = HEAD // mode.rep
    n_acc = _n_acc(vsub)
    _scan_init(mode, t_axis, s_ref, s0_ref, src_ref)

    def gather(s):
        for i, x_t in enumerate((r_t, w_t, k_t, v_t, kk_t, a_t)):
            x_t[s] = _scan_gather(mode, src_ref, ins[2 * i], ins[2 * i + 1], s)

    def step(s):
        for g in range(mode.groups):
            r0 = g * HEAD
            acc = [jnp.zeros((vsub, LANES), F32) for _ in range(n_acc)]
            for d in range(HEAD):
                acc[d % n_acc] = acc[d % n_acc] + s_ref[g, d] * kk_t[s, r0 + d:r0 + d + 1, :]
            sa = -sum(acc[1:], acc[0])
            vv = _scan_value_slab(mode, v_t[s, r0:r0 + HEAD, :])
            acc = [jnp.zeros((vsub, LANES), F32) for _ in range(n_acc)]
            for d in range(HEAD):
                r = r0 + d
                b = kk_t[s, r:r + 1, :] * a_t[s, r:r + 1, :]
                sn = s_ref[g, d] * w_t[s, r:r + 1, :] + sa * b + vv * k_t[s, r:r + 1, :]
                s_ref[g, d] = sn
                acc[d % n_acc] = acc[d % n_acc] + sn * r_t[s, r:r + 1, :]
            ys_ref[s, g] = sum(acc[1:], acc[0])

    def scatter(s):
        _scan_scatter(mode, src_ref, ys_ref, yf_ref, yb_ref, s)

    _scan_run(mode, ys_ref, gather, step, scatter)


def _scan_scratch(mode, n_streams):
    rows = mode.groups * HEAD
    vsub = HEAD // mode.rep
    return ([pltpu.VMEM((LANES, LANES), F32)]
            + [pltpu.VMEM((mode.tb, rows, LANES), F32) for _ in range(n_streams)]
            + [pltpu.VMEM((mode.tb, mode.groups, vsub, LANES), F32)])


def _hgrn_scans(slabs, lb, state0):
    mode = _ScanMode(prompt=True, groups=2, rep=1, r2=0, tb=16, pipelined=False)
    n_t = SEQ // mode.tb
    rows = mode.tb * BATCH
    width = PAIRS * LANES
    assert width == H_A * HEAD

    def spec(p, rev):
        if rev:
            return pl.BlockSpec((None, rows, width), lambda t: (p, n_t - 1 - t, 0))
        return pl.BlockSpec((None, rows, width), lambda t: (p, t, 0))

    def ospec(rev):
        if rev:
            return pl.BlockSpec((rows, width), lambda t: (n_t - 1 - t, 0))
        return pl.BlockSpec((rows, width), lambda t: (t, 0))

    lb4 = lb.reshape(2, PAIRS, 2, HEAD)
    lb_p = jnp.broadcast_to(jnp.transpose(lb4, (2, 3, 0, 1))[..., None], (2, HEAD, 2, PAIRS, BATCH))
    lb_p = lb_p.reshape(2 * HEAD, LANES)
    st_spec = pl.BlockSpec((2, HEAD, HEAD, LANES), lambda t: (0, 0, 0, 0))
    o_shape = jax.ShapeDtypeStruct((N_PROMPT, width), F32)
    view = slabs
    o_fp, o_bp, s_fin = pl.pallas_call(
        functools.partial(_hgrn_scan_kernel, mode=mode, has_s0=False, t_axis=0),
        out_shape=(o_shape, o_shape, jax.ShapeDtypeStruct((2, HEAD, HEAD, LANES), F32)),
        grid=(n_t,),
        in_specs=[spec(0, False), spec(0, True), spec(1, False), spec(2, True), spec(3, False), spec(3, True),
                  pl.BlockSpec((2 * HEAD, LANES), lambda t: (0, 0))],
        out_specs=(ospec(False), ospec(True), st_spec),
        scratch_shapes=_scan_scratch(mode, 4),
        compiler_params=_cparams(("arbitrary",)),
        name="hgrn_scan_prompt",
    )(view, view, view, view, view, view, lb_p)
    s_fin = jnp.transpose(s_fin.reshape(2, HEAD, HEAD, 2, PAIRS, BATCH), (5, 3, 4, 0, 1, 2))
    s_fin = s_fin.reshape(BATCH, 2, H_A, DK_A, DV_A)

    mode = _ScanMode(prompt=False, groups=1, rep=4, r2=H_A, tb=32)
    n_t = DEC_SEQ // mode.tb
    view = slabs[:, N_PROMPT:].reshape(5, DEC_BATCH, DEC_SEQ, H_A, HEAD)

    def spec(p, rev):
        if rev:
            return pl.BlockSpec((None, DEC_BATCH, mode.tb, H_A, HEAD), lambda t: (p, 0, n_t - 1 - t, 0, 0))
        return pl.BlockSpec((None, DEC_BATCH, mode.tb, H_A, HEAD), lambda t: (p, 0, t, 0, 0))

    def ospec(rev):
        if rev:
            return pl.BlockSpec((DEC_BATCH, mode.tb, H_A, HEAD), lambda t: (0, n_t - 1 - t, 0, 0))
        return pl.BlockSpec((DEC_BATCH, mode.tb, H_A, HEAD), lambda t: (0, t, 0, 0))

    vsub = HEAD // mode.rep
    lb_s = jnp.transpose(lb.reshape(2, H_A, HEAD), (2, 0, 1))
    lb_s = jnp.broadcast_to(lb_s[:, None, :, None, :], (HEAD, mode.rep, 2, DEC_BATCH, H_A)).reshape(HEAD, LANES)
    s0 = state0.reshape(DEC_BATCH, 2, H_A, DK_A, mode.rep, vsub)
    s0 = jnp.transpose(s0, (3, 5, 4, 1, 0, 2)).reshape(1, HEAD, vsub, LANES)
    st_spec = pl.BlockSpec((1, HEAD, vsub, LANES), lambda t: (0, 0, 0, 0))
    o_shape = jax.ShapeDtypeStruct((DEC_BATCH, DEC_SEQ, H_A, HEAD), F32)
    o_fl, o_bl, _ = pl.pallas_call(
        functools.partial(_hgrn_scan_kernel, mode=mode, has_s0=True, t_axis=0),
        out_shape=(o_shape, o_shape, jax.ShapeDtypeStruct((1, HEAD, vsub, LANES), F32)),
        grid=(n_t,),
        in_specs=[spec(0, False), spec(0, True), spec(1, False), spec(2, True), spec(3, False), spec(3, True),
                  pl.BlockSpec((HEAD, LANES), lambda t: (0, 0)), st_spec],
        out_specs=(ospec(False), ospec(True), st_spec),
        scratch_shapes=_scan_scratch(mode, 4),
        compiler_params=_cparams(("arbitrary",)),
        name="hgrn_scan_latent",
    )(view, view, view, view, view, view, lb_s, s0)
    return o_fp, o_bp, o_fl.reshape(N_SAMPLE, width), o_bl.reshape(N_SAMPLE, width), s_fin


def _rwkv_scans(r, w_f, w_b, k, v, kk, a, state0):
    streams = ((r, r), (w_f, w_b), (k, k), (v, v), (kk, kk), (a, a))
    mode = _ScanMode(prompt=True, groups=2, rep=1, r2=0, tb=16)
    n_t = SEQ // mode.tb
    rows = mode.tb * BATCH
    width = PAIRS * LANES
    n_half = D_MODEL // width

    fwd = pl.BlockSpec((rows, width), lambda h, t: (t, h))
    bwd = pl.BlockSpec((rows, width), lambda h, t: (n_t - 1 - t, h))
    st_spec = pl.BlockSpec((None, 2, HEAD, HEAD, LANES), lambda h, t: (h, 0, 0, 0, 0))
    y_shape = jax.ShapeDtypeStruct((N_PROMPT, D_MODEL), F32)
    args = []
    for x_f, x_b in streams:
        args += [x_f, x_b]
    y_fp, y_bp, s_fin = pl.pallas_call(
        functools.partial(_rwkv_scan_kernel, mode=mode, has_s0=False, t_axis=1),
        out_shape=(y_shape, y_shape, jax.ShapeDtypeStruct((n_half, 2, HEAD, HEAD, LANES), F32)),
        grid=(n_half, n_t),
        in_specs=[fwd, bwd] * 6,
        out_specs=(fwd, bwd, st_spec),
        scratch_shapes=_scan_scratch(mode, 6),
        compiler_params=_cparams(("parallel", "arbitrary")),
        name="rwkv_scan_prompt",
    )(*args)
    s_fin = s_fin.reshape(n_half, 2, HEAD, HEAD, 2, PAIRS, BATCH)
    s_fin = jnp.transpose(s_fin, (6, 4, 0, 5, 1, 3, 2)).reshape(BATCH, 2, H_C, N_C, N_C)

    mode = _ScanMode(prompt=False, groups=1, rep=2, r2=H_C, tb=32)
    n_t = DEC_SEQ // mode.tb

    def view(x):
        return x[N_PROMPT:].reshape(DEC_BATCH, DEC_SEQ, H_C, HEAD)

    fwd = pl.BlockSpec((DEC_BATCH, mode.tb, H_C, HEAD), lambda t: (0, t, 0, 0))
    bwd = pl.BlockSpec((DEC_BATCH, mode.tb, H_C, HEAD), lambda t: (0, n_t - 1 - t, 0, 0))
    vsub = HEAD // mode.rep
    s0 = state0.reshape(DEC_BATCH, 2, H_C, mode.rep, vsub, N_C)
    s0 = jnp.transpose(s0, (5, 4, 3, 1, 0, 2)).reshape(1, HEAD, vsub, LANES)
    st_spec = pl.BlockSpec((1, HEAD, vsub, LANES), lambda t: (0, 0, 0, 0))
    y_shape = jax.ShapeDtypeStruct((DEC_BATCH, DEC_SEQ, H_C, HEAD), F32)
    args = []
    for x_f, x_b in streams:
        x_fv = view(x_f)
        args += [x_fv, x_fv if x_b is x_f else view(x_b)]
    y_fl, y_bl, _ = pl.pallas_call(
        functools.partial(_rwkv_scan_kernel, mode=mode, has_s0=True, t_axis=0),
        out_shape=(y_shape, y_shape, jax.ShapeDtypeStruct((1, HEAD, vsub, LANES), F32)),
        grid=(n_t,),
        in_specs=[fwd, bwd] * 6 + [st_spec],
        out_specs=(fwd, bwd, st_spec),
        scratch_shapes=_scan_scratch(mode, 6),
        compiler_params=_cparams(("arbitrary",)),
        name="rwkv_scan_latent",
    )(*args, s0)
    return y_fp, y_bp, y_fl.reshape(N_SAMPLE, D_MODEL), y_bl.reshape(N_SAMPLE, D_MODEL), s_fin


def _ew_spec(width=D_MODEL, col=0):
    return pl.BlockSpec((EW_TILE, width), lambda i: (i, col))


def _prompt_spec(width):
    return pl.BlockSpec((EW_TILE, width), lambda i: (jnp.minimum(i, EW_PROMPT - 1), 0))


def _latent_spec(width):
    return pl.BlockSpec((EW_TILE, width), lambda i: (jnp.maximum(i - EW_PROMPT, 0), 0))


def _hgrn_post_kernel(ofp_ref, obp_ref, ofl_ref, obl_ref, g_ref, gain_ref, bo_ref, o_ref):
    is_prompt = pl.program_id(0) < EW_PROMPT
    o = jnp.where(is_prompt, ofp_ref[...] + obp_ref[...], ofl_ref[...] + obl_ref[...])
    ms = _head_sum(o * o, bo_ref[...]) * (1.0 / DV_A)
    y = o * lax.rsqrt(ms + EPS) * gain_ref[...]
    g = g_ref[...]
    o_ref[...] = (y * (g * jax.nn.sigmoid(g))).astype(o_ref.dtype)


def _hgrn_post(o_fp, o_bp, o_fl, o_bl, slabs, gain):
    w = H_A * DV_A
    return pl.pallas_call(
        _hgrn_post_kernel,
        out_shape=jax.ShapeDtypeStruct((N_TOK, w), BF16),
        grid=(EW_TILES,),
        in_specs=[_prompt_spec(w), _prompt_spec(w), _latent_spec(w), _latent_spec(w),
                  pl.BlockSpec((None, EW_TILE, w), lambda i: (4, i, 0)),
                  _vec_spec(w), pl.BlockSpec((LANES, LANES), lambda i: (0, 0))],
        out_specs=_ew_spec(w),
        compiler_params=_cparams(("parallel",)),
        name="hgrn_post",
    )(o_fp, o_bp, o_fl, o_bl, slabs, jnp.tile(gain, H_A).reshape(1, w), _block_ones())


def _q_up_kernel(cq_ref, gain_ref, w_ref, cos_ref, sin_ref, qn_ref, qr_ref):
    cq = _rms(cq_ref[...], gain_ref[...]).astype(BF16)
    q = jnp.dot(cq, w_ref[...], preferred_element_type=F32)
    qn_ref[...] = q[:, :512].astype(BF16)
    qr_ref[...] = (q[:, 512:768] * cos_ref[...] + q[:, 768:1024] * sin_ref[...]).astype(BF16)


def _q_up(tail, gain, w, cos, sin):
    return pl.pallas_call(
        _q_up_kernel,
        out_shape=(jax.ShapeDtypeStruct((N_TOK, 512), BF16), jax.ShapeDtypeStruct((N_TOK, 256), BF16)),
        grid=(EW_TILES,),
        in_specs=[_ew_spec(Q_RANK, T_CQ // Q_RANK),
                  _vec_spec(Q_RANK), pl.BlockSpec((Q_RANK, 1024), lambda i: (0, 0)),
                  _ew_spec(256), _ew_spec(256)],
        out_specs=(_ew_spec(512), _ew_spec(256)),
        compiler_params=_cparams(("parallel",)),
        name="q_up",
    )(tail, gain.reshape(1, Q_RANK), w, cos, sin)


def _kv_up_kernel(ckv_ref, kr_ref, krot_ref, gain_ref, w_ref, cos_ref, sin_ref, ckvn_ref, kn_ref, v_ref, krope_ref):
    ckv = _rms(ckv_ref[...], gain_ref[...])
    ckvn_ref[...] = ckv
    kv = jnp.dot(ckv.astype(BF16), w_ref[...], preferred_element_type=F32)
    kn_ref[...] = kv[:, :512].astype(BF16)
    v_ref[...] = kv[:, 512:].astype(BF16)
    krope_ref[...] = (kr_ref[...] * cos_ref[...] + krot_ref[...] * sin_ref[...]).astype(BF16)


def _kv_up(tail, gain, w, cos4, sin4):
    return pl.pallas_call(
        _kv_up_kernel,
        out_shape=(jax.ShapeDtypeStruct((N_TOK, KV_RANK), F32), jax.ShapeDtypeStruct((N_TOK, 512), BF16),
                   jax.ShapeDtypeStruct((N_TOK, 512), BF16), jax.ShapeDtypeStruct((N_TOK, LANES), BF16)),
        grid=(EW_TILES,),
        in_specs=[_ew_spec(LANES, T_CKV // LANES), _ew_spec(LANES, T_KR // LANES), _ew_spec(LANES, T_KROT // LANES),
                  _vec_spec(KV_RANK), pl.BlockSpec((KV_RANK, 1024), lambda i: (0, 0)),
                  _ew_spec(LANES), _ew_spec(LANES)],
        out_specs=(_ew_spec(KV_RANK), _ew_spec(512), _ew_spec(512), _ew_spec(LANES)),
        compiler_params=_cparams(("parallel",)),
        name="kv_up",
    )(tail, tail, tail, gain.reshape(1, KV_RANK), w, cos4, sin4)


def _attn_kernel(qn_ref, qr_ref, kn_ref, kr_ref, v_ref, o_ref, *, scale):
    kr = kr_ref[0]
    lane = lax.broadcasted_iota(jnp.int32, (1, LANES), 1)
    zero = jnp.zeros((), BF16)
    for hp in range(H_B // 2):
        cols = slice(hp * LANES, (hp + 1) * LANES)
        qn = qn_ref[0, :, cols]
        qr = qr_ref[0, :, (hp // 2) * LANES:(hp // 2 + 1) * LANES]
        kcat = jnp.concatenate([kn_ref[0, :, cols], kr], axis=-1)
        v = v_ref[0, :, cols]
        outs = []
        for j in range(2):
            mn = (lane // D_NOPE) == j
            mr = (lane // D_ROPE) == (2 * hp + j) % 4
            qcat = jnp.concatenate([jnp.where(mn, qn, zero), jnp.where(mr, qr, zero)], axis=-1)
            s = lax.dot_general(qcat, kcat, (((1,), (1,)), ((), ())), preferred_element_type=F32) * scale
            p = jnp.exp(s - jnp.max(s, axis=-1, keepdims=True))
            l = jnp.sum(p, axis=-1, keepdims=True)
            outs.append(jnp.dot(p.astype(BF16), v, preferred_element_type=F32) / l)
        o_ref[0, :, cols] = jnp.where(lane < DV_B, outs[0], outs[1]).astype(o_ref.dtype)


def _attention(qn, qr, kn, kr, v, bsz, time_major, tq=256):
    t_len, tk = qn.shape[1], kn.shape[1]

    def spec(is_query, width):
        rows = tq if is_query else tk
        if time_major:
            return pl.BlockSpec((1, rows, width), lambda b, q: (0, q if is_query else 0, b))
        return pl.BlockSpec((1, rows, width), lambda b, q: (b, q if is_query else 0, 0))

    w_n, w_r = H_B * D_NOPE, H_B * D_ROPE
    return pl.pallas_call(
        functools.partial(_attn_kernel, scale=float((D_NOPE + D_ROPE) ** -0.5)),
        out_shape=jax.ShapeDtypeStruct(qn.shape, BF16),
        grid=(bsz, t_len // tq),
        in_specs=[spec(True, w_n), spec(True, w_r), spec(False, w_n), spec(False, LANES), spec(False, w_n)],
        out_specs=spec(True, w_n),
        compiler_params=_cparams(("parallel", "parallel")),
        name="attention",
    )(qn, qr, kn, kr, v)


def _odd_pre_kernel(x_ref, xp_ref, xn_ref, g_ref, sc_ref, sh_ref, mu_ref, *o_refs):
    i = pl.program_id(0)
    g, sc, sh = g_ref[...], sc_ref[0], sh_ref[0]

    def nm(x):
        return _rms(x, g) * (1.0 + sc) + sh

    h = nm(x_ref[...])
    hp = nm(xp_ref[...])
    hn = nm(xn_ref[...])
    is_prompt = i < BATCH
    hp_p = jnp.where(i > 0, hp, 0.0)
    hn_p = jnp.where(i < BATCH - 1, hn, 0.0)
    prev_p = jnp.concatenate([hp_p, h[:ROW_TILE - BATCH]], axis=0)
    next_p = jnp.concatenate([h[BATCH:], hn_p], axis=0)
    part = (i - BATCH) % 4
    hp_s = jnp.where(part != 0, hp[HALO - 1:HALO], 0.0)
    hn_s = jnp.where(part != 3, hn[0:1], 0.0)
    row = lax.broadcasted_iota(jnp.int32, (ROW_TILE, 1), 0)
    prev_s = jnp.where(row == 0, hp_s, pltpu.roll(h, 1, axis=0))
    next_s = jnp.where(row == ROW_TILE - 1, hn_s, pltpu.roll(h, ROW_TILE - 1, axis=0))
    prev = jnp.where(is_prompt, prev_p, prev_s)
    nxt = jnp.where(is_prompt, next_p, next_s)
    xx = 0.5 * (prev + nxt) - h
    for j, o_ref in enumerate(o_refs):
        o_ref[...] = (h + xx * mu_ref[j:j + 1, :]).astype(o_ref.dtype)


def _odd_pre(x, g, sc, sh, mu):
    assert HALO == BATCH
    per = ROW_TILE // HALO
    last = N_TOK // HALO - 1
    return pl.pallas_call(
        _odd_pre_kernel,
        out_shape=tuple(jax.ShapeDtypeStruct((N_TOK, D_MODEL), BF16) for _ in range(6)),
        grid=(N_TILES,),
        in_specs=[_row_spec(),
                  pl.BlockSpec((HALO, D_MODEL), lambda i: (jnp.maximum(i * per - 1, 0), 0)),
                  pl.BlockSpec((HALO, D_MODEL), lambda i: (jnp.minimum((i + 1) * per, last), 0)),
                  _vec_spec(), _tile_spec(), _tile_spec(),
                  pl.BlockSpec((8, D_MODEL), lambda i: (0, 0))],
        out_specs=tuple(_row_spec() for _ in range(6)),
        compiler_params=_cparams(("parallel",)),
        name="odd_pre",
    )(x, x, x, g.reshape(1, D_MODEL), sc, sh, jnp.pad(mu, ((0, 2), (0, 0))))


def _rwkv_prep_kernel(k_ref, a_ref, kk_w_ref, ka_w_ref, bo_ref, kk_ref, k2_ref):
    k = k_ref[...]
    kk = k * kk_w_ref[...]
    kk_ref[...] = kk * lax.rsqrt(_head_sum(kk * kk, bo_ref[...]) + 1e-12)
    k2_ref[...] = k * (1.0 + (a_ref[...] - 1.0) * ka_w_ref[...])


def _rwkv_prep(k, a, k_k, k_a):
    return pl.pallas_call(
        _rwkv_prep_kernel,
        out_shape=(jax.ShapeDtypeStruct((N_TOK, D_MODEL), F32), jax.ShapeDtypeStruct((N_TOK, D_MODEL), F32)),
        grid=(EW_TILES,),
        in_specs=[_ew_spec(), _ew_spec(), _vec_spec(), _vec_spec(),
                  pl.BlockSpec((LANES, LANES), lambda i: (0, 0))],
        out_specs=(_ew_spec(), _ew_spec()),
        compiler_params=_cparams(("parallel",)),
        name="rwkv_prep",
    )(k, a, k_k.reshape(1, D_MODEL), k_a.reshape(1, D_MODEL), _block_ones())


def _rwkv_post_kernel(yfp_ref, ybp_ref, yfl_ref, ybl_ref, r_ref, k_ref, v_ref, g_ref, rk_ref, lnw_ref, lnb_ref,
                      bo_ref, o_ref):
    bo = bo_ref[...]
    y = jnp.where(pl.program_id(0) < EW_PROMPT, yfp_ref[...] + ybp_ref[...], yfl_ref[...] + ybl_ref[...])
    d = y - _head_sum(y, bo) * (1.0 / N_C)
    var = _head_sum(d * d, bo) * (1.0 / N_C)
    yn = d * lax.rsqrt(var + GN_EPS) * lnw_ref[...] + lnb_ref[...]
    bonus = _head_sum(r_ref[...] * k_ref[...] * rk_ref[...], bo) * v_ref[...]
    o_ref[...] = ((yn + bonus) * g_ref[...]).astype(o_ref.dtype)


def _rwkv_post(y_fp, y_bp, y_fl, y_bl, r, k2, v, g, r_k, ln_w, ln_b):
    return pl.pallas_call(
        _rwkv_post_kernel,
        out_shape=jax.ShapeDtypeStruct((N_TOK, D_MODEL), BF16),
        grid=(EW_TILES,),
        in_specs=([_prompt_spec(D_MODEL)] * 2 + [_latent_spec(D_MODEL)] * 2 + [_ew_spec()] * 4 + [_vec_spec()] * 3
                  + [pl.BlockSpec((LANES, LANES), lambda i: (0, 0))]),
        out_specs=_ew_spec(),
        compiler_params=_cparams(("parallel",)),
        name="rwkv_post",
    )(y_fp, y_bp, y_fl, y_bl, r, k2, v, g, r_k.reshape(1, D_MODEL), ln_w.reshape(1, D_MODEL),
      ln_b.reshape(1, D_MODEL), _block_ones())


def _rot_cols(w):
    wb = w.reshape(w.shape[:-1] + (2, 2, D_ROPE // 4))
    return jnp.concatenate([-wb[..., 1:, :], wb[..., :1, :]], axis=-2).reshape(w.shape)


def _rope_tables():
    rows = DEC_SEQ // GRID_W
    row, col = np.meshgrid(np.arange(rows), np.arange(GRID_W), indexing='ij')
    row = jnp.asarray(row.reshape(-1), F32)
    col = jnp.asarray(col.reshape(-1), F32)
    n_freq = D_ROPE // 4
    inv = 1.0 / (ROPE_BASE ** (jnp.arange(n_freq, dtype=F32) / n_freq))
    ar = row[:, None] * inv
    ac = col[:, None] * inv
    ang = jnp.concatenate([ar, ar, ac, ac], axis=-1)
    cos, sin = jnp.cos(ang), jnp.sin(ang)
    cos = jnp.concatenate([jnp.ones((N_PROMPT, D_ROPE), F32), jnp.tile(cos, (DEC_BATCH, 1))], axis=0)
    sin = jnp.concatenate([jnp.zeros((N_PROMPT, D_ROPE), F32), jnp.tile(sin, (DEC_BATCH, 1))], axis=0)
    return cos, sin


def _even_mixer(j, h, lb_all, cos, sin, cache_ckv, cache_krope, state_hgrn,
                ev_w_in, hgrn_norm, mla_q_norm, mla_w_q_up, mla_kv_norm, mla_w_kv_up):
    w_in = ev_w_in[j]
    n_a = 5 * H_A * DK_A
    w_kr = w_in[:, n_a + Q_RANK + KV_RANK:]
    w_tail = jnp.concatenate([w_in[:, n_a:n_a + Q_RANK + KV_RANK], jnp.tile(w_kr, (1, 4)),
                              jnp.tile(_rot_cols(w_kr), (1, 4))], axis=1).astype(BF16)
    slabs = _mm_split(h, w_in[:, :n_a].astype(BF16), H_A * DK_A, name="hgrn_proj")
    tail = _mm(h, w_tail, F32, tn=T_N, name="mla_proj")

    o_fp, o_bp, o_fl, o_bl, s_h = _hgrn_scans(slabs, lb_all[j], state_hgrn[:, j])
    o_a = _hgrn_post(o_fp, o_bp, o_fl, o_bl, slabs, hgrn_norm[j])

    wq = mla_w_q_up[j].reshape(Q_RANK, H_B, D_NOPE + D_ROPE)
    wq_n = wq[:, :, :D_NOPE].reshape(Q_RANK, H_B * D_NOPE)
    wq_r = wq[:, :, D_NOPE:]
    wq_aug = jnp.concatenate([wq_n, wq_r.reshape(Q_RANK, -1), _rot_cols(wq_r).reshape(Q_RANK, -1)], axis=1).astype(BF16)
    qn, qr = _q_up(tail, mla_q_norm[j], wq_aug, jnp.tile(cos, (1, H_B)), jnp.tile(sin, (1, H_B)))
    wkv = mla_w_kv_up[j].reshape(KV_RANK, H_B, D_NOPE + DV_B)
    wkv_aug = jnp.concatenate([wkv[:, :, :D_NOPE].reshape(KV_RANK, -1), wkv[:, :, D_NOPE:].reshape(KV_RANK, -1)],
                              axis=1).astype(BF16)
    ckv_n, kn, vv, kr = _kv_up(tail, mla_kv_norm[j], wkv_aug, jnp.tile(cos, (1, 4)), jnp.tile(sin, (1, 4)))
    kv_ctx = _mm(cache_ckv[:, j].reshape(DEC_BATCH * PAST_LEN, KV_RANK), wkv_aug, BF16, name="kv_ctx")
    kn_c = kv_ctx[:, :512].reshape(DEC_BATCH, PAST_LEN, 512)
    v_c = kv_ctx[:, 512:].reshape(DEC_BATCH, PAST_LEN, 512)
    kr_c = jnp.tile(cache_krope[:, j], (1, 1, 4)).astype(BF16)

    def prompt(x):
        return x[:N_PROMPT].reshape(1, SEQ, BATCH * x.shape[-1])

    def latent(x):
        return x[N_PROMPT:].reshape(DEC_BATCH, DEC_SEQ, x.shape[-1])

    o_p = _attention(prompt(qn), prompt(qr), prompt(kn), prompt(kr), prompt(vv), BATCH, True)
    o_s = _attention(latent(qn), latent(qr),
                     jnp.concatenate([kn_c, latent(kn)], axis=1),
                     jnp.concatenate([kr_c, latent(kr)], axis=1),
                     jnp.concatenate([v_c, latent(vv)], axis=1), DEC_BATCH, False)
    o_att = jnp.concatenate([o_p.reshape(N_PROMPT, 512), o_s.reshape(N_SAMPLE, 512)], axis=0)

    new_ckv = jnp.transpose(ckv_n[:N_PROMPT].reshape(SEQ, BATCH, KV_RANK), (1, 0, 2))
    new_krope = jnp.transpose(tail[:N_PROMPT, T_KR:T_KR + D_ROPE].reshape(SEQ, BATCH, D_ROPE), (1, 0, 2))
    return o_a, o_att, (s_h, new_ckv, new_krope)


def _odd_mixer(j, x, g0, sc, sh, state_rwkv, rw_mu, rw_w_r, rw_w_k, rw_w_v, rw_w0, rw_w1, rw_w2, rw_a0, rw_a1, rw_a2,
               rw_g1, rw_g2, rw_k_k, rw_k_a, rw_r_k, rw_ln_w, rw_ln_b):
    xr, xw, xk, xv, xa, xg = _odd_pre(x, g0, sc, sh, rw_mu[j])
    r = _mm(xr, rw_w_r[j].astype(BF16), name="rwkv_r")
    k = _mm(xk, rw_w_k[j].astype(BF16), name="rwkv_k")
    v = _mm(xv, rw_w_v[j].astype(BF16), name="rwkv_v")
    a = _lora(xa, rw_a1[j], rw_a2[j], rw_a0[j], _identity, jax.nn.sigmoid)
    g = _lora(xg, rw_g1[j], rw_g2[j], jnp.zeros((D_MODEL,), F32), jax.nn.sigmoid, _identity)
    w_f = _lora(xw, rw_w1[j, 0], rw_w2[j, 0], rw_w0[j, 0], jnp.tanh, _decay)
    w_b = _lora(xw, rw_w1[j, 1], rw_w2[j, 1], rw_w0[j, 1], jnp.tanh, _decay)
    kk, k2 = _rwkv_prep(k, a, rw_k_k[j], rw_k_a[j])
    y_fp, y_bp, y_fl, y_bl, s_r = _rwkv_scans(r, w_f, w_b, k2, v, kk, a, state_rwkv[:, j])
    mix = _rwkv_post(y_fp, y_bp, y_fl, y_bl, r, k2, v, g, rw_r_k[j], rw_ln_w[j], rw_ln_b[j])
    return mix, s_r


def kernel(x_prompt, x_sample, cache_ckv, cache_krope, state_hgrn, state_rwkv, c, c_ctx, ada_w, ada_b, norm_gains, ev_w_in, hgrn_lb_logits, hgrn_norm, mla_q_norm, mla_w_q_up, mla_kv_norm, mla_w_kv_up, ev_w_out, rw_mu, rw_w_r, rw_w_k, rw_w_v, rw_w_o, rw_w0, rw_w1, rw_w2, rw_a0, rw_a1, rw_a2, rw_g1, rw_g2, rw_k_k, rw_k_a, rw_r_k, rw_ln_w, rw_ln_b, ffn_w_gate, ffn_w_up, ffn_w_down, moe_router, moe_w_gate, moe_w_up, moe_w_down):
    lb_all = jnp.cumsum(jax.nn.softmax(hgrn_lb_logits.astype(F32), axis=0), axis=0)
    lb_all = lb_all - lb_all[:1]
    cos, sin = _rope_tables()

    cond8 = jnp.zeros((8, D_MODEL), F32).at[0].set(c_ctx).at[1:1 + DEC_BATCH].set(c)
    mod = _modulation(cond8, ada_w, ada_b)
    tile_row = np.array([0] * BATCH + [1 + t // 4 for t in range(N_TILES - BATCH)])
    modt = mod[:, tile_row, :].reshape(DEPTH, N_TILES, 6, 1, D_MODEL)

    x = jnp.concatenate([jnp.transpose(x_prompt, (1, 0, 2)).reshape(N_PROMPT, D_MODEL),
                         x_sample.reshape(N_SAMPLE, D_MODEL)], axis=0)
    ckv_l, krope_l, hgrn_l, rwkv_l = [], [], [], []
    for l in range(DEPTH):
        j = l // 2
        sh_m, sc_m, gt_m, sh_f, sc_f, gt_f = [modt[l, :, i] for i in range(6)]
        if l % 2 == 0:
            h = _norm_mod(x, norm_gains[l, 0], sc_m, sh_m)
            o_a, o_att, (s_h, n_ckv, n_kr) = _even_mixer(j, h, lb_all, cos, sin, cache_ckv, cache_krope, state_hgrn,
                                                        ev_w_in, hgrn_norm, mla_q_norm, mla_w_q_up, mla_kv_norm,
                                                        mla_w_kv_up)
            hgrn_l.append(s_h)
            ckv_l.append(n_ckv)
            krope_l.append(n_kr)
            x = _mm2_resid(o_a, o_att, ev_w_out[j].astype(BF16), x, norm_gains[l, 1], gt_m)
            h = _norm_mod(x, norm_gains[l, 2], sc_f, sh_f)
            hid = _swiglu(h, ffn_w_gate[j].astype(BF16), ffn_w_up[j].astype(BF16))
            x = _mm_resid(hid, ffn_w_down[j].astype(BF16), x, norm_gains[l, 3], gt_f)
        else:
            mix, s_r = _odd_mixer(j, x, norm_gains[l, 0], sc_m, sh_m, state_rwkv, rw_mu, rw_w_r, rw_w_k, rw_w_v,
                                  rw_w0, rw_w1, rw_w2, rw_a0, rw_a1, rw_a2, rw_g1, rw_g2, rw_k_k, rw_k_a, rw_r_k,
                                  rw_ln_w, rw_ln_b)
            rwkv_l.append(s_r)
            x = _mm_resid(mix, rw_w_o[j].astype(BF16), x, norm_gains[l, 1], gt_m)
            h, route = _norm_mod_router(x, norm_gains[l, 2], sc_f, sh_f, moe_router[j])
            dest, src_tok, tile_e, n_used = _moe_plan(route)
            h_sorted = _row_gather(src_tok, h, MOE_ROWS, MOE_TM)
            hid = _moe_swiglu(tile_e, n_used, h_sorted, moe_w_gate, moe_w_up, j)
            y_sorted = _moe_down(tile_e, n_used, hid, moe_w_down, j)
            x = _moe_combine(dest, y_sorted, route, x, norm_gains[l, 3], gt_f)

    y_prompt = jnp.transpose(x[:N_PROMPT].reshape(SEQ, BATCH, D_MODEL), (1, 0, 2))
    y_sample = x[N_PROMPT:].reshape(DEC_BATCH, DEC_SEQ, D_MODEL)
    return (y_prompt, y_sample, jnp.stack(ckv_l, axis=1), jnp.stack(krope_l, axis=1),
            jnp.stack(hgrn_l, axis=1), jnp.stack(rwkv_l, axis=1))
```
